```python
import math
import jax, jax.numpy as jnp
from jax import lax
import numpy as np

D_MODEL = 2048
BATCH = 4
SEQ = 4096
DEPTH = 4

EPS = 1e-6
N_MOD = 6
MLSTM_WIDTH = D_MODEL // 2
MLSTM_HEADS = 4
MLSTM_HEAD_DIM = MLSTM_WIDTH // MLSTM_HEADS
MLSTM_CHUNK = 128
MLSTM_CONV = 4
MOBA_WIDTH = D_MODEL - MLSTM_WIDTH
MOBA_HEAD_DIM = 128
MOBA_HEADS = MOBA_WIDTH // MOBA_HEAD_DIM
MOBA_BLOCK = 256
MOBA_TOPK = 3
IN_SPLITS = (MLSTM_WIDTH,) * 4 + (MLSTM_HEADS,) * 2 + (MOBA_WIDTH,) * 3
N_IN = sum(IN_SPLITS)
S5_GROUP = 16
S5_GROUPS = D_MODEL // S5_GROUP
S5_STATE = 64
S5_CHUNK = 128
D_FF = ((8 * D_MODEL // 3 + 127) // 128) * 128
N_EXPERTS = 8
TOP_K = 2
D_FF_EXPERT = 7 * D_MODEL // 2
MOE_BLOCK = 256

kernel_name = "hybrid_mlstm_moba_s5_moe"

F32 = jnp.float32


def rms_norm(x, gain):
    xf = x.astype(F32)
    y = xf * lax.rsqrt(jnp.mean(xf * xf, axis=-1, keepdims=True) + EPS)
    return (y * gain.astype(F32)).astype(x.dtype)


def modulate(h, shift, scale):
    return h * (1 + scale[:, None, :]) + shift[:, None, :]


def causal_depthwise_conv(x, w):
    k = w.shape[0]
    return lax.conv_general_dilated(
        x, w[:, None, :].astype(x.dtype), window_strides=(1,), padding=[(k - 1, 0)],
        dimension_numbers=('NWC', 'WIO', 'NWC'), feature_group_count=x.shape[-1])


def mlstm_chunkwise(q, k, v, ig, lf):
    B, S, H, dh = q.shape
    L = MLSTM_CHUNK
    nc = S // L

    def chunks(a):
        return a.reshape(B, nc, L, H, -1).transpose(1, 0, 3, 2, 4)

    def gchunks(a):
        return a.reshape(B, nc, L, H).transpose(1, 0, 3, 2)

    causal = jnp.tril(jnp.ones((L, L), bool))

    def step(carry, inp):
        c_mat, n_vec, m_prev = carry
        q_, k_, v_, i_, f_ = inp
        g = jnp.cumsum(f_, axis=-1)
        d = jnp.where(causal, g[..., :, None] - g[..., None, :] + i_[..., None, :], -jnp.inf)
        m_inter = g + m_prev[..., None]
        m_t = jnp.maximum(m_inter, jnp.max(d, axis=-1))
        s = jnp.einsum('bhtd,bhsd->bhts', q_, k_) * jnp.exp(d - m_t[..., None])
        decay = jnp.exp(m_inter - m_t)
        num = jnp.einsum('bhts,bhsd->bhtd', s, v_) + decay[..., None] * jnp.einsum('bhvk,bhtk->bhtv', c_mat, q_)
        den = jnp.sum(s, axis=-1) + decay * jnp.einsum('bhk,bhtk->bht', n_vec, q_)
        h = num / jnp.maximum(jnp.abs(den), jnp.exp(-m_t))[..., None]
        g_last = g[..., -1]
        a = g_last[..., None] - g + i_
        m_new = jnp.maximum(g_last + m_prev, jnp.max(a, axis=-1))
        w = jnp.exp(a - m_new[..., None])
        carry_decay = jnp.exp(g_last + m_prev - m_new)
        c_mat = carry_decay[..., None, None] * c_mat + jnp.einsum('bhs,bhsv,bhsk->bhvk', w, v_, k_)
        n_vec = carry_decay[..., None] * n_vec + jnp.einsum('bhs,bhsk->bhk', w, k_)
        return (c_mat, n_vec, m_new), h

    init = (jnp.zeros((B, H, dh, dh), F32), jnp.zeros((B, H, dh), F32), jnp.full((B, H), -1e30, F32))
    _, h = lax.scan(step, init, (chunks(q), chunks(k), chunks(v), gchunks(ig), gchunks(lf)))
    return h.transpose(1, 0, 3, 2, 4).reshape(B, S, H, dh)


def moba_attention(q, k, v):
    B, S, H, dh = q.shape
    nb = -(-S // MOBA_BLOCK)
    s_pad = nb * MOBA_BLOCK
    pad = ((0, 0), (0, s_pad - S), (0, 0), (0, 0))
    q, k, v = jnp.pad(q, pad), jnp.pad(k, pad), jnp.pad(v, pad)
    bh = B * H
    qf = q.transpose(0, 2, 1, 3).reshape(bh, s_pad, dh)
    kf = k.transpose(0, 2, 1, 3).reshape(bh * nb, MOBA_BLOCK, dh)
    vf = v.transpose(0, 2, 1, 3).reshape(bh * nb, MOBA_BLOCK, dh)
    k_mean = jnp.mean(kf.astype(F32), axis=1).reshape(bh, nb, dh)
    q_blk = jnp.arange(s_pad) // MOBA_BLOCK
    past = jnp.arange(nb)[None, :] < q_blk[:, None]
    gate = jnp.where(past, jnp.einsum('ntd,nbd->ntb', qf.astype(F32), k_mean), -jnp.inf)
    kk = min(MOBA_TOPK, nb)
    _, sel = lax.top_k(gate, kk)
    slot_ok = jnp.arange(kk)[None, :] < q_blk[:, None]
    sel = jnp.where(slot_ok, sel, q_blk[:, None])
    blocks = jnp.concatenate([sel, jnp.broadcast_to(q_blk[None, :, None], (bh, s_pad, 1))], axis=-1)
    blocks = blocks + (jnp.arange(bh) * nb)[:, None, None]
    n_slots = kk + 1
    is_own = jnp.arange(n_slots) == kk
    causal = jnp.arange(MOBA_BLOCK)[None, :] <= jnp.arange(MOBA_BLOCK)[:, None]
    scale = dh ** -0.5

    def attend(args):
        q_c, blk_c, j = args
        k_g = kf[blk_c]
        v_g = vf[blk_c]
        s = jnp.einsum('td,tjkd->tjk', q_c, k_g).astype(F32) * scale
        mask = jnp.where(is_own[None, :, None], causal[:, None, :], (jnp.arange(n_slots) < j)[None, :, None])
        s = jnp.where(mask, s, -jnp.inf)
        p = jax.nn.softmax(s.reshape(MOBA_BLOCK, -1), axis=-1).reshape(s.shape)
        return jnp.einsum('tjk,tjkd->td', p.astype(v_g.dtype), v_g)

    j_ids = jnp.broadcast_to(jnp.arange(nb)[None, :], (bh, nb)).reshape(-1)
    out = lax.map(attend, (qf.reshape(bh * nb, MOBA_BLOCK, dh), blocks.reshape(bh * nb, MOBA_BLOCK, n_slots), j_ids))
    return out.reshape(B, H, s_pad, dh).transpose(0, 2, 1, 3)[:, :S]


def mlstm_moba_mixer(h, w_in, conv_w, b_igate, b_fgate, mh_gain, w_out):
    B, S, _ = h.shape
    proj = h @ w_in
    q_m, k_m, v_m, o_m, i_pre, f_pre, q_b, k_b, v_b = jnp.split(proj, list(np.cumsum(IN_SPLITS)[:-1]), axis=-1)
    qk = jax.nn.silu(causal_depthwise_conv(jnp.concatenate([q_m, k_m], axis=-1), conv_w))
    q_m, k_m = jnp.split(qk, 2, axis=-1)
    qm = q_m.reshape(B, S, MLSTM_HEADS, MLSTM_HEAD_DIM).astype(F32)
    km = k_m.reshape(B, S, MLSTM_HEADS, MLSTM_HEAD_DIM).astype(F32) * (MLSTM_HEAD_DIM ** -0.5)
    vm = v_m.reshape(B, S, MLSTM_HEADS, MLSTM_HEAD_DIM).astype(F32)
    ig = (i_pre + b_igate).astype(F32)
    lf = jax.nn.log_sigmoid((f_pre + b_fgate).astype(F32))
    hm = mlstm_chunkwise(qm, km, vm, ig, lf)
    hm = hm * lax.rsqrt(jnp.mean(hm * hm, axis=-1, keepdims=True) + EPS) * mh_gain.astype(F32).reshape(MLSTM_HEADS, MLSTM_HEAD_DIM)
    hm = (hm.reshape(B, S, MLSTM_WIDTH) * jax.nn.sigmoid(o_m.astype(F32))).astype(h.dtype)
    hb = moba_attention(q_b.reshape(B, S, MOBA_HEADS, MOBA_HEAD_DIM), k_b.reshape(B, S, MOBA_HEADS, MOBA_HEAD_DIM),
                        v_b.reshape(B, S, MOBA_HEADS, MOBA_HEAD_DIM)).reshape(B, S, MOBA_WIDTH)
    return jnp.concatenate([hm, hb], axis=-1) @ w_out


def s5_scan(u, lam_re, lam_im, log_step, b_re, b_im, c_re, c_im, d_skip):
    B, S, D = u.shape
    uf = u.astype(F32)
    lam = lax.complex(lam_re.astype(F32), lam_im.astype(F32))
    lam_dt = lam * jnp.exp(log_step.astype(F32))[:, None]
    lam_bar = jnp.exp(lam_dt)
    b_bar = ((lam_bar - 1.0) / lam)[:, :, None] * lax.complex(b_re.astype(F32), b_im.astype(F32))
    c_mat = lax.complex(c_re.astype(F32), c_im.astype(F32))
    nc = S // S5_CHUNK
    u_chunks = uf.reshape(B, nc, S5_CHUNK, S5_GROUPS, S5_GROUP).transpose(1, 2, 0, 3, 4)
    tau = jnp.arange(1, S5_CHUNK + 1, dtype=F32)
    carry_pows = jnp.exp(lam_dt[None] * tau[:, None, None])
    a_elems = jnp.broadcast_to(lam_bar, (S5_CHUNK, B, S5_GROUPS, S5_STATE))

    def combine(e1, e2):
        a1, x1 = e1
        a2, x2 = e2
        return a1 * a2, a2 * x1 + x2

    def chunk_step(state, u_c):
        bu = jnp.einsum('gpn,lbgn->lbgp', b_bar, u_c.astype(jnp.complex64))
        _, xs = lax.associative_scan(combine, (a_elems, bu), axis=0)
        xs = xs + carry_pows[:, None] * state[None]
        y = jnp.einsum('gnp,lbgp->lbgn', c_mat, xs).real
        return xs[-1], y

    _, y = lax.scan(chunk_step, jnp.zeros((B, S5_GROUPS, S5_STATE), jnp.complex64), u_chunks)
    y = y.transpose(2, 0, 1, 3, 4).reshape(B, S, D)
    return y + d_skip.astype(F32) * uf


def s5_mixer(h, lam_re, lam_im, log_step, b_re, b_im, c_re, c_im, d_skip, glu_w_a, glu_w_b):
    g = jax.nn.gelu(s5_scan(h, lam_re, lam_im, log_step, b_re, b_im, c_re, c_im, d_skip)).astype(h.dtype)
    return (g @ glu_w_a) * jax.nn.sigmoid(g @ glu_w_b)


def swiglu(h, w_gate, w_up, w_down):
    return (jax.nn.silu(h @ w_gate) * (h @ w_up)) @ w_down


def moe_swiglu(h, router_w, router_b, w_gate, w_up, w_down):
    T, D = h.shape
    logits = h.astype(F32) @ router_w.astype(F32) + router_b.astype(F32)
    top_logit, top_e = lax.top_k(logits, TOP_K)
    gates = jax.nn.softmax(top_logit, axis=-1)
    n_assign = T * TOP_K
    e_flat = top_e.reshape(-1)
    tok_flat = jnp.repeat(jnp.arange(T, dtype=jnp.int32), TOP_K)
    order = jnp.argsort(e_flat)
    e_sorted = e_flat[order]
    tok_sorted = tok_flat[order]
    gate_sorted = gates.reshape(-1)[order]
    counts = jnp.bincount(e_flat, length=N_EXPERTS)
    padded = (counts + MOE_BLOCK - 1) // MOE_BLOCK * MOE_BLOCK
    pad_end = jnp.cumsum(padded)
    pad_start = pad_end - padded
    grp_start = jnp.cumsum(counts) - counts
    dest = pad_start[e_sorted] + jnp.arange(n_assign) - grp_start[e_sorted]
    n_blocks = -(-n_assign // MOE_BLOCK) + N_EXPERTS
    row_tok = jnp.full((n_blocks * MOE_BLOCK,), T, jnp.int32).at[dest].set(tok_sorted)
    block_expert = jnp.minimum(jnp.searchsorted(pad_end, jnp.arange(n_blocks) * MOE_BLOCK, side='right'), N_EXPERTS - 1)
    h_pad = jnp.concatenate([h, jnp.zeros((1, D), h.dtype)], axis=0)

    def expert_block(args):
        tok, e = args
        xb = h_pad[tok]
        return (jax.nn.silu(xb @ w_gate[e]) * (xb @ w_up[e])) @ w_down[e]

    y_rows = lax.map(expert_block, (row_tok.reshape(n_blocks, MOE_BLOCK), block_expert)).reshape(-1, D)
    y_assign = y_rows[dest] * gate_sorted[:, None].astype(y_rows.dtype)
    return jax.ops.segment_sum(y_assign, tok_sorted, num_segments=T)


def setup_inputs(seed: int = 0) -> dict:
    key = jax.random.key(seed)
    ks = iter(jax.random.split(key, 40))
    D = D_MODEL
    ne = (DEPTH + 1) // 2
    no = DEPTH // 2
    G, P, N, E = S5_GROUPS, S5_STATE, S5_GROUP, N_EXPERTS

    def nrm(shape, scale):
        return jax.random.normal(next(ks), shape, F32) * scale

    inputs = {}
    inputs['x'] = nrm((BATCH, SEQ, D), 1.0)
    inputs['c'] = nrm((BATCH, D), 1.0)
    inputs['w_ada'] = nrm((DEPTH, D, N_MOD * D), 0.5 * D ** -0.5)
    inputs['b_ada'] = nrm((DEPTH, N_MOD * D), 0.02)
    inputs['g_mix'] = 1.0 + nrm((DEPTH, D), 0.02)
    inputs['g_ffn'] = 1.0 + nrm((DEPTH, D), 0.02)
    inputs['g_final'] = 1.0 + nrm((D,), 0.02)
    inputs['w_in'] = nrm((ne, D, N_IN), D ** -0.5)
    inputs['conv_w'] = nrm((ne, MLSTM_CONV, 2 * MLSTM_WIDTH), MLSTM_CONV ** -0.5)
    inputs['b_igate'] = nrm((ne, MLSTM_HEADS), 0.1)
    inputs['b_fgate'] = jnp.linspace(3.0, 6.0, MLSTM_HEADS, dtype=F32)[None, :] + nrm((ne, MLSTM_HEADS), 0.1)
    inputs['mh_gain'] = 1.0 + nrm((ne, MLSTM_WIDTH), 0.02)
    inputs['w_out'] = nrm((ne, MLSTM_WIDTH + MOBA_WIDTH, D), D ** -0.5)
    inputs['ffn_w_gate'] = nrm((ne, D, D_FF), D ** -0.5)
    inputs['ffn_w_up'] = nrm((ne, D, D_FF), D ** -0.5)
    inputs['ffn_w_down'] = nrm((ne, D_FF, D), D_FF ** -0.5)
    inputs['s5_lam_re'] = -0.5 + nrm((no, G, P), 0.01)
    inputs['s5_lam_im'] = math.pi * jnp.arange(P, dtype=F32)[None, None, :] + nrm((no, G, P), 0.01)
    inputs['s5_log_step'] = jax.random.uniform(next(ks), (no, G), F32, math.log(1e-3), math.log(1e-1))
    inputs['s5_b_re'] = nrm((no, G, P, N), (2 * N) ** -0.5)
    inputs['s5_b_im'] = nrm((no, G, P, N), (2 * N) ** -0.5)
    inputs['s5_c_re'] = nrm((no, G, N, P), P ** -0.5)
    inputs['s5_c_im'] = nrm((no, G, N, P), P ** -0.5)
    inputs['s5_d'] = nrm((no, D), 1.0)
    inputs['glu_w_a'] = nrm((no, D, D), D ** -0.5)
    inputs['glu_w_b'] = nrm((no, D, D), D ** -0.5)
    inputs['router_w'] = nrm((no, D, E), D ** -0.5)
    inputs['router_b'] = nrm((no, E), 0.01)
    inputs['exp_w_gate'] = nrm((no, E, D, D_FF_EXPERT), D ** -0.5)
    inputs['exp_w_up'] = nrm((no, E, D, D_FF_EXPERT), D ** -0.5)
    inputs['exp_w_down'] = nrm((no, E, D_FF_EXPERT, D), D_FF_EXPERT ** -0.5)
    return inputs


def reference(x, c, w_ada, b_ada, g_mix, g_ffn, g_final, w_in, conv_w, b_igate, b_fgate, mh_gain, w_out,
              ffn_w_gate, ffn_w_up, ffn_w_down, s5_lam_re, s5_lam_im, s5_log_step, s5_b_re, s5_b_im,
              s5_c_re, s5_c_im, s5_d, glu_w_a, glu_w_b, router_w, router_b, exp_w_gate, exp_w_up, exp_w_down):
    B, S, D = x.shape
    cond = jax.nn.silu(c)
    for layer in range(DEPTH):
        i = layer // 2
        mod = cond @ w_ada[layer] + b_ada[layer]
        sh1, sc1, gt1, sh2, sc2, gt2 = jnp.split(mod, N_MOD, axis=-1)
        h = modulate(rms_norm(x, g_mix[layer]), sh1, sc1)
        if layer % 2 == 0:
            mix = mlstm_moba_mixer(h, w_in[i], conv_w[i], b_igate[i], b_fgate[i], mh_gain[i], w_out[i])
        else:
            mix = s5_mixer(h, s5_lam_re[i], s5_lam_im[i], s5_log_step[i], s5_b_re[i], s5_b_im[i],
                           s5_c_re[i], s5_c_im[i], s5_d[i], glu_w_a[i], glu_w_b[i])
        x = x + gt1[:, None, :] * mix
        h = modulate(rms_norm(x, g_ffn[layer]), sh2, sc2)
        if layer % 2 == 0:
            ff = swiglu(h, ffn_w_gate[i], ffn_w_up[i], ffn_w_down[i])
        else:
            ff = moe_swiglu(h.reshape(B * S, D), router_w[i], router_b[i], exp_w_gate[i], exp_w_up[i],
                            exp_w_down[i]).reshape(B, S, D)
        x = x + gt2[:, None, :] * ff
    return rms_norm(x, g_final)
```

```python
import functools
import math

import jax
import jax.numpy as jnp
from jax import lax
from jax.experimental import pallas as pl
from jax.experimental.pallas import tpu as pltpu

F32 = jnp.float32
BF16 = jnp.bfloat16
HI = lax.Precision.HIGHEST
NEG_INF = float("-inf")

EPS = 1e-6
N_MOD = 6
MLSTM_HEADS = 4
MLSTM_CHUNK = 128
MLSTM_CONV = 4
MOBA_HEAD_DIM = 128
MOBA_BLOCK = 256
MOBA_TOPK = 3
S5_GROUP = 16
S5_STATE = 64
S5_SUB = 16
S5_TILE_GROUPS = 8
N_EXPERTS = 8
TOP_K = 2
MOE_ROWS = 512
LANES = 128


def _params(*sem):
    return pltpu.CompilerParams(dimension_semantics=sem)


def _dot(a, b):
    return jnp.dot(a, b, preferred_element_type=F32)


def _dot_nt(a, b, precision=None):
    return lax.dot_general(a, b, (((1,), (1,)), ((), ())), precision=precision,
                           preferred_element_type=F32)


def _silu(x):
    return x * jax.nn.sigmoid(x)


def _norm_mod(x, gain, shift, scale):
    ms = jnp.mean(x * x, axis=-1, keepdims=True)
    y = x * lax.rsqrt(ms + EPS) * gain
    return y * (1.0 + scale) + shift


def _ada_kernel(c_ref, w_ref, b_ref, o_ref):
    c = c_ref[...]
    o_ref[0] = jnp.dot(_silu(c), w_ref[0], precision=HI, preferred_element_type=F32) + b_ref[0]


def ada_mod(c, w_ada, b_ada, tn=1024):
    depth, d, n = w_ada.shape
    b = c.shape[0]
    bp = 8
    cp = jnp.pad(c, ((0, bp - b), (0, 0)))
    out = pl.pallas_call(
        _ada_kernel,
        grid=(depth, n // tn),
        in_specs=[pl.BlockSpec((bp, d), lambda l, j: (0, 0)),
                  pl.BlockSpec((1, d, tn), lambda l, j: (l, 0, j)),
                  pl.BlockSpec((1, 1, tn), lambda l, j: (l, 0, j))],
        out_specs=pl.BlockSpec((1, bp, tn), lambda l, j: (l, 0, j)),
        out_shape=jax.ShapeDtypeStruct((depth, bp, n), F32),
        compiler_params=_params("parallel", "parallel"),
        name="ada_mod",
    )(cp, w_ada, b_ada.reshape(depth, 1, n))
    return out[:, :b].reshape(depth, b, N_MOD, d)


def _inproj_kernel(x_ref, g_ref, mod_ref, w_ref, wg_ref, o_ref, og_ref, h_ref):
    @pl.when(pl.program_id(1) == 0)
    def _():
        m = mod_ref[0]
        h = _norm_mod(x_ref[...], g_ref[...], m[0:1], m[1:2])
        h_ref[...] = h.astype(BF16)
        og_ref[...] = jnp.dot(h, wg_ref[...], precision=HI, preferred_element_type=F32)

    o_ref[...] = _dot(h_ref[...], w_ref[...]).astype(BF16)


def in_proj(x, gain, mod, w_big, w_gates, rows_per_batch, tm=512, tn=512):
    t, d = x.shape
    n = w_big.shape[1]
    tm = min(tm, rows_per_batch)
    tn = min(tn, n)
    tpb = rows_per_batch // tm
    return pl.pallas_call(
        _inproj_kernel,
        grid=(t // tm, n // tn),
        in_specs=[pl.BlockSpec((tm, d), lambda i, j: (i, 0)),
                  pl.BlockSpec((1, d), lambda i, j: (0, 0)),
                  pl.BlockSpec((1, N_MOD, d), lambda i, j: (i // tpb, 0, 0)),
                  pl.BlockSpec((d, tn), lambda i, j: (0, j)),
                  pl.BlockSpec((d, LANES), lambda i, j: (0, 0))],
        out_specs=[pl.BlockSpec((tm, tn), lambda i, j: (i, j)),
                   pl.BlockSpec((tm, LANES), lambda i, j: (i, 0))],
        out_shape=[jax.ShapeDtypeStruct((t, n), BF16), jax.ShapeDtypeStruct((t, LANES), F32)],
        scratch_shapes=[pltpu.VMEM((tm, d), BF16)],
        compiler_params=_params("parallel", "arbitrary"),
        name="in_proj",
    )(x, gain.reshape(1, d), mod, w_big, w_gates)


def _mlstm_kernel(bias_ref, q_ref, k_ref, v_ref, o_ref, gi_ref, gf_ref, cwq_ref, cwk_ref, gain_ref,
                  out_ref, qbuf, kbuf, c_st, n_st, m_st):
    head = pl.program_id(1)
    chunk = pl.program_id(2)
    L, dh = q_ref.shape
    taps = cwq_ref.shape[0]
    halo = 8

    @pl.when(chunk == 0)
    def _():
        qbuf[0:halo] = jnp.zeros((halo, dh), F32)
        kbuf[0:halo] = jnp.zeros((halo, dh), F32)
        c_st[...] = jnp.zeros_like(c_st)
        n_st[...] = jnp.zeros_like(n_st)
        m_st[...] = jnp.full(m_st.shape, -1e30, F32)

    def conv_silu(src_ref, buf, w_ref):
        buf[halo:halo + L] = src_ref[...].astype(F32)
        w = w_ref[...]
        acc = buf[halo:halo + L] * w[taps - 1:taps]
        for d in range(1, taps):
            acc = acc + buf[pl.ds(halo - d, L), :] * w[taps - 1 - d:taps - d]
        buf[0:halo] = buf[L:L + halo]
        return _silu(acc)

    q = conv_silu(q_ref, qbuf, cwq_ref)
    k = conv_silu(k_ref, kbuf, cwk_ref) * (dh ** -0.5)
    vb = v_ref[...]

    ig_row = gi_ref[0, 0, 0] + bias_ref[0, head]
    fz = gf_ref[0, 0, 0] + bias_ref[1, head]
    lf_row = jnp.minimum(fz, 0.0) - jnp.log(1.0 + jnp.exp(-jnp.abs(fz)))

    row = lax.broadcasted_iota(jnp.int32, (L, L), 0)
    col = lax.broadcasted_iota(jnp.int32, (L, L), 1)
    eye = row == col

    def to_col(x_row):
        return jnp.sum(jnp.where(eye, jnp.broadcast_to(x_row, (L, L)), 0.0), axis=1, keepdims=True)

    upper = (row <= col).astype(F32)
    g_row = jnp.dot(jnp.broadcast_to(lf_row, (8, L)), upper, precision=HI,
                    preferred_element_type=F32)[0:1]
    g_col = to_col(g_row)
    b_row = ig_row - g_row
    d_mat = jnp.where(col <= row, g_col + b_row, NEG_INF)
    m_prev = m_st[...]
    m_inter = g_col + m_prev
    m_t = jnp.maximum(m_inter, jnp.max(d_mat, axis=1, keepdims=True))
    qb = q.astype(BF16)
    kb = k.astype(BF16)
    s = _dot_nt(qb, kb) * jnp.exp(d_mat - m_t)
    decay = jnp.exp(m_inter - m_t)
    num = _dot(s.astype(BF16), vb) + decay * _dot(qb, c_st[...].astype(BF16))
    den = jnp.sum(s, axis=1, keepdims=True) + decay * jnp.sum(q * n_st[...], axis=1, keepdims=True)
    hh = num / jnp.maximum(jnp.abs(den), jnp.exp(-m_t))

    g_last = g_row[:, L - 1:L]
    a_row = g_last + b_row
    m_new = jnp.maximum(g_last + m_prev, jnp.max(a_row, axis=1, keepdims=True))
    w_col = to_col(jnp.exp(a_row - m_new))
    carry = jnp.exp(g_last + m_prev - m_new)
    kw = k * w_col
    c_st[...] = carry * c_st[...] + _dot(kw.T.astype(BF16), vb)
    n_st[...] = carry * n_st[...] + jnp.sum(kw, axis=0, keepdims=True)
    m_st[...] = m_new

    hn = hh * lax.rsqrt(jnp.mean(hh * hh, axis=1, keepdims=True) + EPS) * gain_ref[...]
    out_ref[...] = (hn * jax.nn.sigmoid(o_ref[...].astype(F32))).astype(BF16)


def mlstm_mix(proj, gates, conv_w, b_igate, b_fgate, mh_gain, batch, seq):
    heads, L = MLSTM_HEADS, MLSTM_CHUNK
    width = mh_gain.shape[0]
    dh = width // heads
    nc = seq // L
    t = batch * seq

    def rows(a):
        return a.reshape(batch, nc, L, heads).transpose(0, 3, 1, 2).reshape(batch, heads, nc, 1, L)

    gi = rows(gates[:, 0:heads])
    gf = rows(gates[:, heads:2 * heads])
    bias = jnp.stack([b_igate, b_fgate]).astype(F32)
    blk = lambda off: pl.BlockSpec((L, dh), lambda b, h, c: (b * nc + c, off * heads + h))
    gspec = pl.BlockSpec((1, 1, 1, 1, L), lambda b, h, c: (b, h, c, 0, 0))
    return pl.pallas_call(
        _mlstm_kernel,
        grid=(batch, heads, nc),
        in_specs=[pl.BlockSpec(memory_space=pltpu.SMEM),
                  blk(0), blk(1), blk(2), blk(3), gspec, gspec,
                  pl.BlockSpec((MLSTM_CONV, dh), lambda b, h, c: (0, h)),
                  pl.BlockSpec((MLSTM_CONV, dh), lambda b, h, c: (0, heads + h)),
                  pl.BlockSpec((1, dh), lambda b, h, c: (0, h))],
        out_specs=pl.BlockSpec((L, dh), lambda b, h, c: (b * nc + c, h)),
        out_shape=jax.ShapeDtypeStruct((t, width), BF16),
        scratch_shapes=[pltpu.VMEM((L + 8, dh), F32), pltpu.VMEM((L + 8, dh), F32),
                        pltpu.VMEM((dh, dh), F32), pltpu.VMEM((1, dh), F32), pltpu.VMEM((1, 1), F32)],
        compiler_params=_params("parallel", "parallel", "arbitrary"),
        name="mlstm",
    )(bias, proj, proj, proj, proj, gi, gf, conv_w, conv_w, mh_gain.reshape(1, width))


def _moba_kernel(q_ref, k_ref, v_ref, o_ref, kmean_ref):
    j = pl.program_id(1)
    blk, dh = q_ref.shape
    nb = k_ref.shape[0] // blk
    scale = dh ** -0.5

    @pl.when(j == 0)
    def _():
        kmean_ref[...] = jnp.zeros_like(kmean_ref)
        for b in range(nb):
            kmean_ref[b:b + 1, :] = jnp.mean(k_ref[b * blk:(b + 1) * blk, :].astype(F32), axis=0,
                                             keepdims=True)

    q = q_ref[...]
    gate = _dot_nt(q.astype(F32), kmean_ref[...], precision=HI)
    lane = lax.broadcasted_iota(jnp.int32, gate.shape, 1)
    valid = lane < j
    sc = jnp.where(valid, gate, NEG_INF)
    beaten = jnp.zeros(gate.shape, F32)
    for b2 in range(nb):
        other = sc[:, b2:b2 + 1]
        wins = (other > sc) | ((other == sc) & (b2 < lane))
        beaten = beaten + wins.astype(F32)
    chosen = jnp.where(valid & (beaten < MOBA_TOPK), 1.0, 0.0)

    row = lax.broadcasted_iota(jnp.int32, (blk, blk), 0)
    col = lax.broadcasted_iota(jnp.int32, (blk, blk), 1)
    start = pl.multiple_of(j * blk, blk)
    s = _dot_nt(q, k_ref[pl.ds(start, blk), :]) * scale
    s = jnp.where(col <= row, s, NEG_INF)
    m0 = jnp.max(s, axis=1, keepdims=True)
    p = jnp.exp(s - m0)
    l0 = jnp.sum(p, axis=1, keepdims=True)
    acc0 = _dot(p.astype(BF16), v_ref[pl.ds(start, blk), :])

    def body(kb, carry):
        m, l, acc = carry
        off = pl.multiple_of(kb * blk, blk)
        s = _dot_nt(q, k_ref[pl.ds(off, blk), :]) * scale
        picked = jnp.sum(jnp.where(lane == kb, chosen, 0.0), axis=1, keepdims=True) > 0.5
        s = jnp.where(picked, s, NEG_INF)
        m_new = jnp.maximum(m, jnp.max(s, axis=1, keepdims=True))
        alpha = jnp.exp(m - m_new)
        p = jnp.exp(s - m_new)
        l = alpha * l + jnp.sum(p, axis=1, keepdims=True)
        acc = alpha * acc + _dot(p.astype(BF16), v_ref[pl.ds(off, blk), :])
        return m_new, l, acc

    _, l, acc = lax.fori_loop(0, j, body, (m0, l0, acc0))
    o_ref[...] = (acc / l).astype(BF16)


def moba_mix(proj, col0, batch, seq, heads):
    dh, blk = MOBA_HEAD_DIM, MOBA_BLOCK
    nb = seq // blk
    t = batch * seq
    return pl.pallas_call(
        _moba_kernel,
        grid=(batch * heads, nb),
        in_specs=[pl.BlockSpec((blk, dh), lambda g, j: ((g // heads) * nb + j, col0 + g % heads)),
                  pl.BlockSpec((seq, dh), lambda g, j: (g // heads, col0 + heads + g % heads)),
                  pl.BlockSpec((seq, dh), lambda g, j: (g // heads, col0 + 2 * heads + g % heads))],
        out_specs=pl.BlockSpec((blk, dh), lambda g, j: ((g // heads) * nb + j, g % heads)),
        out_shape=jax.ShapeDtypeStruct((t, heads * dh), BF16),
        scratch_shapes=[pltpu.VMEM((LANES, dh), F32)],
        compiler_params=_params("parallel", "arbitrary"),
        name="moba",
    )(proj, proj, proj)


def _outproj_kernel(hm_ref, hb_ref, w1_ref, w2_ref, x_ref, mod_ref, o_ref):
    acc = _dot(hm_ref[...], w1_ref[...]) + _dot(hb_ref[...], w2_ref[...])
    o_ref[...] = x_ref[...] + mod_ref[0][2:3] * acc


def out_proj(hm, hb, w_out, x, mod, rows_per_batch, tm=512):
    t, d = x.shape
    k1, k2 = hm.shape[1], hb.shape[1]
    tm = min(tm, rows_per_batch)
    tpb = rows_per_batch // tm
    return pl.pallas_call(
        _outproj_kernel,
        grid=(t // tm,),
        in_specs=[pl.BlockSpec((tm, k1), lambda i: (i, 0)),
                  pl.BlockSpec((tm, k2), lambda i: (i, 0)),
                  pl.BlockSpec((k1, d), lambda i: (0, 0)),
                  pl.BlockSpec((k2, d), lambda i: (0, 0)),
                  pl.BlockSpec((tm, d), lambda i: (i, 0)),
                  pl.BlockSpec((1, N_MOD, d), lambda i: (i // tpb, 0, 0))],
        out_specs=pl.BlockSpec((tm, d), lambda i: (i, 0)),
        out_shape=jax.ShapeDtypeStruct((t, d), F32),
        compiler_params=_params("parallel"),
        name="out_proj",
    )(hm, hb, w_out[:k1], w_out[k1:], x, mod)


def _ffn_kernel(x_ref, g_ref, mod_ref, wg_ref, wu_ref, wd_ref, o_ref, h_ref, acc_ref):
    f = pl.program_id(1)

    @pl.when(f == 0)
    def _():
        m = mod_ref[0]
        h_ref[...] = _norm_mod(x_ref[...], g_ref[...], m[3:4], m[4:5]).astype(BF16)
        acc_ref[...] = jnp.zeros_like(acc_ref)

    h = h_ref[...]
    act = (_silu(_dot(h, wg_ref[...])) * _dot(h, wu_ref[...])).astype(BF16)
    acc_ref[...] += _dot(act, wd_ref[...])

    @pl.when(f == pl.num_programs(1) - 1)
    def _():
        o_ref[...] = x_ref[...] + mod_ref[0][5:6] * acc_ref[...]


def ffn_swiglu(x, gain, mod, w_gate, w_up, w_down, rows_per_batch, tm=512, tf=512):
    t, d = x.shape
    f_dim = w_gate.shape[1]
    tm = min(tm, rows_per_batch)
    tf = min(tf, f_dim)
    tpb = rows_per_batch // tm
    return pl.pallas_call(
        _ffn_kernel,
        grid=(t // tm, f_dim // tf),
        in_specs=[pl.BlockSpec((tm, d), lambda i, f: (i, 0)),
                  pl.BlockSpec((1, d), lambda i, f: (0, 0)),
                  pl.BlockSpec((1, N_MOD, d), lambda i, f: (i // tpb, 0, 0)),
                  pl.BlockSpec((d, tf), lambda i, f: (0, f)),
                  pl.BlockSpec((d, tf), lambda i, f: (0, f)),
                  pl.BlockSpec((tf, d), lambda i, f: (f, 0))],
        out_specs=pl.BlockSpec((tm, d), lambda i, f: (i, 0)),
        out_shape=jax.ShapeDtypeStruct((t, d), F32),
        scratch_shapes=[pltpu.VMEM((tm, d), BF16), pltpu.VMEM((tm, d), F32)],
        compiler_params=_params("parallel", "arbitrary"),
        name="ffn_swiglu",
    )(x, gain.reshape(1, d), mod, w_gate, w_up, w_down)


def _normmod_kernel(x_ref, g_ref, mod_ref, o_ref):
    m = mod_ref[0]
    o_ref[...] = _norm_mod(x_ref[...], g_ref[...], m[0:1], m[1:2])


def norm_mod(x, gain, mod, rows_per_batch, tm=512):
    t, d = x.shape
    tm = min(tm, rows_per_batch)
    tpb = rows_per_batch // tm
    return pl.pallas_call(
        _normmod_kernel,
        grid=(t // tm,),
        in_specs=[pl.BlockSpec((tm, d), lambda i: (i, 0)),
                  pl.BlockSpec((1, d), lambda i: (0, 0)),
                  pl.BlockSpec((1, N_MOD, d), lambda i: (i // tpb, 0, 0))],
        out_specs=pl.BlockSpec((tm, d), lambda i: (i, 0)),
        out_shape=jax.ShapeDtypeStruct((t, d), F32),
        compiler_params=_params("parallel"),
        name="norm_mod",
    )(x, gain.reshape(1, d), mod)


def s5_operators(lam_re, lam_im, log_step, b_re, b_im, c_re, c_im):
    g_all, p = lam_re.shape
    n = b_re.shape[-1]
    sub, tg = S5_SUB, S5_TILE_GROUPS
    nt = g_all // tg
    lam = lax.complex(lam_re.astype(F32), lam_im.astype(F32))
    lam_dt = lam * jnp.exp(log_step.astype(F32))[:, None]
    lam_bar = jnp.exp(lam_dt)
    b_bar = ((lam_bar - 1.0) / lam)[:, :, None] * lax.complex(b_re.astype(F32), b_im.astype(F32))
    c_mat = lax.complex(c_re.astype(F32), c_im.astype(F32))
    tau = jnp.arange(sub + 1, dtype=F32)
    pw = jnp.exp(lam_dt[None] * tau[:, None, None])

    wb = pw[:sub, :, :, None] * b_bar[None]
    taps = (jnp.einsum('gmp,tgpn->tgnm', c_mat.real, wb.real, precision=HI)
            - jnp.einsum('gmp,tgpn->tgnm', c_mat.imag, wb.imag, precision=HI))
    eye_g = jnp.eye(tg, dtype=F32)
    kd = jnp.einsum('tqgnm,gh->qtgnhm', taps.reshape(sub, nt, tg, n, n), eye_g)
    kd = kd.reshape(nt, sub, tg * n, tg * n)
    krev = kd[:, ::-1].reshape(nt, sub * tg * n, tg * n).astype(BF16)

    par = (jnp.arange(tg) % 2)[:, None] == jnp.arange(2)[None, :]
    wb_in = wb[::-1].reshape(sub, nt, tg, p, n)

    def lay_in(a):
        z = jnp.einsum('lqgpn,gr->qlgnrp', a, par.astype(F32))
        return z.reshape(nt, sub, tg * n, 2 * p).astype(BF16)

    b_in_re, b_in_im = lay_in(wb_in.real), lay_in(wb_in.imag)

    e_out = c_mat[None] * pw[1:sub + 1, :, None, :]
    e_out = e_out.reshape(sub, nt, tg, n, p)

    def lay_out(a):
        z = jnp.einsum('lqgmp,gr->qlrpgm', a, par.astype(F32))
        return z.reshape(nt, sub, 2 * p, tg * n).astype(BF16)

    c_out_re, c_out_im = lay_out(e_out.real), lay_out(-e_out.imag)

    n_lvl = 16
    lv = jnp.exp(lam_dt[None] * (sub * 2.0 ** jnp.arange(n_lvl, dtype=F32))[:, None, None])
    lv = lv.reshape(n_lvl, nt, tg * p)
    lam_lv = jnp.stack([lv.real, lv.imag], axis=2).transpose(1, 0, 2, 3)
    return krev, b_in_re, b_in_im, c_out_re, c_out_im, lam_lv


def _gelu_tanh(x):
    return 0.5 * x * (1.0 + jnp.tanh(0.7978845608028654 * (x + 0.044715 * (x * x * x))))


def _s5_kernel(u_ref, krev_ref, bre_ref, bim_ref, cre_ref, cim_ref, lam_ref, d_ref, o_ref,
               ucat, bcat, ccat):
    sub = bre_ref.shape[1]
    cw = u_ref.shape[1]
    r = u_ref.shape[0] // sub
    sw = bcat.shape[1] // 2
    pair = bre_ref.shape[3]

    @pl.when(pl.program_id(1) == 0)
    def _():
        rb = lax.broadcasted_iota(jnp.int32, (cw, sw), 0) // (2 * S5_GROUP)
        cb = lax.broadcasted_iota(jnp.int32, (cw, sw), 1) // pair
        in_mask = rb == cb
        rc = lax.broadcasted_iota(jnp.int32, (sw, cw), 0) // pair
        cc = lax.broadcasted_iota(jnp.int32, (sw, cw), 1) // (2 * S5_GROUP)
        out_mask = rc == cc
        reps = sw // pair

        def expand(ref, l, axis, mask):
            tiled = jnp.concatenate([ref[0, l].astype(F32)] * reps, axis=axis)
            return jnp.where(mask, tiled, 0.0).astype(BF16)

        for l in range(sub):
            bcat[l * cw:(l + 1) * cw, 0:sw] = expand(bre_ref, l, 1, in_mask)
            bcat[l * cw:(l + 1) * cw, sw:2 * sw] = expand(bim_ref, l, 1, in_mask)
            ccat[l, 0:sw, :] = expand(cre_ref, l, 0, out_mask)
            ccat[l, sw:2 * sw, :] = expand(cim_ref, l, 0, out_mask)

    for l in range(sub):
        ucat[:, l * cw:(l + 1) * cw] = u_ref[pl.ds(l, r, stride=sub), :].astype(BF16)

    v = _dot(ucat[...], bcat[...])
    s_re, s_im = v[:, 0:sw], v[:, sw:2 * sw]
    rowi = lax.broadcasted_iota(jnp.int32, (r, sw), 0)
    shift, lvl = 1, 0
    while shift < r:
        lr = lam_ref[0, lvl, 0:1, :]
        li = lam_ref[0, lvl, 1:2, :]
        keep = rowi >= shift
        p_re = jnp.where(keep, pltpu.roll(s_re, shift, axis=0), 0.0)
        p_im = jnp.where(keep, pltpu.roll(s_im, shift, axis=0), 0.0)
        s_re, s_im = s_re + lr * p_re - li * p_im, s_im + lr * p_im + li * p_re
        shift, lvl = shift * 2, lvl + 1
    first = rowi >= 1
    x_re = jnp.where(first, pltpu.roll(s_re, 1, axis=0), 0.0)
    x_im = jnp.where(first, pltpu.roll(s_im, 1, axis=0), 0.0)
    xb = jnp.concatenate([x_re, x_im], axis=1).astype(BF16)

    for l in range(sub):
        y = _dot(ucat[:, 0:(l + 1) * cw], krev_ref[0, (sub - 1 - l) * cw:sub * cw, :])
        y = y + _dot(xb, ccat[l])
        ul = u_ref[pl.ds(l, r, stride=sub), :]
        o_ref[pl.ds(l, r, stride=sub), :] = _gelu_tanh(y + d_ref[...] * ul)


def s5_scan_gelu(u, ops, d_skip, batch, seq):
    krev, bre, bim, cre, cim, lam_lv = ops
    t, d = u.shape
    nt, sub = bre.shape[0], bre.shape[1]
    cw = bre.shape[2]
    pair = bre.shape[3]
    sw = S5_TILE_GROUPS * S5_STATE
    n_lvl = lam_lv.shape[1]
    op4 = lambda a, b_: pl.BlockSpec((1, sub, a, b_), lambda c, b: (c, 0, 0, 0))
    return pl.pallas_call(
        _s5_kernel,
        grid=(nt, batch),
        in_specs=[pl.BlockSpec((seq, cw), lambda c, b: (b, c)),
                  pl.BlockSpec((1, sub * cw, cw), lambda c, b: (c, 0, 0)),
                  op4(cw, pair), op4(cw, pair), op4(pair, cw), op4(pair, cw),
                  pl.BlockSpec((1, n_lvl, 2, sw), lambda c, b: (c, 0, 0, 0)),
                  pl.BlockSpec((1, cw), lambda c, b: (0, c))],
        out_specs=pl.BlockSpec((seq, cw), lambda c, b: (b, c)),
        out_shape=jax.ShapeDtypeStruct((t, d), F32),
        scratch_shapes=[pltpu.VMEM((seq // sub, sub * cw), BF16),
                        pltpu.VMEM((sub * cw, 2 * sw), BF16),
                        pltpu.VMEM((sub, 2 * sw, cw), BF16)],
        compiler_params=_params("parallel", "arbitrary"),
        name="s5_scan",
    )(u, krev, bre, bim, cre, cim, lam_lv, d_skip.reshape(1, d))


def _glu_kernel(g_ref, wa_ref, wb_ref, x_ref, mod_ref, o_ref):
    g = g_ref[...].astype(BF16)
    mix = _dot(g, wa_ref[...]) * jax.nn.sigmoid(_dot(g, wb_ref[...]))
    o_ref[...] = x_ref[...] + mod_ref[0][2:3] * mix


def glu_out(g, w_a, w_b, x, mod, rows_per_batch, tm=512, tn=1024):
    t, d = x.shape
    tm = min(tm, rows_per_batch)
    tn = min(tn, d)
    tpb = rows_per_batch // tm
    return pl.pallas_call(
        _glu_kernel,
        grid=(d // tn, t // tm),
        in_specs=[pl.BlockSpec((tm, d), lambda j, i: (i, 0)),
                  pl.BlockSpec((d, tn), lambda j, i: (0, j)),
                  pl.BlockSpec((d, tn), lambda j, i: (0, j)),
                  pl.BlockSpec((tm, tn), lambda j, i: (i, j)),
                  pl.BlockSpec((1, N_MOD, tn), lambda j, i: (i // tpb, 0, j))],
        out_specs=pl.BlockSpec((tm, tn), lambda j, i: (i, j)),
        out_shape=jax.ShapeDtypeStruct((t, d), F32),
        compiler_params=_params("parallel", "parallel"),
        name="glu_out",
    )(g, w_a, w_b, x, mod)


def _router_kernel(x_ref, g_ref, mod_ref, rw_ref, rb_ref, h_ref, r_ref):
    m = mod_ref[0]
    h = _norm_mod(x_ref[...], g_ref[...], m[3:4], m[4:5])
    h_ref[...] = h.astype(BF16)
    logits = jnp.dot(h, rw_ref[...], precision=HI, preferred_element_type=F32) + rb_ref[...]
    lane = lax.broadcasted_iota(jnp.int32, logits.shape, 1)
    logits = jnp.where(lane < N_EXPERTS, logits, NEG_INF)
    m1 = jnp.max(logits, axis=1, keepdims=True)
    i1 = jnp.min(jnp.where(logits == m1, lane, LANES), axis=1, keepdims=True)
    rest = jnp.where(lane == i1, NEG_INF, logits)
    m2 = jnp.max(rest, axis=1, keepdims=True)
    i2 = jnp.min(jnp.where(rest == m2, lane, LANES), axis=1, keepdims=True)
    e2 = jnp.exp(m2 - m1)
    g1 = 1.0 / (1.0 + e2)
    g2 = e2 / (1.0 + e2)
    r_ref[...] = jnp.where(lane == 0, i1.astype(F32),
                           jnp.where(lane == 1, i2.astype(F32),
                                     jnp.where(lane == 2, g1, jnp.where(lane == 3, g2, 0.0))))


def moe_router(x, gain, mod, router_w, router_b, rows_per_batch, tm=512):
    t, d = x.shape
    e = router_w.shape[1]
    tm = min(tm, rows_per_batch)
    tpb = rows_per_batch // tm
    rw = jnp.pad(router_w.astype(F32), ((0, 0), (0, LANES - e)))
    rb = jnp.pad(router_b.astype(F32), (0, LANES - e)).reshape(1, LANES)
    return pl.pallas_call(
        _router_kernel,
        grid=(t // tm,),
        in_specs=[pl.BlockSpec((tm, d), lambda i: (i, 0)),
                  pl.BlockSpec((1, d), lambda i: (0, 0)),
                  pl.BlockSpec((1, N_MOD, d), lambda i: (i // tpb, 0, 0)),
                  pl.BlockSpec((d, LANES), lambda i: (0, 0)),
                  pl.BlockSpec((1, LANES), lambda i: (0, 0))],
        out_specs=[pl.BlockSpec((tm, d), lambda i: (i, 0)),
                   pl.BlockSpec((tm, LANES), lambda i: (i, 0))],
        out_shape=[jax.ShapeDtypeStruct((t, d), BF16), jax.ShapeDtypeStruct((t, LANES), F32)],
        compiler_params=_params("parallel"),
        name="moe_router",
    )(x, gain.reshape(1, d), mod, rw, rb)


def moe_dispatch(top_e, rows):
    t = top_e.shape[0]
    n_assign = t * TOP_K
    n_blocks = -(-n_assign // rows) + N_EXPERTS
    e_flat = top_e.reshape(-1)
    onehot = (e_flat[:, None] == jnp.arange(N_EXPERTS, dtype=jnp.int32)[None, :]).astype(jnp.int32)
    csum = jnp.cumsum(onehot, axis=0)
    rank = jnp.sum((csum - onehot) * onehot, axis=1)
    counts = csum[-1]
    padded = (counts + rows - 1) // rows * rows
    pad_end = jnp.cumsum(padded)
    pad_start = pad_end - padded
    dest = jnp.sum(onehot * pad_start[None, :], axis=1) + rank
    tok = jnp.arange(n_assign, dtype=jnp.int32) // TOP_K
    row_tok = jnp.full((n_blocks * rows,), t, jnp.int32).at[dest].set(tok)
    n_active = (pad_end[-1] // rows).astype(jnp.int32)
    blk_start = jnp.minimum(jnp.arange(n_blocks, dtype=jnp.int32), n_active - 1) * rows
    block_expert = jnp.minimum(jnp.searchsorted(pad_end, blk_start, side='right'),
                               N_EXPERTS - 1).astype(jnp.int32)
    return row_tok, dest.reshape(t, TOP_K), block_expert, n_active.reshape(1)


def _expert_kernel(be_ref, na_ref, x_ref, wg_ref, wu_ref, wd_ref, o_ref, acc_ref):
    i = pl.program_id(0)
    f = pl.program_id(1)

    @pl.when(f == 0)
    def _():
        acc_ref[...] = jnp.zeros_like(acc_ref)

    @pl.when(i < na_ref[0])
    def _():
        x = x_ref[...]
        a = _dot(x, wg_ref[0].astype(BF16))
        u = _dot(x, wu_ref[0].astype(BF16))
        act = (_silu(a) * u).astype(BF16)
        acc_ref[...] += _dot(act, wd_ref[0].astype(BF16))

    @pl.when(f == pl.num_programs(1) - 1)
    def _():
        o_ref[...] = acc_ref[...].astype(BF16)


def moe_experts(xg, block_expert, n_active, w_gate, w_up, w_down, rows, tf=256):
    r_tot, d = xg.shape
    f_dim = w_gate.shape[2]
    tf = min(tf, f_dim)
    nf = f_dim // tf
    n_blocks = r_tot // rows

    def f_idx(i, f, na):
        return jnp.where(i < na[0], f, nf - 1)

    grid_spec = pltpu.PrefetchScalarGridSpec(
        num_scalar_prefetch=2,
        grid=(n_blocks, nf),
        in_specs=[pl.BlockSpec((rows, d), lambda i, f, be, na: (jnp.minimum(i, na[0] - 1), 0)),
                  pl.BlockSpec((1, d, tf), lambda i, f, be, na: (be[i], 0, f_idx(i, f, na))),
                  pl.BlockSpec((1, d, tf), lambda i, f, be, na: (be[i], 0, f_idx(i, f, na))),
                  pl.BlockSpec((1, tf, d), lambda i, f, be, na: (be[i], f_idx(i, f, na), 0))],
        out_specs=pl.BlockSpec((rows, d), lambda i, f, be, na: (i, 0)),
        scratch_shapes=[pltpu.VMEM((rows, d), F32)],
    )
    return pl.pallas_call(
        _expert_kernel,
        grid_spec=grid_spec,
        out_shape=jax.ShapeDtypeStruct((r_tot, d), BF16),
        compiler_params=_params("arbitrary", "arbitrary"),
        name="moe_experts",
    )(block_expert, n_active, xg, w_gate, w_up, w_down)


def _combine_kernel(x_ref, y1_ref, y2_ref, r_ref, mod_ref, gf_ref, o_ref, *, final_norm):
    r = r_ref[...]
    ff = r[:, 2:3] * y1_ref[...].astype(F32) + r[:, 3:4] * y2_ref[...].astype(F32)
    x_new = x_ref[...] + mod_ref[0][5:6] * ff
    if final_norm:
        ms = jnp.mean(x_new * x_new, axis=-1, keepdims=True)
        x_new = x_new * lax.rsqrt(ms + EPS) * gf_ref[...]
    o_ref[...] = x_new


def moe_combine(x, y1, y2, route, mod, g_final, rows_per_batch, final_norm, tm=512):
    t, d = x.shape
    tm = min(tm, rows_per_batch)
    tpb = rows_per_batch // tm
    row = lambda w: pl.BlockSpec((tm, w), lambda i: (i, 0))
    return pl.pallas_call(
        functools.partial(_combine_kernel, final_norm=final_norm),
        grid=(t // tm,),
        in_specs=[row(d), row(d), row(d), row(LANES),
                  pl.BlockSpec((1, N_MOD, d), lambda i: (i // tpb, 0, 0)),
                  pl.BlockSpec((1, d), lambda i: (0, 0))],
        out_specs=row(d),
        out_shape=jax.ShapeDtypeStruct((t, d), F32),
        compiler_params=_params("parallel"),
        name="moe_combine",
    )(x, y1, y2, route, mod, g_final.reshape(1, d))


def _pad_cols(w, mult):
    n = w.shape[-1]
    return jnp.pad(w, ((0, 0), (0, -n % mult)))


def kernel(x, c, w_ada, b_ada, g_mix, g_ffn, g_final, w_in, conv_w, b_igate, b_fgate, mh_gain, w_out,
           ffn_w_gate, ffn_w_up, ffn_w_down, s5_lam_re, s5_lam_im, s5_log_step, s5_b_re, s5_b_im,
           s5_c_re, s5_c_im, s5_d, glu_w_a, glu_w_b, router_w, router_b, exp_w_gate, exp_w_up, exp_w_down):
    batch, seq, d = x.shape
    depth = w_ada.shape[0]
    t = batch * seq
    m_width = mh_gain.shape[1]
    heads_b = (d - m_width) // MOBA_HEAD_DIM
    n_gate = 2 * MLSTM_HEADS

    mods = ada_mod(c, w_ada, b_ada)
    xs = x.reshape(t, d)
    for layer in range(depth):
        i = layer // 2
        mod = mods[layer]
        if layer % 2 == 0:
            w = w_in[i]
            g0 = 4 * m_width
            w_big = jnp.concatenate([w[:, :g0], w[:, g0 + n_gate:]], axis=1).astype(BF16)
            w_gates = jnp.pad(w[:, g0:g0 + n_gate], ((0, 0), (0, LANES - n_gate)))
            proj, gates = in_proj(xs, g_mix[layer], mod, w_big, w_gates, seq)
            hm = mlstm_mix(proj, gates, conv_w[i], b_igate[i], b_fgate[i], mh_gain[i], batch, seq)
            hb = moba_mix(proj, g0 // MOBA_HEAD_DIM, batch, seq, heads_b)
            xs = out_proj(hm, hb, w_out[i].astype(BF16), xs, mod, seq)
            f_mult = 512
            wg = _pad_cols(ffn_w_gate[i], f_mult).astype(BF16)
            wu = _pad_cols(ffn_w_up[i], f_mult).astype(BF16)
            wd = _pad_cols(ffn_w_down[i].T, f_mult).T.astype(BF16)
            xs = ffn_swiglu(xs, g_ffn[layer], mod, wg, wu, wd, seq)
        else:
            u = norm_mod(xs, g_mix[layer], mod, seq)
            ops = s5_operators(s5_lam_re[i], s5_lam_im[i], s5_log_step[i], s5_b_re[i], s5_b_im[i],
                               s5_c_re[i], s5_c_im[i])
            g = s5_scan_gelu(u, ops, s5_d[i], batch, seq)
            xs = glu_out(g, glu_w_a[i].astype(BF16), glu_w_b[i].astype(BF16), xs, mod, seq)
            h, route = moe_router(xs, g_ffn[layer], mod, router_w[i], router_b[i], seq)
            top_e = route[:, 0:TOP_K].astype(jnp.int32)
            row_tok, pos, block_expert, n_active = moe_dispatch(top_e, MOE_ROWS)
            h_pad = jnp.concatenate([h, jnp.zeros((1, d), h.dtype)], axis=0)
            y_rows = moe_experts(h_pad[row_tok], block_expert, n_active,
                                 exp_w_gate[i], exp_w_up[i], exp_w_down[i], MOE_ROWS)
            xs = moe_combine(xs, y_rows[pos[:, 0]], y_rows[pos[:, 1]], route, mod, g_final, seq,
                             final_norm=(layer == depth - 1))
    if depth % 2 == 1:
        raise NotImplementedError("final norm is fused into the last (odd) layer")
    return xs.reshape(batch, seq, d)
```

```python
import functools
import math

import jax
import jax.numpy as jnp
from jax import lax
from jax.experimental import pallas as pl
from jax.experimental.pallas import tpu as pltpu

F32 = jnp.float32
BF16 = jnp.bfloat16
HI = lax.Precision.HIGHEST
NEG_INF = float("-inf")

EPS = 1e-6
N_MOD = 6
MLSTM_HEADS = 4
MLSTM_CHUNK = 128
MLSTM_CONV = 4
MOBA_HEAD_DIM = 128
MOBA_BLOCK = 256
MOBA_TOPK = 3
S5_GROUP = 16
S5_STATE = 64
S5_SUB = 16
S5_TILE_GROUPS = 8
N_EXPERTS = 8
TOP_K = 2
MOE_ROWS = 1024
MOE_SUB_ROWS = 512
LANES = 128


def _tile(n, pref, align=8):
    for cand in range(min(pref, n), 0, -1):
        if n % cand == 0 and cand % align == 0:
            return cand
    raise ValueError(f"no {align}-aligned tile divides {n}")


def _params(*sem):
    return pltpu.CompilerParams(dimension_semantics=sem)


def _dot(a, b):
    return jnp.dot(a, b, preferred_element_type=F32)


def _dot_nt(a, b, precision=None):
    return lax.dot_general(a, b, (((1,), (1,)), ((), ())), precision=precision,
                           preferred_element_type=F32)


def _silu(x):
    return x * jax.nn.sigmoid(x)


def _norm_mod(x, gain, shift, scale):
    ms = jnp.mean(x * x, axis=-1, keepdims=True)
    y = x * lax.rsqrt(ms + EPS) * gain
    return y * (1.0 + scale) + shift


def _ada_kernel(c_ref, w_ref, b_ref, o_ref):
    c = c_ref[...]
    o_ref[0] = jnp.dot(_silu(c), w_ref[0], precision=HI, preferred_element_type=F32) + b_ref[0]


def ada_mod(c, w_ada, b_ada, tn=1024):
    depth, d, n = w_ada.shape
    b = c.shape[0]
    bp = 8
    cp = jnp.pad(c, ((0, bp - b), (0, 0)))
    out = pl.pallas_call(
        _ada_kernel,
        grid=(depth, n // tn),
        in_specs=[pl.BlockSpec((bp, d), lambda l, j: (0, 0)),
                  pl.BlockSpec((1, d, tn), lambda l, j: (l, 0, j)),
                  pl.BlockSpec((1, 1, tn), lambda l, j: (l, 0, j))],
        out_specs=pl.BlockSpec((1, bp, tn), lambda l, j: (l, 0, j)),
        out_shape=jax.ShapeDtypeStruct((depth, bp, n), F32),
        compiler_params=_params("parallel", "parallel"),
        name="ada_mod",
    )(cp, w_ada, b_ada.reshape(depth, 1, n))
    return out[:, :b].reshape(depth, b, N_MOD, d)


def _inproj_kernel(x_ref, g_ref, mod_ref, w_ref, wg_ref, o_ref, og_ref, h_ref):
    @pl.when(pl.program_id(1) == 0)
    def _():
        m = mod_ref[0]
        h = _norm_mod(x_ref[...], g_ref[...], m[0:1], m[1:2])
        h_ref[...] = h.astype(BF16)
        og_ref[...] = jnp.dot(h, wg_ref[...], precision=HI, preferred_element_type=F32)

    o_ref[...] = _dot(h_ref[...], w_ref[...]).astype(BF16)


def in_proj(x, gain, mod, w_big, w_gates, rows_per_batch, tm=1024, tn=1024):
    t, d = x.shape
    n = w_big.shape[1]
    tm = _tile(rows_per_batch, tm)
    tn = min(tn, n)
    tpb = rows_per_batch // tm
    return pl.pallas_call(
        _inproj_kernel,
        grid=(t // tm, n // tn),
        in_specs=[pl.BlockSpec((tm, d), lambda i, j: (i, 0)),
                  pl.BlockSpec((1, d), lambda i, j: (0, 0)),
                  pl.BlockSpec((1, N_MOD, d), lambda i, j: (i // tpb, 0, 0)),
                  pl.BlockSpec((d, tn), lambda i, j: (0, j)),
                  pl.BlockSpec((d, LANES), lambda i, j: (0, 0))],
        out_specs=[pl.BlockSpec((tm, tn), lambda i, j: (i, j)),
                   pl.BlockSpec((tm, LANES), lambda i, j: (i, 0))],
        out_shape=[jax.ShapeDtypeStruct((t, n), BF16), jax.ShapeDtypeStruct((t, LANES), F32)],
        scratch_shapes=[pltpu.VMEM((tm, d), BF16)],
        compiler_params=_params("parallel", "arbitrary"),
        name="in_proj",
    )(x, gain.reshape(1, d), mod, w_big, w_gates)


def _mlstm_kernel(bias_ref, q_ref, k_ref, v_ref, o_ref, gi_ref, gf_ref, cwq_ref, cwk_ref, gain_ref,
                  out_ref, qbuf, kbuf, c_st, n_st, m_st):
    head = pl.program_id(1)
    chunk = pl.program_id(2)
    L, dh = q_ref.shape
    taps = cwq_ref.shape[0]
    halo = 8

    @pl.when(chunk == 0)
    def _():
        qbuf[0:halo] = jnp.zeros((halo, dh), F32)
        kbuf[0:halo] = jnp.zeros((halo, dh), F32)
        c_st[...] = jnp.zeros_like(c_st)
        n_st[...] = jnp.zeros_like(n_st)
        m_st[...] = jnp.full(m_st.shape, -1e30, F32)

    def conv_silu(src_ref, buf, w_ref):
        buf[halo:halo + L] = src_ref[...].astype(F32)
        w = w_ref[...]
        acc = buf[halo:halo + L] * w[taps - 1:taps]
        for d in range(1, taps):
            acc = acc + buf[pl.ds(halo - d, L), :] * w[taps - 1 - d:taps - d]
        buf[0:halo] = buf[L:L + halo]
        return _silu(acc)

    q = conv_silu(q_ref, qbuf, cwq_ref)
    k = conv_silu(k_ref, kbuf, cwk_ref) * (dh ** -0.5)
    vb = v_ref[...]

    ig_row = gi_ref[0, 0, 0] + bias_ref[0, head]
    fz = gf_ref[0, 0, 0] + bias_ref[1, head]
    lf_row = jnp.minimum(fz, 0.0) - jnp.log(1.0 + jnp.exp(-jnp.abs(fz)))

    row = lax.broadcasted_iota(jnp.int32, (L, L), 0)
    col = lax.broadcasted_iota(jnp.int32, (L, L), 1)
    eye = row == col

    def to_col(x_row):
        return jnp.sum(jnp.where(eye, jnp.broadcast_to(x_row, (L, L)), 0.0), axis=1, keepdims=True)

    upper = (row <= col).astype(F32)
    g_row = jnp.dot(jnp.broadcast_to(lf_row, (8, L)), upper, precision=HI,
                    preferred_element_type=F32)[0:1]
    g_col = to_col(g_row)
    b_row = ig_row - g_row
    d_mat = jnp.where(col <= row, g_col + b_row, NEG_INF)
    m_prev = m_st[...]
    m_inter = g_col + m_prev
    m_t = jnp.maximum(m_inter, jnp.max(d_mat, axis=1, keepdims=True))
    qb = q.astype(BF16)
    kb = k.astype(BF16)
    s = _dot_nt(qb, kb) * jnp.exp(d_mat - m_t)
    decay = jnp.exp(m_inter - m_t)
    num = _dot(s.astype(BF16), vb) + decay * _dot(qb, c_st[...].astype(BF16))
    den = jnp.sum(s, axis=1, keepdims=True) + decay * jnp.sum(q * n_st[...], axis=1, keepdims=True)
    hh = num / jnp.maximum(jnp.abs(den), jnp.exp(-m_t))

    g_last = g_row[:, L - 1:L]
    a_row = g_last + b_row
    m_new = jnp.maximum(g_last + m_prev, jnp.max(a_row, axis=1, keepdims=True))
    w_col = to_col(jnp.exp(a_row - m_new))
    carry = jnp.exp(g_last + m_prev - m_new)
    kw = k * w_col
    c_st[...] = carry * c_st[...] + _dot(kw.T.astype(BF16), vb)
    n_st[...] = carry * n_st[...] + jnp.sum(kw, axis=0, keepdims=True)
    m_st[...] = m_new

    hn = hh * lax.rsqrt(jnp.mean(hh * hh, axis=1, keepdims=True) + EPS) * gain_ref[...]
    out_ref[...] = (hn * jax.nn.sigmoid(o_ref[...].astype(F32))).astype(BF16)


def mlstm_mix(proj, gates, conv_w, b_igate, b_fgate, mh_gain, batch, seq):
    heads, L = MLSTM_HEADS, MLSTM_CHUNK
    width = mh_gain.shape[0]
    dh = width // heads
    nc = seq // L
    t = batch * seq

    def rows(a):
        return a.reshape(batch, nc, L, heads).transpose(0, 3, 1, 2).reshape(batch, heads, nc, 1, L)

    gi = rows(gates[:, 0:heads])
    gf = rows(gates[:, heads:2 * heads])
    bias = jnp.stack([b_igate, b_fgate]).astype(F32)
    blk = lambda off: pl.BlockSpec((L, dh), lambda b, h, c: (b * nc + c, off * heads + h))
    gspec = pl.BlockSpec((1, 1, 1, 1, L), lambda b, h, c: (b, h, c, 0, 0))
    return pl.pallas_call(
        _mlstm_kernel,
        grid=(batch, heads, nc),
        in_specs=[pl.BlockSpec(memory_space=pltpu.SMEM),
                  blk(0), blk(1), blk(2), blk(3), gspec, gspec,
                  pl.BlockSpec((MLSTM_CONV, dh), lambda b, h, c: (0, h)),
                  pl.BlockSpec((MLSTM_CONV, dh), lambda b, h, c: (0, heads + h)),
                  pl.BlockSpec((1, dh), lambda b, h, c: (0, h))],
        out_specs=pl.BlockSpec((L, dh), lambda b, h, c: (b * nc + c, h)),
        out_shape=jax.ShapeDtypeStruct((t, width), BF16),
        scratch_shapes=[pltpu.VMEM((L + 8, dh), F32), pltpu.VMEM((L + 8, dh), F32),
                        pltpu.VMEM((dh, dh), F32), pltpu.VMEM((1, dh), F32), pltpu.VMEM((1, 1), F32)],
        compiler_params=_params("parallel", "parallel", "arbitrary"),
        name="mlstm",
    )(bias, proj, proj, proj, proj, gi, gf, conv_w, conv_w, mh_gain.reshape(1, width))


MOBA_MASK_BIAS = -1e9
MOBA_WIDTH_STEP = 4


def _moba_kernel(q_ref, k_ref, v_ref, o_ref, kmean_ref, kaug_ref):
    j = pl.program_id(1)
    blk, dh = q_ref.shape
    seq = k_ref.shape[0]
    nb = seq // blk
    nbp = kmean_ref.shape[0]
    scale = dh ** -0.5

    @pl.when(j == 0)
    def _():
        kmean_ref[...] = jnp.zeros_like(kmean_ref)
        for b in range(nb):
            kmean_ref[b:b + 1, :] = jnp.mean(k_ref[b * blk:(b + 1) * blk, :].astype(F32), axis=0,
                                             keepdims=True)
        kaug_ref[:, 0:dh] = k_ref[...]
        key_blk = lax.broadcasted_iota(jnp.int32, (seq, LANES), 0) // blk
        lane = lax.broadcasted_iota(jnp.int32, (seq, LANES), 1)
        kaug_ref[:, dh:dh + LANES] = jnp.where(key_blk == lane, 1.0, 0.0).astype(BF16)

    q = q_ref[...]
    gate_t = _dot_nt(kmean_ref[...], q.astype(F32), precision=HI)
    blk_id = lax.broadcasted_iota(jnp.int32, gate_t.shape, 0)
    valid = blk_id < j
    sc = jnp.where(valid, gate_t, NEG_INF)
    beaten = jnp.zeros(gate_t.shape, F32)
    for b2 in range(nb):
        other = sc[b2:b2 + 1, :]
        wins = (other > sc) | ((other == sc) & (b2 < blk_id))
        beaten = beaten + wins.astype(F32)
    chosen = valid & (beaten < MOBA_TOPK)
    bias_t = jnp.where(chosen, 0.0, MOBA_MASK_BIAS)
    bias = jnp.concatenate([bias_t, jnp.zeros((LANES - nbp, blk), F32)], axis=0).T
    q_aug = jnp.concatenate([q, bias.astype(BF16)], axis=1)

    row = lax.broadcasted_iota(jnp.int32, (blk, blk), 0)
    col = lax.broadcasted_iota(jnp.int32, (blk, blk), 1)
    start = pl.multiple_of(j * blk, blk)
    s_own = _dot_nt(q, k_ref[pl.ds(start, blk), :]) * scale
    s_own = jnp.where(col <= row, s_own, NEG_INF)
    m_own = jnp.max(s_own, axis=1, keepdims=True)
    v_own = v_ref[pl.ds(start, blk), :]

    @pl.when(j == 0)
    def _():
        p = jnp.exp(s_own - m_own)
        o_ref[...] = (_dot(p.astype(BF16), v_own) / jnp.sum(p, axis=1, keepdims=True)).astype(BF16)

    def attend(n_blocks):
        w = n_blocks * blk
        s = _dot_nt(q_aug, kaug_ref[0:w, :]) * scale
        m = jnp.maximum(jnp.max(s, axis=1, keepdims=True), m_own)
        p = jnp.exp(s - m)
        p_own = jnp.exp(s_own - m)
        l = jnp.sum(p, axis=1, keepdims=True) + jnp.sum(p_own, axis=1, keepdims=True)
        acc = _dot(p.astype(BF16), v_ref[0:w, :]) + _dot(p_own.astype(BF16), v_own)
        o_ref[...] = (acc / l).astype(BF16)

    lo = 0
    for hi in list(range(MOBA_WIDTH_STEP, nb - 1, MOBA_WIDTH_STEP)) + [nb - 1]:
        pl.when((j > lo) & (j <= hi))(functools.partial(attend, hi))
        lo = hi


def moba_mix(proj, col0, batch, seq, heads):
    dh, blk = MOBA_HEAD_DIM, MOBA_BLOCK
    nb = seq // blk
    nbp = -(-nb // 8) * 8
    t = batch * seq
    return pl.pallas_call(
        _moba_kernel,
        grid=(batch * heads, nb),
        in_specs=[pl.BlockSpec((blk, dh), lambda g, j: ((g // heads) * nb + j, col0 + g % heads)),
                  pl.BlockSpec((seq, dh), lambda g, j: (g // heads, col0 + heads + g % heads)),
                  pl.BlockSpec((seq, dh), lambda g, j: (g // heads, col0 + 2 * heads + g % heads))],
        out_specs=pl.BlockSpec((blk, dh), lambda g, j: ((g // heads) * nb + j, g % heads)),
        out_shape=jax.ShapeDtypeStruct((t, heads * dh), BF16),
        scratch_shapes=[pltpu.VMEM((nbp, dh), F32), pltpu.VMEM((seq, dh + LANES), BF16)],
        compiler_params=_params("parallel", "arbitrary"),
        name="moba",
    )(proj, proj, proj)


def _outproj_kernel(hm_ref, hb_ref, w1_ref, w2_ref, x_ref, mod_ref, o_ref):
    acc = _dot(hm_ref[...], w1_ref[...]) + _dot(hb_ref[...], w2_ref[...])
    o_ref[...] = x_ref[...] + mod_ref[0][2:3] * acc


def out_proj(hm, hb, w_out, x, mod, rows_per_batch, tm=512):
    t, d = x.shape
    k1, k2 = hm.shape[1], hb.shape[1]
    tm = _tile(rows_per_batch, tm)
    tpb = rows_per_batch // tm
    return pl.pallas_call(
        _outproj_kernel,
        grid=(t // tm,),
        in_specs=[pl.BlockSpec((tm, k1), lambda i: (i, 0)),
                  pl.BlockSpec((tm, k2), lambda i: (i, 0)),
                  pl.BlockSpec((k1, d), lambda i: (0, 0)),
                  pl.BlockSpec((k2, d), lambda i: (0, 0)),
                  pl.BlockSpec((tm, d), lambda i: (i, 0)),
                  pl.BlockSpec((1, N_MOD, d), lambda i: (i // tpb, 0, 0))],
        out_specs=pl.BlockSpec((tm, d), lambda i: (i, 0)),
        out_shape=jax.ShapeDtypeStruct((t, d), F32),
        compiler_params=_params("parallel"),
        name="out_proj",
    )(hm, hb, w_out[:k1], w_out[k1:], x, mod)


def _ffn_kernel(x_ref, g_ref, mod_ref, wg_ref, wu_ref, wd_ref, o_ref, h_ref, acc_ref):
    f = pl.program_id(1)

    @pl.when(f == 0)
    def _():
        m = mod_ref[0]
        h_ref[...] = _norm_mod(x_ref[...], g_ref[...], m[3:4], m[4:5]).astype(BF16)
        acc_ref[...] = jnp.zeros_like(acc_ref)

    h = h_ref[...]
    act = (_silu(_dot(h, wg_ref[...])) * _dot(h, wu_ref[...])).astype(BF16)
    acc_ref[...] += _dot(act, wd_ref[...])

    @pl.when(f == pl.num_programs(1) - 1)
    def _():
        o_ref[...] = x_ref[...] + mod_ref[0][5:6] * acc_ref[...]


def ffn_swiglu(x, gain, mod, w_gate, w_up, w_down, rows_per_batch, tm=512, tf=512):
    t, d = x.shape
    f_dim = w_gate.shape[1]
    tm = _tile(rows_per_batch, tm)
    tf = min(tf, f_dim)
    tpb = rows_per_batch // tm
    return pl.pallas_call(
        _ffn_kernel,
        grid=(t // tm, f_dim // tf),
        in_specs=[pl.BlockSpec((tm, d), lambda i, f: (i, 0)),
                  pl.BlockSpec((1, d), lambda i, f: (0, 0)),
                  pl.BlockSpec((1, N_MOD, d), lambda i, f: (i // tpb, 0, 0)),
                  pl.BlockSpec((d, tf), lambda i, f: (0, f)),
                  pl.BlockSpec((d, tf), lambda i, f: (0, f)),
                  pl.BlockSpec((tf, d), lambda i, f: (f, 0))],
        out_specs=pl.BlockSpec((tm, d), lambda i, f: (i, 0)),
        out_shape=jax.ShapeDtypeStruct((t, d), F32),
        scratch_shapes=[pltpu.VMEM((tm, d), BF16), pltpu.VMEM((tm, d), F32)],
        compiler_params=_params("parallel", "arbitrary"),
        name="ffn_swiglu",
    )(x, gain.reshape(1, d), mod, w_gate, w_up, w_down)


def _normmod_kernel(x_ref, g_ref, mod_ref, o_ref):
    m = mod_ref[0]
    o_ref[...] = _norm_mod(x_ref[...], g_ref[...], m[0:1], m[1:2])


def norm_mod(x, gain, mod, rows_per_batch, tm=512):
    t, d = x.shape
    tm = _tile(rows_per_batch, tm)
    tpb = rows_per_batch // tm
    return pl.pallas_call(
        _normmod_kernel,
        grid=(t // tm,),
        in_specs=[pl.BlockSpec((tm, d), lambda i: (i, 0)),
                  pl.BlockSpec((1, d), lambda i: (0, 0)),
                  pl.BlockSpec((1, N_MOD, d), lambda i: (i // tpb, 0, 0))],
        out_specs=pl.BlockSpec((tm, d), lambda i: (i, 0)),
        out_shape=jax.ShapeDtypeStruct((t, d), F32),
        compiler_params=_params("parallel"),
        name="norm_mod",
    )(x, gain.reshape(1, d), mod)


def s5_operators(lam_re, lam_im, log_step, b_re, b_im, c_re, c_im):
    g_all, p = lam_re.shape
    n = b_re.shape[-1]
    sub, tg = S5_SUB, S5_TILE_GROUPS
    nt = g_all // tg
    lam = lax.complex(lam_re.astype(F32), lam_im.astype(F32))
    lam_dt = lam * jnp.exp(log_step.astype(F32))[:, None]
    lam_bar = jnp.exp(lam_dt)
    b_bar = ((lam_bar - 1.0) / lam)[:, :, None] * lax.complex(b_re.astype(F32), b_im.astype(F32))
    c_mat = lax.complex(c_re.astype(F32), c_im.astype(F32))
    tau = jnp.arange(sub + 1, dtype=F32)
    pw = jnp.exp(lam_dt[None] * tau[:, None, None])

    wb = pw[:sub, :, :, None] * b_bar[None]
    taps = (jnp.einsum('gmp,tgpn->tgnm', c_mat.real, wb.real, precision=HI)
            - jnp.einsum('gmp,tgpn->tgnm', c_mat.imag, wb.imag, precision=HI))
    eye_g = jnp.eye(tg, dtype=F32)
    kd = jnp.einsum('tqgnm,gh->qtgnhm', taps.reshape(sub, nt, tg, n, n), eye_g)
    kd = kd.reshape(nt, sub, tg * n, tg * n)
    krev = kd[:, ::-1].reshape(nt, sub * tg * n, tg * n).astype(BF16)

    par = (jnp.arange(tg) % 2)[:, None] == jnp.arange(2)[None, :]
    wb_in = wb[::-1].reshape(sub, nt, tg, p, n)

    def lay_in(a):
        z = jnp.einsum('lqgpn,gr->qlgnrp', a, par.astype(F32))
        return z.reshape(nt, sub, tg * n, 2 * p).astype(BF16)

    b_in_re, b_in_im = lay_in(wb_in.real), lay_in(wb_in.imag)

    e_out = c_mat[None] * pw[1:sub + 1, :, None, :]
    e_out = e_out.reshape(sub, nt, tg, n, p)

    def lay_out(a):
        z = jnp.einsum('lqgmp,gr->qlrpgm', a, par.astype(F32))
        return z.reshape(nt, sub, 2 * p, tg * n).astype(BF16)

    c_out_re, c_out_im = lay_out(e_out.real), lay_out(-e_out.imag)

    n_lvl = 16
    lv = jnp.exp(lam_dt[None] * (sub * 2.0 ** jnp.arange(n_lvl, dtype=F32))[:, None, None])
    lv = lv.reshape(n_lvl, nt, tg * p)
    lam_lv = jnp.stack([lv.real, lv.imag], axis=2).transpose(1, 0, 2, 3)
    return krev, b_in_re, b_in_im, c_out_re, c_out_im, lam_lv


def _gelu_tanh(x):
    return 0.5 * x * (1.0 + jnp.tanh(0.7978845608028654 * (x + 0.044715 * (x * x * x))))


def _s5_kernel(u_ref, krev_ref, bre_ref, bim_ref, cre_ref, cim_ref, lam_ref, d_ref, o_ref,
               ucat, bcat, ccat):
    sub = bre_ref.shape[1]
    cw = u_ref.shape[1]
    r = u_ref.shape[0] // sub
    sw = bcat.shape[1] // 2
    pair = bre_ref.shape[3]

    @pl.when(pl.program_id(1) == 0)
    def _():
        rb = lax.broadcasted_iota(jnp.int32, (cw, sw), 0) // (2 * S5_GROUP)
        cb = lax.broadcasted_iota(jnp.int32, (cw, sw), 1) // pair
        in_mask = rb == cb
        rc = lax.broadcasted_iota(jnp.int32, (sw, cw), 0) // pair
        cc = lax.broadcasted_iota(jnp.int32, (sw, cw), 1) // (2 * S5_GROUP)
        out_mask = rc == cc
        reps = sw // pair

        def expand(ref, l, axis, mask):
            tiled = jnp.concatenate([ref[0, l].astype(F32)] * reps, axis=axis)
            return jnp.where(mask, tiled, 0.0).astype(BF16)

        for l in range(sub):
            bcat[l * cw:(l + 1) * cw, 0:sw] = expand(bre_ref, l, 1, in_mask)
            bcat[l * cw:(l + 1) * cw, sw:2 * sw] = expand(bim_ref, l, 1, in_mask)
            ccat[l, 0:sw, :] = expand(cre_ref, l, 0, out_mask)
            ccat[l, sw:2 * sw, :] = expand(cim_ref, l, 0, out_mask)

    for l in range(sub):
        ucat[:, l * cw:(l + 1) * cw] = u_ref[pl.ds(l, r, stride=sub), :].astype(BF16)

    v = _dot(ucat[...], bcat[...])
    s_re, s_im = v[:, 0:sw], v[:, sw:2 * sw]
    rowi = lax.broadcasted_iota(jnp.int32, (r, sw), 0)
    shift, lvl = 1, 0
    while shift < r:
        lr = lam_ref[0, lvl, 0:1, :]
        li = lam_ref[0, lvl, 1:2, :]
        keep = rowi >= shift
        p_re = jnp.where(keep, pltpu.roll(s_re, shift, axis=0), 0.0)
        p_im = jnp.where(keep, pltpu.roll(s_im, shift, axis=0), 0.0)
        s_re, s_im = s_re + lr * p_re - li * p_im, s_im + lr * p_im + li * p_re
        shift, lvl = shift * 2, lvl + 1
    first = rowi >= 1
    x_re = jnp.where(first, pltpu.roll(s_re, 1, axis=0), 0.0)
    x_im = jnp.where(first, pltpu.roll(s_im, 1, axis=0), 0.0)
    xb = jnp.concatenate([x_re, x_im], axis=1).astype(BF16)

    for l in range(sub):
        y = _dot(ucat[:, 0:(l + 1) * cw], krev_ref[0, (sub - 1 - l) * cw:sub * cw, :])
        y = y + _dot(xb, ccat[l])
        ul = u_ref[pl.ds(l, r, stride=sub), :]
        o_ref[pl.ds(l, r, stride=sub), :] = _gelu_tanh(y + d_ref[...] * ul)


def s5_scan_gelu(u, ops, d_skip, batch, seq):
    krev, bre, bim, cre, cim, lam_lv = ops
    t, d = u.shape
    nt, sub = bre.shape[0], bre.shape[1]
    cw = bre.shape[2]
    pair = bre.shape[3]
    sw = S5_TILE_GROUPS * S5_STATE
    n_lvl = lam_lv.shape[1]
    op4 = lambda a, b_: pl.BlockSpec((1, sub, a, b_), lambda c, b: (c, 0, 0, 0))
    return pl.pallas_call(
        _s5_kernel,
        grid=(nt, batch),
        in_specs=[pl.BlockSpec((seq, cw), lambda c, b: (b, c)),
                  pl.BlockSpec((1, sub * cw, cw), lambda c, b: (c, 0, 0)),
                  op4(cw, pair), op4(cw, pair), op4(pair, cw), op4(pair, cw),
                  pl.BlockSpec((1, n_lvl, 2, sw), lambda c, b: (c, 0, 0, 0)),
                  pl.BlockSpec((1, cw), lambda c, b: (0, c))],
        out_specs=pl.BlockSpec((seq, cw), lambda c, b: (b, c)),
        out_shape=jax.ShapeDtypeStruct((t, d), F32),
        scratch_shapes=[pltpu.VMEM((seq // sub, sub * cw), BF16),
                        pltpu.VMEM((sub * cw, 2 * sw), BF16),
                        pltpu.VMEM((sub, 2 * sw, cw), BF16)],
        compiler_params=_params("parallel", "arbitrary"),
        name="s5_scan",
    )(u, krev, bre, bim, cre, cim, lam_lv, d_skip.reshape(1, d))


def _glu_kernel(g_ref, wa_ref, wb_ref, x_ref, mod_ref, o_ref):
    g = g_ref[...].astype(BF16)
    mix = _dot(g, wa_ref[...]) * jax.nn.sigmoid(_dot(g, wb_ref[...]))
    o_ref[...] = x_ref[...] + mod_ref[0][2:3] * mix


def glu_out(g, w_a, w_b, x, mod, rows_per_batch, tm=512, tn=1024):
    t, d = x.shape
    tm = _tile(rows_per_batch, tm)
    tn = min(tn, d)
    tpb = rows_per_batch // tm
    return pl.pallas_call(
        _glu_kernel,
        grid=(d // tn, t // tm),
        in_specs=[pl.BlockSpec((tm, d), lambda j, i: (i, 0)),
                  pl.BlockSpec((d, tn), lambda j, i: (0, j)),
                  pl.BlockSpec((d, tn), lambda j, i: (0, j)),
                  pl.BlockSpec((tm, tn), lambda j, i: (i, j)),
                  pl.BlockSpec((1, N_MOD, tn), lambda j, i: (i // tpb, 0, j))],
        out_specs=pl.BlockSpec((tm, tn), lambda j, i: (i, j)),
        out_shape=jax.ShapeDtypeStruct((t, d), F32),
        compiler_params=_params("parallel", "parallel"),
        name="glu_out",
    )(g, w_a, w_b, x, mod)


def _router_kernel(x_ref, g_ref, mod_ref, rw_ref, rb_ref, h_ref, r_ref):
    m = mod_ref[0]
    h = _norm_mod(x_ref[...], g_ref[...], m[3:4], m[4:5])
    h_ref[...] = h.astype(BF16)
    logits = jnp.dot(h, rw_ref[...], precision=HI, preferred_element_type=F32) + rb_ref[...]
    lane = lax.broadcasted_iota(jnp.int32, logits.shape, 1)
    logits = jnp.where(lane < N_EXPERTS, logits, NEG_INF)
    m1 = jnp.max(logits, axis=1, keepdims=True)
    i1 = jnp.min(jnp.where(logits == m1, lane, LANES), axis=1, keepdims=True)
    rest = jnp.where(lane == i1, NEG_INF, logits)
    m2 = jnp.max(rest, axis=1, keepdims=True)
    i2 = jnp.min(jnp.where(rest == m2, lane, LANES), axis=1, keepdims=True)
    e2 = jnp.exp(m2 - m1)
    g1 = 1.0 / (1.0 + e2)
    g2 = e2 / (1.0 + e2)
    r_ref[...] = jnp.where(lane == 0, i1.astype(F32),
                           jnp.where(lane == 1, i2.astype(F32),
                                     jnp.where(lane == 2, g1, jnp.where(lane == 3, g2, 0.0))))


def moe_router(x, gain, mod, router_w, router_b, rows_per_batch, tm=512):
    t, d = x.shape
    e = router_w.shape[1]
    tm = _tile(rows_per_batch, tm)
    tpb = rows_per_batch // tm
    rw = jnp.pad(router_w.astype(F32), ((0, 0), (0, LANES - e)))
    rb = jnp.pad(router_b.astype(F32), (0, LANES - e)).reshape(1, LANES)
    return pl.pallas_call(
        _router_kernel,
        grid=(t // tm,),
        in_specs=[pl.BlockSpec((tm, d), lambda i: (i, 0)),
                  pl.BlockSpec((1, d), lambda i: (0, 0)),
                  pl.BlockSpec((1, N_MOD, d), lambda i: (i // tpb, 0, 0)),
                  pl.BlockSpec((d, LANES), lambda i: (0, 0)),
                  pl.BlockSpec((1, LANES), lambda i: (0, 0))],
        out_specs=[pl.BlockSpec((tm, d), lambda i: (i, 0)),
                   pl.BlockSpec((tm, LANES), lambda i: (i, 0))],
        out_shape=[jax.ShapeDtypeStruct((t, d), BF16), jax.ShapeDtypeStruct((t, LANES), F32)],
        compiler_params=_params("parallel"),
        name="moe_router",
    )(x, gain.reshape(1, d), mod, rw, rb)


def moe_dispatch(top_e, rows, sub_rows):
    t = top_e.shape[0]
    n_assign = t * TOP_K
    n_blocks = -(-n_assign // rows) + N_EXPERTS
    e_flat = top_e.reshape(-1)
    onehot = (e_flat[:, None] == jnp.arange(N_EXPERTS, dtype=jnp.int32)[None, :]).astype(jnp.int32)
    csum = jnp.cumsum(onehot, axis=0)
    rank = jnp.sum((csum - onehot) * onehot, axis=1)
    counts = csum[-1]
    padded = (counts + rows - 1) // rows * rows
    pad_end = jnp.cumsum(padded)
    pad_start = pad_end - padded
    dest = jnp.sum(onehot * pad_start[None, :], axis=1) + rank
    tok = jnp.arange(n_assign, dtype=jnp.int32) // TOP_K
    row_tok = jnp.zeros((n_blocks * rows,), jnp.int32).at[dest].set(tok)
    n_active = pad_end[-1] // rows
    blk = jnp.arange(n_blocks, dtype=jnp.int32)
    blk_start = jnp.minimum(blk, n_active - 1) * rows
    block_expert = jnp.minimum(jnp.searchsorted(pad_end, blk_start, side='right'),
                               N_EXPERTS - 1).astype(jnp.int32)
    real_rows = jnp.clip(pad_start[block_expert] + counts[block_expert] - blk_start, 0, rows)
    n_sub = jnp.where(blk < n_active, (real_rows + sub_rows - 1) // sub_rows, 0).astype(jnp.int32)
    return row_tok, dest.reshape(t, TOP_K), block_expert, n_sub


def _expert_kernel(be_ref, ns_ref, x_ref, wg_ref, wu_ref, wd_ref, o_ref, acc_ref, wgb, wub, wdb,
                   *, sub_rows):
    i = pl.program_id(0)
    f = pl.program_id(1)
    rows = x_ref.shape[0]

    @pl.when(f == 0)
    def _():
        acc_ref[...] = jnp.zeros_like(acc_ref)

    @pl.when(ns_ref[i] > 0)
    def _():
        wgb[...] = wg_ref[0].astype(BF16)
        wub[...] = wu_ref[0].astype(BF16)
        wdb[...] = wd_ref[0].astype(BF16)

    for s in range(rows // sub_rows):
        @pl.when(s < ns_ref[i])
        def _():
            sl = slice(s * sub_rows, (s + 1) * sub_rows)
            x = x_ref[sl, :]
            act = (_silu(_dot(x, wgb[...])) * _dot(x, wub[...])).astype(BF16)
            acc_ref[sl, :] += _dot(act, wdb[...])

    @pl.when(f == pl.num_programs(1) - 1)
    def _():
        o_ref[...] = acc_ref[...].astype(BF16)


def moe_experts(xg, block_expert, n_sub, w_gate, w_up, w_down, layer, rows, sub_rows, tf=256):
    r_tot, d = xg.shape
    f_dim = w_gate.shape[2]
    tf = min(tf, f_dim)
    nf = f_dim // tf
    n_blocks = r_tot // rows
    e0 = layer * N_EXPERTS

    def f_idx(i, f, ns):
        return jnp.where(ns[i] > 0, f, nf - 1)

    grid_spec = pltpu.PrefetchScalarGridSpec(
        num_scalar_prefetch=2,
        grid=(n_blocks, nf),
        in_specs=[pl.BlockSpec((rows, d), lambda i, f, be, ns: (i, 0)),
                  pl.BlockSpec((1, d, tf), lambda i, f, be, ns: (e0 + be[i], 0, f_idx(i, f, ns))),
                  pl.BlockSpec((1, d, tf), lambda i, f, be, ns: (e0 + be[i], 0, f_idx(i, f, ns))),
                  pl.BlockSpec((1, tf, d), lambda i, f, be, ns: (e0 + be[i], f_idx(i, f, ns), 0))],
        out_specs=pl.BlockSpec((rows, d), lambda i, f, be, ns: (i, 0)),
        scratch_shapes=[pltpu.VMEM((rows, d), F32), pltpu.VMEM((d, tf), BF16),
                        pltpu.VMEM((d, tf), BF16), pltpu.VMEM((tf, d), BF16)],
    )
    return pl.pallas_call(
        functools.partial(_expert_kernel, sub_rows=sub_rows),
        grid_spec=grid_spec,
        out_shape=jax.ShapeDtypeStruct((r_tot, d), BF16),
        compiler_params=_params("arbitrary", "arbitrary"),
        name="moe_experts",
    )(block_expert, n_sub, xg, w_gate, w_up, w_down)


def _combine_kernel(x_ref, y1_ref, y2_ref, r_ref, mod_ref, gf_ref, o_ref, *, final_norm):
    r = r_ref[...]
    ff = r[:, 2:3] * y1_ref[...].astype(F32) + r[:, 3:4] * y2_ref[...].astype(F32)
    x_new = x_ref[...] + mod_ref[0][5:6] * ff
    if final_norm:
        ms = jnp.mean(x_new * x_new, axis=-1, keepdims=True)
        x_new = x_new * lax.rsqrt(ms + EPS) * gf_ref[...]
    o_ref[...] = x_new


def moe_combine(x, y1, y2, route, mod, g_final, rows_per_batch, final_norm, tm=512):
    t, d = x.shape
    tm = _tile(rows_per_batch, tm)
    tpb = rows_per_batch // tm
    row = lambda w: pl.BlockSpec((tm, w), lambda i: (i, 0))
    return pl.pallas_call(
        functools.partial(_combine_kernel, final_norm=final_norm),
        grid=(t // tm,),
        in_specs=[row(d), row(d), row(d), row(LANES),
                  pl.BlockSpec((1, N_MOD, d), lambda i: (i // tpb, 0, 0)),
                  pl.BlockSpec((1, d), lambda i: (0, 0))],
        out_specs=row(d),
        out_shape=jax.ShapeDtypeStruct((t, d), F32),
        compiler_params=_params("parallel"),
        name="moe_combine",
    )(x, y1, y2, route, mod, g_final.reshape(1, d))


def kernel(x, c, w_ada, b_ada, g_mix, g_ffn, g_final, w_in, conv_w, b_igate, b_fgate, mh_gain, w_out,
           ffn_w_gate, ffn_w_up, ffn_w_down, s5_lam_re, s5_lam_im, s5_log_step, s5_b_re, s5_b_im,
           s5_c_re, s5_c_im, s5_d, glu_w_a, glu_w_b, router_w, router_b, exp_w_gate, exp_w_up, exp_w_down):
    batch, seq, d = x.shape
    depth = w_ada.shape[0]
    t = batch * seq
    m_width = mh_gain.shape[1]
    heads_b = (d - m_width) // MOBA_HEAD_DIM
    n_gate = 2 * MLSTM_HEADS

    mods = ada_mod(c, w_ada, b_ada)
    xs = x.reshape(t, d)
    for layer in range(depth):
        i = layer // 2
        mod = mods[layer]
        if layer % 2 == 0:
            w = w_in[i]
            g0 = 4 * m_width
            w_big = jnp.concatenate([w[:, :g0], w[:, g0 + n_gate:]], axis=1).astype(BF16)
            w_gates = jnp.pad(w[:, g0:g0 + n_gate], ((0, 0), (0, LANES - n_gate)))
            proj, gates = in_proj(xs, g_mix[layer], mod, w_big, w_gates, seq)
            hm = mlstm_mix(proj, gates, conv_w[i], b_igate[i], b_fgate[i], mh_gain[i], batch, seq)
            hb = moba_mix(proj, g0 // MOBA_HEAD_DIM, batch, seq, heads_b)
            xs = out_proj(hm, hb, w_out[i].astype(BF16), xs, mod, seq)
            f_pad = -ffn_w_gate.shape[2] % 512
            wg = jnp.pad(ffn_w_gate[i], ((0, 0), (0, f_pad))).astype(BF16)
            wu = jnp.pad(ffn_w_up[i], ((0, 0), (0, f_pad))).astype(BF16)
            wd = jnp.pad(ffn_w_down[i], ((0, f_pad), (0, 0))).astype(BF16)
            xs = ffn_swiglu(xs, g_ffn[layer], mod, wg, wu, wd, seq)
        else:
            u = norm_mod(xs, g_mix[layer], mod, seq)
            ops = s5_operators(s5_lam_re[i], s5_lam_im[i], s5_log_step[i], s5_b_re[i], s5_b_im[i],
                               s5_c_re[i], s5_c_im[i])
            g = s5_scan_gelu(u, ops, s5_d[i], batch, seq)
            xs = glu_out(g, glu_w_a[i].astype(BF16), glu_w_b[i].astype(BF16), xs, mod, seq)
            h, route = moe_router(xs, g_ffn[layer], mod, router_w[i], router_b[i], seq)
            top_e = route[:, 0:TOP_K].astype(jnp.int32)
            row_tok, pos, block_expert, n_sub = moe_dispatch(top_e, MOE_ROWS, MOE_SUB_ROWS)
            stack = lambda w_: w_.reshape((-1,) + w_.shape[2:])
            y_rows = moe_experts(h[row_tok], block_expert, n_sub, stack(exp_w_gate), stack(exp_w_up),
                                 stack(exp_w_down), i, MOE_ROWS, MOE_SUB_ROWS)
            xs = moe_combine(xs, y_rows[pos[:, 0]], y_rows[pos[:, 1]], route, mod, g_final, seq,
                             final_norm=(layer == depth - 1))
    if depth % 2 == 1:
        raise NotImplementedError("final norm is fused into the last (odd) layer")
    return xs.reshape(batch, seq, d)
```

```python
import functools
import math

import jax
import jax.numpy as jnp
from jax import lax
from jax.experimental import pallas as pl
from jax.experimental.pallas import tpu as pltpu

F32 = jnp.float32
BF16 = jnp.bfloat16
HI = lax.Precision.HIGHEST
NEG_INF = float("-inf")

EPS = 1e-6
N_MOD = 6
MLSTM_HEADS = 4
MLSTM_CHUNK = 128
MLSTM_CONV = 4
MOBA_HEAD_DIM = 128
MOBA_BLOCK = 256
MOBA_TOPK = 3
S5_GROUP = 16
S5_STATE = 64
S5_SUB = 16
S5_TILE_GROUPS = 8
N_EXPERTS = 8
TOP_K = 2
MOE_ROWS = 1024
MOE_SUB_ROWS = 512
LANES = 128


def _tile(n, pref, align=8):
    for cand in range(min(pref, n), 0, -1):
        if n % cand == 0 and cand % align == 0:
            return cand
    raise ValueError(f"no {align}-aligned tile divides {n}")


def _params(*sem):
    return pltpu.CompilerParams(dimension_semantics=sem)


def _dot(a, b):
    return jnp.dot(a, b, preferred_element_type=F32)


def _dot_nt(a, b, precision=None):
    return lax.dot_general(a, b, (((1,), (1,)), ((), ())), precision=precision,
                           preferred_element_type=F32)


def _silu(x):
    return x * jax.nn.sigmoid(x)


def _norm_mod(x, gain, shift, scale):
    ms = jnp.mean(x * x, axis=-1, keepdims=True)
    y = x * lax.rsqrt(ms + EPS) * gain
    return y * (1.0 + scale) + shift


def _ada_kernel(c_ref, w_ref, b_ref, o_ref):
    c = c_ref[...]
    o_ref[0] = jnp.dot(_silu(c), w_ref[0], precision=HI, preferred_element_type=F32) + b_ref[0]


def ada_mod(c, w_ada, b_ada, tn=1024):
    depth, d, n = w_ada.shape
    b = c.shape[0]
    bp = 8
    cp = jnp.pad(c, ((0, bp - b), (0, 0)))
    out = pl.pallas_call(
        _ada_kernel,
        grid=(depth, n // tn),
        in_specs=[pl.BlockSpec((bp, d), lambda l, j: (0, 0)),
                  pl.BlockSpec((1, d, tn), lambda l, j: (l, 0, j)),
                  pl.BlockSpec((1, 1, tn), lambda l, j: (l, 0, j))],
        out_specs=pl.BlockSpec((1, bp, tn), lambda l, j: (l, 0, j)),
        out_shape=jax.ShapeDtypeStruct((depth, bp, n), F32),
        compiler_params=_params("parallel", "parallel"),
        name="ada_mod",
    )(cp, w_ada, b_ada.reshape(depth, 1, n))
    return out[:, :b].reshape(depth, b, N_MOD, d)


def _inproj_kernel(x_ref, g_ref, mod_ref, w_ref, wg_ref, o_ref, og_ref, h_ref):
    @pl.when(pl.program_id(1) == 0)
    def _():
        m = mod_ref[0]
        h = _norm_mod(x_ref[...], g_ref[...], m[0:1], m[1:2])
        h_ref[...] = h.astype(BF16)
        og_ref[...] = jnp.dot(h, wg_ref[...], precision=HI, preferred_element_type=F32)

    o_ref[...] = _dot(h_ref[...], w_ref[...]).astype(BF16)


def in_proj(x, gain, mod, w_big, w_gates, rows_per_batch, tm=1024, tn=1024):
    t, d = x.shape
    n = w_big.shape[1]
    tm = _tile(rows_per_batch, tm)
    tn = min(tn, n)
    tpb = rows_per_batch // tm
    return pl.pallas_call(
        _inproj_kernel,
        grid=(t // tm, n // tn),
        in_specs=[pl.BlockSpec((tm, d), lambda i, j: (i, 0)),
                  pl.BlockSpec((1, d), lambda i, j: (0, 0)),
                  pl.BlockSpec((1, N_MOD, d), lambda i, j: (i // tpb, 0, 0)),
                  pl.BlockSpec((d, tn), lambda i, j: (0, j)),
                  pl.BlockSpec((d, LANES), lambda i, j: (0, 0))],
        out_specs=[pl.BlockSpec((tm, tn), lambda i, j: (i, j)),
                   pl.BlockSpec((tm, LANES), lambda i, j: (i, 0))],
        out_shape=[jax.ShapeDtypeStruct((t, n), BF16), jax.ShapeDtypeStruct((t, LANES), F32)],
        scratch_shapes=[pltpu.VMEM((tm, d), BF16)],
        compiler_params=_params("parallel", "arbitrary"),
        name="in_proj",
    )(x, gain.reshape(1, d), mod, w_big, w_gates)


def _mlstm_kernel(bias_ref, q_ref, k_ref, v_ref, o_ref, gi_ref, gf_ref, cwq_ref, cwk_ref, gain_ref,
                  out_ref, qbuf, kbuf, c_st, n_st, m_st):
    chunk = pl.program_id(1)
    L = q_ref.shape[0]
    heads, dh = c_st.shape[0], c_st.shape[1]
    taps = cwq_ref.shape[0]
    halo = 8

    @pl.when(chunk == 0)
    def _():
        qbuf[0:halo] = jnp.zeros((halo, qbuf.shape[1]), F32)
        kbuf[0:halo] = jnp.zeros((halo, kbuf.shape[1]), F32)
        c_st[...] = jnp.zeros_like(c_st)
        n_st[...] = jnp.zeros_like(n_st)
        m_st[...] = jnp.full(m_st.shape, -1e30, F32)

    def conv_silu(src_ref, buf, w_ref):
        buf[halo:halo + L] = src_ref[...].astype(F32)
        w = w_ref[...]
        acc = buf[halo:halo + L] * w[taps - 1:taps]
        for d in range(1, taps):
            acc = acc + buf[pl.ds(halo - d, L), :] * w[taps - 1 - d:taps - d]
        buf[0:halo] = buf[L:L + halo]
        return _silu(acc)

    q_all = conv_silu(q_ref, qbuf, cwq_ref)
    k_all = conv_silu(k_ref, kbuf, cwk_ref) * (dh ** -0.5)

    row = lax.broadcasted_iota(jnp.int32, (L, L), 0)
    col = lax.broadcasted_iota(jnp.int32, (L, L), 1)
    eye = row == col

    def to_col(x_row):
        return jnp.sum(jnp.where(eye, jnp.broadcast_to(x_row, (L, L)), 0.0), axis=1, keepdims=True)

    fz = jnp.concatenate([gf_ref[0, h, 0] + bias_ref[1, h] for h in range(heads)]
                         + [jnp.zeros((8 - heads, L), F32)], axis=0)
    lf_rows = jnp.minimum(fz, 0.0) - jnp.log(1.0 + jnp.exp(-jnp.abs(fz)))
    g_rows = jnp.dot(lf_rows, (row <= col).astype(F32), precision=HI,
                     preferred_element_type=F32)

    for h in range(heads):
        sl = slice(h * dh, (h + 1) * dh)
        q, k, vb = q_all[:, sl], k_all[:, sl], v_ref[:, sl]
        ig_row = gi_ref[0, h, 0] + bias_ref[0, h]
        g_row = g_rows[h:h + 1]
        g_col = to_col(g_row)
        b_row = ig_row - g_row
        d_mat = jnp.where(col <= row, g_col + b_row, NEG_INF)
        m_prev = m_st[h]
        m_inter = g_col + m_prev
        m_t = jnp.maximum(m_inter, jnp.max(d_mat, axis=1, keepdims=True))
        qb = q.astype(BF16)
        kb = k.astype(BF16)
        s = _dot_nt(qb, kb) * jnp.exp(d_mat - m_t)
        decay = jnp.exp(m_inter - m_t)
        num = _dot(s.astype(BF16), vb) + decay * _dot(qb, c_st[h].astype(BF16))
        den = jnp.sum(s, axis=1, keepdims=True) + decay * jnp.sum(q * n_st[h], axis=1, keepdims=True)
        hh = num / jnp.maximum(jnp.abs(den), jnp.exp(-m_t))

        g_last = g_row[:, L - 1:L]
        a_row = g_last + b_row
        m_new = jnp.maximum(g_last + m_prev, jnp.max(a_row, axis=1, keepdims=True))
        w_col = to_col(jnp.exp(a_row - m_new))
        carry = jnp.exp(g_last + m_prev - m_new)
        kw = k * w_col
        c_st[h] = carry * c_st[h] + _dot(kw.T.astype(BF16), vb)
        n_st[h] = carry * n_st[h] + jnp.sum(kw, axis=0, keepdims=True)
        m_st[h] = m_new

        hn = hh * lax.rsqrt(jnp.mean(hh * hh, axis=1, keepdims=True) + EPS) * gain_ref[:, sl]
        out_ref[:, sl] = (hn * jax.nn.sigmoid(o_ref[:, sl].astype(F32))).astype(BF16)


def mlstm_mix(proj, gates, conv_w, b_igate, b_fgate, mh_gain, batch, seq):
    heads, L = MLSTM_HEADS, MLSTM_CHUNK
    width = mh_gain.shape[0]
    dh = width // heads
    nc = seq // L
    t = batch * seq

    def rows(a):
        return a.reshape(batch, nc, L, heads).transpose(0, 3, 1, 2).reshape(batch, heads, nc, 1, L)

    gi = rows(gates[:, 0:heads])
    gf = rows(gates[:, heads:2 * heads])
    bias = jnp.stack([b_igate, b_fgate]).astype(F32)
    blk = lambda off: pl.BlockSpec((L, width), lambda b, c: (b * nc + c, off))
    gspec = pl.BlockSpec((1, heads, 1, 1, L), lambda b, c: (b, 0, c, 0, 0))
    return pl.pallas_call(
        _mlstm_kernel,
        grid=(batch, nc),
        in_specs=[pl.BlockSpec(memory_space=pltpu.SMEM),
                  blk(0), blk(1), blk(2), blk(3), gspec, gspec,
                  pl.BlockSpec((MLSTM_CONV, width), lambda b, c: (0, 0)),
                  pl.BlockSpec((MLSTM_CONV, width), lambda b, c: (0, 1)),
                  pl.BlockSpec((1, width), lambda b, c: (0, 0))],
        out_specs=pl.BlockSpec((L, width), lambda b, c: (b * nc + c, 0)),
        out_shape=jax.ShapeDtypeStruct((t, width), BF16),
        scratch_shapes=[pltpu.VMEM((L + 8, width), F32), pltpu.VMEM((L + 8, width), F32),
                        pltpu.VMEM((heads, dh, dh), F32), pltpu.VMEM((heads, 1, dh), F32),
                        pltpu.VMEM((heads, 1, 1), F32)],
        compiler_params=_params("parallel", "arbitrary"),
        name="mlstm",
    )(bias, proj, proj, proj, proj, gi, gf, conv_w, conv_w, mh_gain.reshape(1, width))


MOBA_MASK_BIAS = -1e9
MOBA_WIDTH_STEP = 4
MOBA_CHUNK_BLOCKS = 2


def _moba_kernel(q_ref, k_ref, v_ref, o_ref, kmean_ref, kaug_ref, s_ref):
    j = pl.program_id(1)
    blk, dh = q_ref.shape
    seq = k_ref.shape[0]
    nb = seq // blk
    nbp = kmean_ref.shape[0]
    scale = dh ** -0.5

    @pl.when(j == 0)
    def _():
        kmean_ref[...] = jnp.zeros_like(kmean_ref)
        for b in range(nb):
            kmean_ref[b:b + 1, :] = jnp.mean(k_ref[b * blk:(b + 1) * blk, :].astype(F32), axis=0,
                                             keepdims=True)
        kaug_ref[:, 0:dh] = k_ref[...]
        key_blk = lax.broadcasted_iota(jnp.int32, (seq, LANES), 0) // blk
        lane = lax.broadcasted_iota(jnp.int32, (seq, LANES), 1)
        kaug_ref[:, dh:dh + LANES] = jnp.where(key_blk == lane, 1.0, 0.0).astype(BF16)

    q = q_ref[...]
    gate_t = _dot_nt(kmean_ref[...], q.astype(F32), precision=HI)
    blk_id = lax.broadcasted_iota(jnp.int32, gate_t.shape, 0)
    valid = blk_id < j
    sc = jnp.where(valid, gate_t, NEG_INF)
    beaten = jnp.zeros(gate_t.shape, F32)
    for b2 in range(nb):
        other = sc[b2:b2 + 1, :]
        wins = (other > sc) | ((other == sc) & (b2 < blk_id))
        beaten = beaten + wins.astype(F32)
    chosen = valid & (beaten < MOBA_TOPK)
    bias_t = jnp.where(chosen, 0.0, MOBA_MASK_BIAS)
    bias = jnp.concatenate([bias_t, jnp.zeros((LANES - nbp, blk), F32)], axis=0).T
    q_aug = jnp.concatenate([q, bias.astype(BF16)], axis=1)

    row = lax.broadcasted_iota(jnp.int32, (blk, blk), 0)
    col = lax.broadcasted_iota(jnp.int32, (blk, blk), 1)
    start = pl.multiple_of(j * blk, blk)
    s_own = jnp.where(col <= row, _dot_nt(q, k_ref[pl.ds(start, blk), :]), NEG_INF)
    m_own = jnp.max(s_own, axis=1, keepdims=True)
    v_own = v_ref[pl.ds(start, blk), :]
    exp_scale = scale * math.log2(math.e)

    @pl.when(j == 0)
    def _():
        p = jnp.exp2((s_own - m_own) * exp_scale)
        o_ref[...] = (_dot(p.astype(BF16), v_own) / jnp.sum(p, axis=1, keepdims=True)).astype(BF16)

    def attend(n_blocks):
        w = n_blocks * blk
        step = MOBA_CHUNK_BLOCKS * blk
        chunks = [(lo_, min(lo_ + step, w)) for lo_ in range(0, w, step)]
        mx = jnp.full((blk, LANES), NEG_INF, F32)
        for lo_, hi_ in chunks:
            s_c = _dot_nt(q_aug, kaug_ref[lo_:hi_, :])
            s_ref[:, lo_:hi_] = s_c
            for t_ in range((hi_ - lo_) // LANES):
                mx = jnp.maximum(mx, s_c[:, t_ * LANES:(t_ + 1) * LANES])
        m = jnp.maximum(jnp.max(mx, axis=1, keepdims=True), m_own)
        p_own = jnp.exp2((s_own - m) * exp_scale)
        acc = _dot(p_own.astype(BF16), v_own)
        lsum = p_own[:, 0:LANES]
        for t_ in range(1, blk // LANES):
            lsum = lsum + p_own[:, t_ * LANES:(t_ + 1) * LANES]
        for lo_, hi_ in chunks:
            p = jnp.exp2((s_ref[:, lo_:hi_] - m) * exp_scale)
            for t_ in range((hi_ - lo_) // LANES):
                lsum = lsum + p[:, t_ * LANES:(t_ + 1) * LANES]
            acc = acc + _dot(p.astype(BF16), v_ref[lo_:hi_, :])
        o_ref[...] = (acc / jnp.sum(lsum, axis=1, keepdims=True)).astype(BF16)

    lo = 0
    for hi in list(range(MOBA_WIDTH_STEP, nb - 1, MOBA_WIDTH_STEP)) + [nb - 1]:
        pl.when((j > lo) & (j <= hi))(functools.partial(attend, hi))
        lo = hi


def moba_mix(proj, col0, batch, seq, heads):
    dh, blk = MOBA_HEAD_DIM, MOBA_BLOCK
    nb = seq // blk
    nbp = -(-nb // 8) * 8
    t = batch * seq
    return pl.pallas_call(
        _moba_kernel,
        grid=(batch * heads, nb),
        in_specs=[pl.BlockSpec((blk, dh), lambda g, j: ((g // heads) * nb + j, col0 + g % heads)),
                  pl.BlockSpec((seq, dh), lambda g, j: (g // heads, col0 + heads + g % heads)),
                  pl.BlockSpec((seq, dh), lambda g, j: (g // heads, col0 + 2 * heads + g % heads))],
        out_specs=pl.BlockSpec((blk, dh), lambda g, j: ((g // heads) * nb + j, g % heads)),
        out_shape=jax.ShapeDtypeStruct((t, heads * dh), BF16),
        scratch_shapes=[pltpu.VMEM((nbp, dh), F32), pltpu.VMEM((seq, dh + LANES), BF16),
                        pltpu.VMEM((blk, seq), F32)],
        compiler_params=_params("parallel", "arbitrary"),
        name="moba",
    )(proj, proj, proj)


def _outproj_kernel(hm_ref, hb_ref, w1_ref, w2_ref, x_ref, mod_ref, o_ref):
    acc = _dot(hm_ref[...], w1_ref[...]) + _dot(hb_ref[...], w2_ref[...])
    o_ref[...] = x_ref[...] + mod_ref[0][2:3] * acc


def out_proj(hm, hb, w_out, x, mod, rows_per_batch, tm=512):
    t, d = x.shape
    k1, k2 = hm.shape[1], hb.shape[1]
    tm = _tile(rows_per_batch, tm)
    tpb = rows_per_batch // tm
    return pl.pallas_call(
        _outproj_kernel,
        grid=(t // tm,),
        in_specs=[pl.BlockSpec((tm, k1), lambda i: (i, 0)),
                  pl.BlockSpec((tm, k2), lambda i: (i, 0)),
                  pl.BlockSpec((k1, d), lambda i: (0, 0)),
                  pl.BlockSpec((k2, d), lambda i: (0, 0)),
                  pl.BlockSpec((tm, d), lambda i: (i, 0)),
                  pl.BlockSpec((1, N_MOD, d), lambda i: (i // tpb, 0, 0))],
        out_specs=pl.BlockSpec((tm, d), lambda i: (i, 0)),
        out_shape=jax.ShapeDtypeStruct((t, d), F32),
        compiler_params=_params("parallel"),
        name="out_proj",
    )(hm, hb, w_out[:k1], w_out[k1:], x, mod)


def _ffn_kernel(x_ref, g_ref, mod_ref, wg_ref, wu_ref, wd_ref, o_ref, h_ref, acc_ref):
    f = pl.program_id(1)

    @pl.when(f == 0)
    def _():
        m = mod_ref[0]
        h_ref[...] = _norm_mod(x_ref[...], g_ref[...], m[3:4], m[4:5]).astype(BF16)
        acc_ref[...] = jnp.zeros_like(acc_ref)

    h = h_ref[...]
    act = (_silu(_dot(h, wg_ref[...])) * _dot(h, wu_ref[...])).astype(BF16)
    acc_ref[...] += _dot(act, wd_ref[...])

    @pl.when(f == pl.num_programs(1) - 1)
    def _():
        o_ref[...] = x_ref[...] + mod_ref[0][5:6] * acc_ref[...]


def ffn_swiglu(x, gain, mod, w_gate, w_up, w_down, rows_per_batch, tm=512, tf=512):
    t, d = x.shape
    f_dim = w_gate.shape[1]
    tm = _tile(rows_per_batch, tm)
    tf = min(tf, f_dim)
    tpb = rows_per_batch // tm
    return pl.pallas_call(
        _ffn_kernel,
        grid=(t // tm, f_dim // tf),
        in_specs=[pl.BlockSpec((tm, d), lambda i, f: (i, 0)),
                  pl.BlockSpec((1, d), lambda i, f: (0, 0)),
                  pl.BlockSpec((1, N_MOD, d), lambda i, f: (i // tpb, 0, 0)),
                  pl.BlockSpec((d, tf), lambda i, f: (0, f)),
                  pl.BlockSpec((d, tf), lambda i, f: (0, f)),
                  pl.BlockSpec((tf, d), lambda i, f: (f, 0))],
        out_specs=pl.BlockSpec((tm, d), lambda i, f: (i, 0)),
        out_shape=jax.ShapeDtypeStruct((t, d), F32),
        scratch_shapes=[pltpu.VMEM((tm, d), BF16), pltpu.VMEM((tm, d), F32)],
        compiler_params=_params("parallel", "arbitrary"),
        name="ffn_swiglu",
    )(x, gain.reshape(1, d), mod, w_gate, w_up, w_down)


def _normmod_kernel(x_ref, g_ref, mod_ref, o_ref):
    m = mod_ref[0]
    o_ref[...] = _norm_mod(x_ref[...], g_ref[...], m[0:1], m[1:2])


def norm_mod(x, gain, mod, rows_per_batch, tm=512):
    t, d = x.shape
    tm = _tile(rows_per_batch, tm)
    tpb = rows_per_batch // tm
    return pl.pallas_call(
        _normmod_kernel,
        grid=(t // tm,),
        in_specs=[pl.BlockSpec((tm, d), lambda i: (i, 0)),
                  pl.BlockSpec((1, d), lambda i: (0, 0)),
                  pl.BlockSpec((1, N_MOD, d), lambda i: (i // tpb, 0, 0))],
        out_specs=pl.BlockSpec((tm, d), lambda i: (i, 0)),
        out_shape=jax.ShapeDtypeStruct((t, d), F32),
        compiler_params=_params("parallel"),
        name="norm_mod",
    )(x, gain.reshape(1, d), mod)


def s5_operators(lam_re, lam_im, log_step, b_re, b_im, c_re, c_im, n_chunks):
    g_all, p = lam_re.shape
    n = b_re.shape[-1]
    sub, tg = S5_SUB, S5_TILE_GROUPS
    nt = g_all // tg
    lam = lax.complex(lam_re.astype(F32), lam_im.astype(F32))
    lam_dt = lam * jnp.exp(log_step.astype(F32))[:, None]
    lam_bar = jnp.exp(lam_dt)
    b_bar = ((lam_bar - 1.0) / lam)[:, :, None] * lax.complex(b_re.astype(F32), b_im.astype(F32))
    c_mat = lax.complex(c_re.astype(F32), c_im.astype(F32))
    par = ((jnp.arange(tg) % 2)[:, None] == jnp.arange(2)[None, :]).astype(F32)
    ones_n = jnp.ones((n,), F32)

    def lay_in(a):
        z = jnp.einsum('qgpn,gr->qgnrp', a.reshape(nt, tg, p, n), par)
        return z.reshape(nt, tg * n, 2 * p)

    def lay_out(a):
        z = jnp.einsum('qgmp,gr->qrpgm', a.reshape(nt, tg, n, p), par)
        return z.reshape(nt, 2 * p, tg * n)

    lam_g = lam_bar.reshape(nt, tg, p)
    lam_in = jnp.einsum('qgp,n,r->qgnrp', lam_g, ones_n.astype(lam_g.dtype),
                        jnp.ones((2,), lam_g.dtype)).reshape(nt, tg * n, 2 * p)
    lam_out = jnp.einsum('qgp,m,r->qrpgm', lam_g, ones_n.astype(lam_g.dtype),
                         jnp.ones((2,), lam_g.dtype)).reshape(nt, 2 * p, tg * n)
    base = jnp.stack([lay_in(b_bar.real), lay_in(b_bar.imag), lam_in.real, lam_in.imag,
                      lay_out(c_mat.real), lay_out(c_mat.imag), lam_out.real, lam_out.imag], axis=1)

    n_lvl = max(1, (n_chunks - 1).bit_length())
    lv = jnp.exp(lam_dt[None] * (sub * 2.0 ** jnp.arange(n_lvl, dtype=F32))[:, None, None])
    lv = lv.reshape(n_lvl, nt, tg * p)
    lam_lv = jnp.stack([lv.real, lv.imag], axis=2).transpose(1, 0, 2, 3)
    return base, lam_lv


def _gelu_tanh(x):
    return 0.5 * x * (1.0 + jnp.tanh(0.7978845608028654 * (x + 0.044715 * (x * x * x))))


def _s5_kernel(u_ref, base_ref, lam_ref, d_ref, o_ref, ucat, bcat, ccat, krev):
    sub = ccat.shape[0]
    cw = u_ref.shape[1]
    r = u_ref.shape[0] // sub
    sw = bcat.shape[1] // 2
    pair = base_ref.shape[3]

    @pl.when(pl.program_id(1) == 0)
    def _():
        rb = lax.broadcasted_iota(jnp.int32, (cw, sw), 0) // (2 * S5_GROUP)
        cb = lax.broadcasted_iota(jnp.int32, (cw, sw), 1) // pair
        in_mask = rb == cb
        rc = lax.broadcasted_iota(jnp.int32, (sw, cw), 0) // pair
        cc = lax.broadcasted_iota(jnp.int32, (sw, cw), 1) // (2 * S5_GROUP)
        out_mask = rc == cc
        reps = sw // pair

        def expand(x, axis, mask):
            return jnp.where(mask, jnp.concatenate([x] * reps, axis=axis), 0.0).astype(BF16)

        def cmul(ar, ai, br, bi):
            return ar * br - ai * bi, ar * bi + ai * br

        b_r, b_i, lb_r, lb_i = (base_ref[0, k] for k in range(4))
        for l in reversed(range(sub)):
            bcat[l * cw:(l + 1) * cw, 0:sw] = expand(b_r, 1, in_mask)
            bcat[l * cw:(l + 1) * cw, sw:2 * sw] = expand(b_i, 1, in_mask)
            b_r, b_i = cmul(b_r, b_i, lb_r, lb_i)
        b_now = bcat[(sub - 1) * cw:sub * cw, :]

        c_r, c_i, lc_r, lc_i = (base_ref[0, k] for k in range(4, 8))
        c_now = jnp.concatenate([expand(c_r, 0, out_mask), expand(-c_i, 0, out_mask)], axis=0)
        krev[(sub - 1) * cw:sub * cw, :] = _dot(b_now, c_now).astype(BF16)
        for l in range(sub):
            c_r, c_i = cmul(c_r, c_i, lc_r, lc_i)
            ccat[l, 0:sw, :] = expand(c_r, 0, out_mask)
            ccat[l, sw:2 * sw, :] = expand(-c_i, 0, out_mask)
            if l < sub - 1:
                krev[(sub - 2 - l) * cw:(sub - 1 - l) * cw, :] = _dot(b_now, ccat[l]).astype(BF16)

    for l in range(sub):
        ucat[:, l * cw:(l + 1) * cw] = u_ref[pl.ds(l, r, stride=sub), :].astype(BF16)

    v = _dot(ucat[...], bcat[...])
    s_re, s_im = v[:, 0:sw], v[:, sw:2 * sw]
    rowi = lax.broadcasted_iota(jnp.int32, (r, sw), 0)
    shift, lvl = 1, 0
    while shift < r:
        lr = lam_ref[0, lvl, 0:1, :]
        li = lam_ref[0, lvl, 1:2, :]
        keep = rowi >= shift
        p_re = jnp.where(keep, pltpu.roll(s_re, shift, axis=0), 0.0)
        p_im = jnp.where(keep, pltpu.roll(s_im, shift, axis=0), 0.0)
        s_re, s_im = s_re + lr * p_re - li * p_im, s_im + lr * p_im + li * p_re
        shift, lvl = shift * 2, lvl + 1
    first = rowi >= 1
    x_re = jnp.where(first, pltpu.roll(s_re, 1, axis=0), 0.0)
    x_im = jnp.where(first, pltpu.roll(s_im, 1, axis=0), 0.0)
    xb = jnp.concatenate([x_re, x_im], axis=1).astype(BF16)

    for l in range(sub):
        y = _dot(ucat[:, 0:(l + 1) * cw], krev[(sub - 1 - l) * cw:sub * cw, :])
        y = y + _dot(xb, ccat[l])
        ul = u_ref[pl.ds(l, r, stride=sub), :]
        o_ref[pl.ds(l, r, stride=sub), :] = _gelu_tanh(y + d_ref[...] * ul)


def s5_scan_gelu(u, ops, d_skip, batch, seq):
    base, lam_lv = ops
    t, d = u.shape
    nt, n_base, cw, pair = base.shape
    sub = S5_SUB
    sw = S5_TILE_GROUPS * S5_STATE
    n_lvl = lam_lv.shape[1]
    return pl.pallas_call(
        _s5_kernel,
        grid=(nt, batch),
        in_specs=[pl.BlockSpec((seq, cw), lambda c, b: (b, c)),
                  pl.BlockSpec((1, n_base, cw, pair), lambda c, b: (c, 0, 0, 0)),
                  pl.BlockSpec((1, n_lvl, 2, sw), lambda c, b: (c, 0, 0, 0)),
                  pl.BlockSpec((1, cw), lambda c, b: (0, c))],
        out_specs=pl.BlockSpec((seq, cw), lambda c, b: (b, c)),
        out_shape=jax.ShapeDtypeStruct((t, d), F32),
        scratch_shapes=[pltpu.VMEM((seq // sub, sub * cw), BF16),
                        pltpu.VMEM((sub * cw, 2 * sw), BF16),
                        pltpu.VMEM((sub, 2 * sw, cw), BF16),
                        pltpu.VMEM((sub * cw, cw), BF16)],
        compiler_params=_params("parallel", "arbitrary"),
        name="s5_scan",
    )(u, base, lam_lv, d_skip.reshape(1, d))


def _glu_kernel(g_ref, wa_ref, wb_ref, x_ref, mod_ref, o_ref):
    g = g_ref[...].astype(BF16)
    mix = _dot(g, wa_ref[...]) * jax.nn.sigmoid(_dot(g, wb_ref[...]))
    o_ref[...] = x_ref[...] + mod_ref[0][2:3] * mix


def glu_out(g, w_a, w_b, x, mod, rows_per_batch, tm=512, tn=1024):
    t, d = x.shape
    tm = _tile(rows_per_batch, tm)
    tn = min(tn, d)
    tpb = rows_per_batch // tm
    return pl.pallas_call(
        _glu_kernel,
        grid=(d // tn, t // tm),
        in_specs=[pl.BlockSpec((tm, d), lambda j, i: (i, 0)),
                  pl.BlockSpec((d, tn), lambda j, i: (0, j)),
                  pl.BlockSpec((d, tn), lambda j, i: (0, j)),
                  pl.BlockSpec((tm, tn), lambda j, i: (i, j)),
                  pl.BlockSpec((1, N_MOD, tn), lambda j, i: (i // tpb, 0, j))],
        out_specs=pl.BlockSpec((tm, tn), lambda j, i: (i, j)),
        out_shape=jax.ShapeDtypeStruct((t, d), F32),
        compiler_params=_params("parallel", "parallel"),
        name="glu_out",
    )(g, w_a, w_b, x, mod)


def _router_kernel(x_ref, g_ref, mod_ref, rw_ref, rb_ref, h_ref, r_ref):
    m = mod_ref[0]
    h = _norm_mod(x_ref[...], g_ref[...], m[3:4], m[4:5])
    h_ref[...] = h.astype(BF16)
    logits = jnp.dot(h, rw_ref[...], precision=HI, preferred_element_type=F32) + rb_ref[...]
    lane = lax.broadcasted_iota(jnp.int32, logits.shape, 1)
    logits = jnp.where(lane < N_EXPERTS, logits, NEG_INF)
    m1 = jnp.max(logits, axis=1, keepdims=True)
    i1 = jnp.min(jnp.where(logits == m1, lane, LANES), axis=1, keepdims=True)
    rest = jnp.where(lane == i1, NEG_INF, logits)
    m2 = jnp.max(rest, axis=1, keepdims=True)
    i2 = jnp.min(jnp.where(rest == m2, lane, LANES), axis=1, keepdims=True)
    e2 = jnp.exp(m2 - m1)
    g1 = 1.0 / (1.0 + e2)
    g2 = e2 / (1.0 + e2)
    r_ref[...] = jnp.where(lane == 0, i1.astype(F32),
                           jnp.where(lane == 1, i2.astype(F32),
                                     jnp.where(lane == 2, g1, jnp.where(lane == 3, g2, 0.0))))


def moe_router(x, gain, mod, router_w, router_b, rows_per_batch, tm=512):
    t, d = x.shape
    e = router_w.shape[1]
    tm = _tile(rows_per_batch, tm)
    tpb = rows_per_batch // tm
    rw = jnp.pad(router_w.astype(F32), ((0, 0), (0, LANES - e)))
    rb = jnp.pad(router_b.astype(F32), (0, LANES - e)).reshape(1, LANES)
    return pl.pallas_call(
        _router_kernel,
        grid=(t // tm,),
        in_specs=[pl.BlockSpec((tm, d), lambda i: (i, 0)),
                  pl.BlockSpec((1, d), lambda i: (0, 0)),
                  pl.BlockSpec((1, N_MOD, d), lambda i: (i // tpb, 0, 0)),
                  pl.BlockSpec((d, LANES), lambda i: (0, 0)),
                  pl.BlockSpec((1, LANES), lambda i: (0, 0))],
        out_specs=[pl.BlockSpec((tm, d), lambda i: (i, 0)),
                   pl.BlockSpec((tm, LANES), lambda i: (i, 0))],
        out_shape=[jax.ShapeDtypeStruct((t, d), BF16), jax.ShapeDtypeStruct((t, LANES), F32)],
        compiler_params=_params("parallel"),
        name="moe_router",
    )(x, gain.reshape(1, d), mod, rw, rb)


def moe_dispatch(top_e, rows, sub_rows):
    t = top_e.shape[0]
    n_assign = t * TOP_K
    n_blocks = -(-n_assign // rows) + N_EXPERTS
    e_flat = top_e.reshape(-1)
    onehot = (e_flat[:, None] == jnp.arange(N_EXPERTS, dtype=jnp.int32)[None, :]).astype(jnp.int32)
    csum = jnp.cumsum(onehot, axis=0)
    rank = jnp.sum((csum - onehot) * onehot, axis=1)
    counts = csum[-1]
    padded = (counts + rows - 1) // rows * rows
    pad_end = jnp.cumsum(padded)
    pad_start = pad_end - padded
    dest = jnp.sum(onehot * pad_start[None, :], axis=1) + rank
    tok = jnp.arange(n_assign, dtype=jnp.int32) // TOP_K
    spread = jnp.arange(n_blocks * rows, dtype=jnp.int32) % t
    row_tok = spread.at[dest].set(tok)
    n_active = pad_end[-1] // rows
    blk = jnp.arange(n_blocks, dtype=jnp.int32)
    blk_start = jnp.minimum(blk, n_active - 1) * rows
    block_expert = jnp.minimum(jnp.searchsorted(pad_end, blk_start, side='right'),
                               N_EXPERTS - 1).astype(jnp.int32)
    real_rows = jnp.clip(pad_start[block_expert] + counts[block_expert] - blk_start, 0, rows)
    n_sub = jnp.where(blk < n_active, (real_rows + sub_rows - 1) // sub_rows, 0).astype(jnp.int32)
    return row_tok, dest.reshape(t, TOP_K), block_expert, n_sub


def _expert_kernel(be_ref, ns_ref, x_ref, wg_ref, wu_ref, wd_ref, o_ref, acc_ref, *, sub_rows):
    i = pl.program_id(0)
    f = pl.program_id(1)
    rows = x_ref.shape[0]

    @pl.when(f == 0)
    def _():
        acc_ref[...] = jnp.zeros_like(acc_ref)

    def run(n_rows):
        x = x_ref[0:n_rows, :]
        act = (_silu(_dot(x, wg_ref[0].astype(BF16))) * _dot(x, wu_ref[0].astype(BF16))).astype(BF16)
        acc_ref[0:n_rows, :] += _dot(act, wd_ref[0].astype(BF16))

    for s in range(1, rows // sub_rows + 1):
        pl.when(ns_ref[i] == s)(functools.partial(run, s * sub_rows))

    @pl.when(f == pl.num_programs(1) - 1)
    def _():
        o_ref[...] = acc_ref[...].astype(BF16)


def moe_experts(xg, block_expert, n_sub, w_gate, w_up, w_down, layer, rows, sub_rows, tf=256):
    r_tot, d = xg.shape
    f_dim = w_gate.shape[2]
    tf = min(tf, f_dim)
    nf = f_dim // tf
    n_blocks = r_tot // rows
    e0 = layer * N_EXPERTS

    def f_idx(i, f, ns):
        return jnp.where(ns[i] > 0, f, nf - 1)

    grid_spec = pltpu.PrefetchScalarGridSpec(
        num_scalar_prefetch=2,
        grid=(n_blocks, nf),
        in_specs=[pl.BlockSpec((rows, d), lambda i, f, be, ns: (i, 0)),
                  pl.BlockSpec((1, d, tf), lambda i, f, be, ns: (e0 + be[i], 0, f_idx(i, f, ns))),
                  pl.BlockSpec((1, d, tf), lambda i, f, be, ns: (e0 + be[i], 0, f_idx(i, f, ns))),
                  pl.BlockSpec((1, tf, d), lambda i, f, be, ns: (e0 + be[i], f_idx(i, f, ns), 0))],
        out_specs=pl.BlockSpec((rows, d), lambda i, f, be, ns: (i, 0)),
        scratch_shapes=[pltpu.VMEM((rows, d), F32)],
    )
    return pl.pallas_call(
        functools.partial(_expert_kernel, sub_rows=sub_rows),
        grid_spec=grid_spec,
        out_shape=jax.ShapeDtypeStruct((r_tot, d), BF16),
        compiler_params=_params("arbitrary", "arbitrary"),
        name="moe_experts",
    )(block_expert, n_sub, xg, w_gate, w_up, w_down)


def _combine_kernel(x_ref, y1_ref, y2_ref, r_ref, mod_ref, gf_ref, o_ref, *, final_norm):
    r = r_ref[...]
    ff = r[:, 2:3] * y1_ref[...].astype(F32) + r[:, 3:4] * y2_ref[...].astype(F32)
    x_new = x_ref[...] + mod_ref[0][5:6] * ff
    if final_norm:
        ms = jnp.mean(x_new * x_new, axis=-1, keepdims=True)
        x_new = x_new * lax.rsqrt(ms + EPS) * gf_ref[...]
    o_ref[...] = x_new


def moe_combine(x, y1, y2, route, mod, g_final, rows_per_batch, final_norm, tm=512):
    t, d = x.shape
    tm = _tile(rows_per_batch, tm)
    tpb = rows_per_batch // tm
    row = lambda w: pl.BlockSpec((tm, w), lambda i: (i, 0))
    return pl.pallas_call(
        functools.partial(_combine_kernel, final_norm=final_norm),
        grid=(t // tm,),
        in_specs=[row(d), row(d), row(d), row(LANES),
                  pl.BlockSpec((1, N_MOD, d), lambda i: (i // tpb, 0, 0)),
                  pl.BlockSpec((1, d), lambda i: (0, 0))],
        out_specs=row(d),
        out_shape=jax.ShapeDtypeStruct((t, d), F32),
        compiler_params=_params("parallel"),
        name="moe_combine",
    )(x, y1, y2, route, mod, g_final.reshape(1, d))


def kernel(x, c, w_ada, b_ada, g_mix, g_ffn, g_final, w_in, conv_w, b_igate, b_fgate, mh_gain, w_out,
           ffn_w_gate, ffn_w_up, ffn_w_down, s5_lam_re, s5_lam_im, s5_log_step, s5_b_re, s5_b_im,
           s5_c_re, s5_c_im, s5_d, glu_w_a, glu_w_b, router_w, router_b, exp_w_gate, exp_w_up, exp_w_down):
    batch, seq, d = x.shape
    depth = w_ada.shape[0]
    t = batch * seq
    m_width = mh_gain.shape[1]
    heads_b = (d - m_width) // MOBA_HEAD_DIM
    n_gate = 2 * MLSTM_HEADS

    mods = ada_mod(c, w_ada, b_ada)
    xs = x.reshape(t, d)
    for layer in range(depth):
        i = layer // 2
        mod = mods[layer]
        if layer % 2 == 0:
            w = w_in[i]
            g0 = 4 * m_width
            w_big = jnp.concatenate([w[:, :g0], w[:, g0 + n_gate:]], axis=1).astype(BF16)
            w_gates = jnp.pad(w[:, g0:g0 + n_gate], ((0, 0), (0, LANES - n_gate)))
            proj, gates = in_proj(xs, g_mix[layer], mod, w_big, w_gates, seq)
            hm = mlstm_mix(proj, gates, conv_w[i], b_igate[i], b_fgate[i], mh_gain[i], batch, seq)
            hb = moba_mix(proj, g0 // MOBA_HEAD_DIM, batch, seq, heads_b)
            xs = out_proj(hm, hb, w_out[i].astype(BF16), xs, mod, seq)
            f_pad = -ffn_w_gate.shape[2] % 512
            wg = jnp.pad(ffn_w_gate[i], ((0, 0), (0, f_pad))).astype(BF16)
            wu = jnp.pad(ffn_w_up[i], ((0, 0), (0, f_pad))).astype(BF16)
            wd = jnp.pad(ffn_w_down[i], ((0, f_pad), (0, 0))).astype(BF16)
            xs = ffn_swiglu(xs, g_ffn[layer], mod, wg, wu, wd, seq)
        else:
            u = norm_mod(xs, g_mix[layer], mod, seq)
            ops = s5_operators(s5_lam_re[i], s5_lam_im[i], s5_log_step[i], s5_b_re[i], s5_b_im[i],
                               s5_c_re[i], s5_c_im[i], seq // S5_SUB)
            g = s5_scan_gelu(u, ops, s5_d[i], batch, seq)
            xs = glu_out(g, glu_w_a[i].astype(BF16), glu_w_b[i].astype(BF16), xs, mod, seq)
            h, route = moe_router(xs, g_ffn[layer], mod, router_w[i], router_b[i], seq)
            top_e = route[:, 0:TOP_K].astype(jnp.int32)
            row_tok, pos, block_expert, n_sub = moe_dispatch(top_e, MOE_ROWS, MOE_SUB_ROWS)
            stack = lambda w_: w_.reshape((-1,) + w_.shape[2:])
            y_rows = moe_experts(h[row_tok], block_expert, n_sub, stack(exp_w_gate), stack(exp_w_up),
                                 stack(exp_w_down), i, MOE_ROWS, MOE_SUB_ROWS)
            xs = moe_combine(xs, y_rows[pos[:, 0]], y_rows[pos[:, 1]], route, mod, g_final, seq,
                             final_norm=(layer == depth - 1))
    if depth % 2 == 1:
        raise NotImplementedError("final norm is fused into the last (odd) layer")
    return xs.reshape(batch, seq, d)
```

```python
import functools
import math

import jax
import jax.numpy as jnp
from jax import lax
from jax.experimental import pallas as pl
from jax.experimental.pallas import tpu as pltpu

F32 = jnp.float32
BF16 = jnp.bfloat16
HI = lax.Precision.HIGHEST
NEG_INF = float("-inf")

EPS = 1e-6
N_MOD = 6
MLSTM_HEADS = 4
MLSTM_CHUNK = 128
MLSTM_CONV = 4
MOBA_HEAD_DIM = 128
MOBA_BLOCK = 256
MOBA_TOPK = 3
S5_GROUP = 16
S5_STATE = 64
S5_SUB = 16
S5_TILE_GROUPS = 8
N_EXPERTS = 8
TOP_K = 2
MOE_ROWS = 1024
MOE_SUB_ROWS = 256
LANES = 128


def _tile(n, pref, align=8):
    for cand in range(min(pref, n), 0, -1):
        if n % cand == 0 and cand % align == 0:
            return cand
    raise ValueError(f"no {align}-aligned tile divides {n}")


def _params(*sem):
    return pltpu.CompilerParams(dimension_semantics=sem)


def _dot(a, b):
    return jnp.dot(a, b, preferred_element_type=F32)


def _dot_nt(a, b, precision=None):
    return lax.dot_general(a, b, (((1,), (1,)), ((), ())), precision=precision,
                           preferred_element_type=F32)


def _silu(x):
    return x * jax.nn.sigmoid(x)


def _norm_mod(x, gain, shift, scale):
    ms = jnp.mean(x * x, axis=-1, keepdims=True)
    y = x * lax.rsqrt(ms + EPS) * gain
    return y * (1.0 + scale) + shift


def _ada_kernel(c_ref, w_ref, b_ref, o_ref):
    c = c_ref[...]
    o_ref[0] = jnp.dot(_silu(c), w_ref[0], precision=HI, preferred_element_type=F32) + b_ref[0]


def ada_mod(c, w_ada, b_ada, tn=2048):
    depth, d, n = w_ada.shape
    b = c.shape[0]
    bp = 8
    cp = jnp.pad(c, ((0, bp - b), (0, 0)))
    out = pl.pallas_call(
        _ada_kernel,
        grid=(depth, n // tn),
        in_specs=[pl.BlockSpec((bp, d), lambda l, j: (0, 0)),
                  pl.BlockSpec((1, d, tn), lambda l, j: (l, 0, j)),
                  pl.BlockSpec((1, 1, tn), lambda l, j: (l, 0, j))],
        out_specs=pl.BlockSpec((1, bp, tn), lambda l, j: (l, 0, j)),
        out_shape=jax.ShapeDtypeStruct((depth, bp, n), F32),
        compiler_params=_params("parallel", "parallel"),
        name="ada_mod",
    )(cp, w_ada, b_ada.reshape(depth, 1, n))
    return out[:, :b].reshape(depth, b, N_MOD, d)


def _inproj_kernel(x_ref, g_ref, mod_ref, w_ref, wg_ref, o_ref, og_ref, h_ref):
    @pl.when(pl.program_id(1) == 0)
    def _():
        m = mod_ref[0]
        h = _norm_mod(x_ref[...], g_ref[...], m[0:1], m[1:2])
        h_ref[...] = h.astype(BF16)
        og_ref[...] = jnp.dot(h, wg_ref[...], precision=HI, preferred_element_type=F32)

    o_ref[...] = _dot(h_ref[...], w_ref[...]).astype(BF16)


def in_proj(x, gain, mod, w_big, w_gates, rows_per_batch, tm=1024, tn=1024):
    t, d = x.shape
    n = w_big.shape[1]
    tm = _tile(rows_per_batch, tm)
    tn = min(tn, n)
    tpb = rows_per_batch // tm
    return pl.pallas_call(
        _inproj_kernel,
        grid=(t // tm, n // tn),
        in_specs=[pl.BlockSpec((tm, d), lambda i, j: (i, 0)),
                  pl.BlockSpec((1, d), lambda i, j: (0, 0)),
                  pl.BlockSpec((1, N_MOD, d), lambda i, j: (i // tpb, 0, 0)),
                  pl.BlockSpec((d, tn), lambda i, j: (0, j)),
                  pl.BlockSpec((d, LANES), lambda i, j: (0, 0))],
        out_specs=[pl.BlockSpec((tm, tn), lambda i, j: (i, j)),
                   pl.BlockSpec((tm, LANES), lambda i, j: (i, 0))],
        out_shape=[jax.ShapeDtypeStruct((t, n), BF16), jax.ShapeDtypeStruct((t, LANES), F32)],
        scratch_shapes=[pltpu.VMEM((tm, d), BF16)],
        compiler_params=_params("parallel", "arbitrary"),
        name="in_proj",
    )(x, gain.reshape(1, d), mod, w_big, w_gates)


def _mlstm_kernel(bias_ref, q_ref, k_ref, v_ref, o_ref, gi_ref, gf_ref, cwq_ref, cwk_ref, gain_ref,
                  out_ref, qbuf, kbuf, c_st, n_st, m_st):
    chunk = pl.program_id(1)
    L = q_ref.shape[0]
    heads, dh = c_st.shape[0], c_st.shape[1]
    taps = cwq_ref.shape[0]
    halo = 8

    @pl.when(chunk == 0)
    def _():
        qbuf[0:halo] = jnp.zeros((halo, qbuf.shape[1]), F32)
        kbuf[0:halo] = jnp.zeros((halo, kbuf.shape[1]), F32)
        c_st[...] = jnp.zeros_like(c_st)
        n_st[...] = jnp.zeros_like(n_st)
        m_st[...] = jnp.full(m_st.shape, -1e30, F32)

    def conv_silu(src_ref, buf, w_ref):
        buf[halo:halo + L] = src_ref[...].astype(F32)
        w = w_ref[...]
        acc = buf[halo:halo + L] * w[taps - 1:taps]
        for d in range(1, taps):
            acc = acc + buf[pl.ds(halo - d, L), :] * w[taps - 1 - d:taps - d]
        buf[0:halo] = buf[L:L + halo]
        return _silu(acc)

    q_all = conv_silu(q_ref, qbuf, cwq_ref)
    k_all = conv_silu(k_ref, kbuf, cwk_ref) * (dh ** -0.5)

    row = lax.broadcasted_iota(jnp.int32, (L, L), 0)
    col = lax.broadcasted_iota(jnp.int32, (L, L), 1)
    eye = row == col

    def to_col(x_row):
        return jnp.sum(jnp.where(eye, jnp.broadcast_to(x_row, (L, L)), 0.0), axis=1, keepdims=True)

    fz = jnp.concatenate([gf_ref[0, h, 0] + bias_ref[1, h] for h in range(heads)]
                         + [jnp.zeros((8 - heads, L), F32)], axis=0)
    lf_rows = jnp.minimum(fz, 0.0) - jnp.log(1.0 + jnp.exp(-jnp.abs(fz)))
    g_rows = jnp.dot(lf_rows, (row <= col).astype(F32), precision=HI,
                     preferred_element_type=F32)

    for h in range(heads):
        sl = slice(h * dh, (h + 1) * dh)
        q, k, vb = q_all[:, sl], k_all[:, sl], v_ref[:, sl]
        ig_row = gi_ref[0, h, 0] + bias_ref[0, h]
        g_row = g_rows[h:h + 1]
        g_col = to_col(g_row)
        b_row = ig_row - g_row
        d_mat = jnp.where(col <= row, g_col + b_row, NEG_INF)
        m_prev = m_st[h]
        m_inter = g_col + m_prev
        m_t = jnp.maximum(m_inter, jnp.max(d_mat, axis=1, keepdims=True))
        qb = q.astype(BF16)
        kb = k.astype(BF16)
        s = _dot_nt(qb, kb) * jnp.exp(d_mat - m_t)
        decay = jnp.exp(m_inter - m_t)
        num = _dot(s.astype(BF16), vb) + decay * _dot(qb, c_st[h].astype(BF16))
        den = jnp.sum(s, axis=1, keepdims=True) + decay * jnp.sum(q * n_st[h], axis=1, keepdims=True)
        hh = num / jnp.maximum(jnp.abs(den), jnp.exp(-m_t))

        g_last = g_row[:, L - 1:L]
        a_row = g_last + b_row
        m_new = jnp.maximum(g_last + m_prev, jnp.max(a_row, axis=1, keepdims=True))
        w_col = to_col(jnp.exp(a_row - m_new))
        carry = jnp.exp(g_last + m_prev - m_new)
        kw = k * w_col
        c_st[h] = carry * c_st[h] + _dot(kw.T.astype(BF16), vb)
        n_st[h] = carry * n_st[h] + jnp.sum(kw, axis=0, keepdims=True)
        m_st[h] = m_new

        hn = hh * lax.rsqrt(jnp.mean(hh * hh, axis=1, keepdims=True) + EPS) * gain_ref[:, sl]
        out_ref[:, sl] = (hn * jax.nn.sigmoid(o_ref[:, sl].astype(F32))).astype(BF16)


def mlstm_mix(proj, gates, conv_w, b_igate, b_fgate, mh_gain, batch, seq):
    heads, L = MLSTM_HEADS, MLSTM_CHUNK
    width = mh_gain.shape[0]
    dh = width // heads
    nc = seq // L
    t = batch * seq

    def rows(a):
        return a.reshape(batch, nc, L, heads).transpose(0, 3, 1, 2).reshape(batch, heads, nc, 1, L)

    gi = rows(gates[:, 0:heads])
    gf = rows(gates[:, heads:2 * heads])
    bias = jnp.stack([b_igate, b_fgate]).astype(F32)
    blk = lambda off: pl.BlockSpec((L, width), lambda b, c: (b * nc + c, off))
    gspec = pl.BlockSpec((1, heads, 1, 1, L), lambda b, c: (b, 0, c, 0, 0))
    return pl.pallas_call(
        _mlstm_kernel,
        grid=(batch, nc),
        in_specs=[pl.BlockSpec(memory_space=pltpu.SMEM),
                  blk(0), blk(1), blk(2), blk(3), gspec, gspec,
                  pl.BlockSpec((MLSTM_CONV, width), lambda b, c: (0, 0)),
                  pl.BlockSpec((MLSTM_CONV, width), lambda b, c: (0, 1)),
                  pl.BlockSpec((1, width), lambda b, c: (0, 0))],
        out_specs=pl.BlockSpec((L, width), lambda b, c: (b * nc + c, 0)),
        out_shape=jax.ShapeDtypeStruct((t, width), BF16),
        scratch_shapes=[pltpu.VMEM((L + 8, width), F32), pltpu.VMEM((L + 8, width), F32),
                        pltpu.VMEM((heads, dh, dh), F32), pltpu.VMEM((heads, 1, dh), F32),
                        pltpu.VMEM((heads, 1, 1), F32)],
        compiler_params=_params("parallel", "arbitrary"),
        name="mlstm",
    )(bias, proj, proj, proj, proj, gi, gf, conv_w, conv_w, mh_gain.reshape(1, width))


MOBA_MASK_BIAS = -1e9
MOBA_WIDTH_STEP = 4
MOBA_CHUNK_BLOCKS = 2


def _moba_kernel(q_ref, k_ref, v_ref, o_ref, kmean_ref, kaug_ref, qaug_ref, s_ref):
    j = pl.program_id(1)
    blk, dh = o_ref.shape
    seq = k_ref.shape[0]
    nb = seq // blk
    nbp = kmean_ref.shape[0]
    scale = dh ** -0.5

    @pl.when(j == 0)
    def _():
        kmean_ref[...] = jnp.zeros_like(kmean_ref)
        for b in range(nb):
            kmean_ref[b:b + 1, :] = jnp.mean(k_ref[b * blk:(b + 1) * blk, :].astype(F32), axis=0,
                                             keepdims=True)
        kaug_ref[:, 0:dh] = k_ref[...]
        key_blk = lax.broadcasted_iota(jnp.int32, (seq, LANES), 0) // blk
        lane = lax.broadcasted_iota(jnp.int32, (seq, LANES), 1)
        kaug_ref[:, dh:dh + LANES] = jnp.where(key_blk == lane, 1.0, 0.0).astype(BF16)

        q_all = q_ref[...]
        qaug_ref[:, 0:dh] = q_all
        gate_t = _dot_nt(kmean_ref[...], q_all.astype(F32), precision=HI)
        blk_id = lax.broadcasted_iota(jnp.int32, gate_t.shape, 0)
        q_blk = lax.broadcasted_iota(jnp.int32, gate_t.shape, 1) // blk
        valid = blk_id < q_blk
        sc = jnp.where(valid, gate_t, NEG_INF)
        beaten = jnp.zeros(gate_t.shape, F32)
        for b2 in range(nb):
            other = sc[b2:b2 + 1, :]
            wins = (other > sc) | ((other == sc) & (b2 < blk_id))
            beaten = beaten + wins.astype(F32)
        chosen = valid & (beaten < MOBA_TOPK)
        bias_t = jnp.where(chosen, 0.0, MOBA_MASK_BIAS)
        pad = jnp.zeros((LANES - nbp, blk), F32)
        for b in range(nb):
            piece = jnp.concatenate([bias_t[:, b * blk:(b + 1) * blk], pad], axis=0).T
            qaug_ref[b * blk:(b + 1) * blk, dh:dh + LANES] = piece.astype(BF16)

    start = pl.multiple_of(j * blk, blk)
    q_aug = qaug_ref[pl.ds(start, blk), :]
    q = q_aug[:, 0:dh]

    row = lax.broadcasted_iota(jnp.int32, (blk, blk), 0)
    col = lax.broadcasted_iota(jnp.int32, (blk, blk), 1)
    s_own = jnp.where(col <= row, _dot_nt(q, k_ref[pl.ds(start, blk), :]), NEG_INF)
    m_own = jnp.max(s_own, axis=1, keepdims=True)
    v_own = v_ref[pl.ds(start, blk), :]
    exp_scale = scale * math.log2(math.e)

    @pl.when(j == 0)
    def _():
        p = jnp.exp2((s_own - m_own) * exp_scale)
        o_ref[...] = (_dot(p.astype(BF16), v_own) / jnp.sum(p, axis=1, keepdims=True)).astype(BF16)

    def attend(n_blocks):
        w = n_blocks * blk
        step = MOBA_CHUNK_BLOCKS * blk
        chunks = [(lo_, min(lo_ + step, w)) for lo_ in range(0, w, step)]
        mx = jnp.full((blk, LANES), NEG_INF, F32)
        for lo_, hi_ in chunks:
            s_c = _dot_nt(q_aug, kaug_ref[lo_:hi_, :])
            s_ref[:, lo_:hi_] = s_c
            for t_ in range((hi_ - lo_) // LANES):
                mx = jnp.maximum(mx, s_c[:, t_ * LANES:(t_ + 1) * LANES])
        m = jnp.maximum(jnp.max(mx, axis=1, keepdims=True), m_own)
        p_own = jnp.exp2((s_own - m) * exp_scale)
        acc = _dot(p_own.astype(BF16), v_own)
        lsum = p_own[:, 0:LANES]
        for t_ in range(1, blk // LANES):
            lsum = lsum + p_own[:, t_ * LANES:(t_ + 1) * LANES]
        for lo_, hi_ in chunks:
            p = jnp.exp2((s_ref[:, lo_:hi_] - m) * exp_scale)
            for t_ in range((hi_ - lo_) // LANES):
                lsum = lsum + p[:, t_ * LANES:(t_ + 1) * LANES]
            acc = acc + _dot(p.astype(BF16), v_ref[lo_:hi_, :])
        o_ref[...] = (acc / jnp.sum(lsum, axis=1, keepdims=True)).astype(BF16)

    lo = 0
    for hi in list(range(MOBA_WIDTH_STEP, nb - 1, MOBA_WIDTH_STEP)) + [nb - 1]:
        pl.when((j > lo) & (j <= hi))(functools.partial(attend, hi))
        lo = hi


def moba_mix(proj, col0, batch, seq, heads):
    dh, blk = MOBA_HEAD_DIM, MOBA_BLOCK
    nb = seq // blk
    nbp = -(-nb // 8) * 8
    t = batch * seq
    return pl.pallas_call(
        _moba_kernel,
        grid=(batch * heads, nb),
        in_specs=[pl.BlockSpec((seq, dh), lambda g, j: (g // heads, col0 + g % heads)),
                  pl.BlockSpec((seq, dh), lambda g, j: (g // heads, col0 + heads + g % heads)),
                  pl.BlockSpec((seq, dh), lambda g, j: (g // heads, col0 + 2 * heads + g % heads))],
        out_specs=pl.BlockSpec((blk, dh), lambda g, j: ((g // heads) * nb + j, g % heads)),
        out_shape=jax.ShapeDtypeStruct((t, heads * dh), BF16),
        scratch_shapes=[pltpu.VMEM((nbp, dh), F32), pltpu.VMEM((seq, dh + LANES), BF16),
                        pltpu.VMEM((seq, dh + LANES), BF16), pltpu.VMEM((blk, seq), F32)],
        compiler_params=_params("parallel", "arbitrary"),
        name="moba",
    )(proj, proj, proj)


def _outproj_kernel(hm_ref, hb_ref, w1_ref, w2_ref, x_ref, mod_ref, o_ref):
    acc = _dot(hm_ref[...], w1_ref[...]) + _dot(hb_ref[...], w2_ref[...])
    o_ref[...] = x_ref[...] + mod_ref[0][2:3] * acc


def out_proj(hm, hb, w_out, x, mod, rows_per_batch, tm=512):
    t, d = x.shape
    k1, k2 = hm.shape[1], hb.shape[1]
    tm = _tile(rows_per_batch, tm)
    tpb = rows_per_batch // tm
    return pl.pallas_call(
        _outproj_kernel,
        grid=(t // tm,),
        in_specs=[pl.BlockSpec((tm, k1), lambda i: (i, 0)),
                  pl.BlockSpec((tm, k2), lambda i: (i, 0)),
                  pl.BlockSpec((k1, d), lambda i: (0, 0)),
                  pl.BlockSpec((k2, d), lambda i: (0, 0)),
                  pl.BlockSpec((tm, d), lambda i: (i, 0)),
                  pl.BlockSpec((1, N_MOD, d), lambda i: (i // tpb, 0, 0))],
        out_specs=pl.BlockSpec((tm, d), lambda i: (i, 0)),
        out_shape=jax.ShapeDtypeStruct((t, d), F32),
        compiler_params=_params("parallel"),
        name="out_proj",
    )(hm, hb, w_out[:k1], w_out[k1:], x, mod)


def _ffn_kernel(x_ref, g_ref, mod_ref, wg_ref, wu_ref, wd_ref, o_ref, h_ref):
    f = pl.program_id(1)
    tm = x_ref.shape[0]
    piece = _tile(tm, 256)

    @pl.when(f == 0)
    def _():
        m = mod_ref[0]
        for r0 in range(0, tm, piece):
            rs = slice(r0, r0 + piece)
            h_ref[rs, :] = _norm_mod(x_ref[rs, :], g_ref[...], m[3:4], m[4:5]).astype(BF16)
        o_ref[...] = jnp.zeros_like(o_ref)

    h = h_ref[...]
    act = (_silu(_dot(h, wg_ref[...])) * _dot(h, wu_ref[...])).astype(BF16)
    o_ref[...] += _dot(act, wd_ref[...])

    @pl.when(f == pl.num_programs(1) - 1)
    def _():
        gate = mod_ref[0][5:6]
        for r0 in range(0, tm, piece):
            rs = slice(r0, r0 + piece)
            o_ref[rs, :] = x_ref[rs, :] + gate * o_ref[rs, :]


def ffn_swiglu(x, gain, mod, w_gate, w_up, w_down, rows_per_batch, tm=1024, tf=512):
    t, d = x.shape
    f_dim = w_gate.shape[1]
    tm = _tile(rows_per_batch, tm)
    tf = min(tf, f_dim)
    tpb = rows_per_batch // tm
    return pl.pallas_call(
        _ffn_kernel,
        grid=(t // tm, f_dim // tf),
        in_specs=[pl.BlockSpec((tm, d), lambda i, f: (i, 0), pipeline_mode=pl.Buffered(1)),
                  pl.BlockSpec((1, d), lambda i, f: (0, 0)),
                  pl.BlockSpec((1, N_MOD, d), lambda i, f: (i // tpb, 0, 0)),
                  pl.BlockSpec((d, tf), lambda i, f: (0, f)),
                  pl.BlockSpec((d, tf), lambda i, f: (0, f)),
                  pl.BlockSpec((tf, d), lambda i, f: (f, 0))],
        out_specs=pl.BlockSpec((tm, d), lambda i, f: (i, 0)),
        out_shape=jax.ShapeDtypeStruct((t, d), F32),
        scratch_shapes=[pltpu.VMEM((tm, d), BF16)],
        compiler_params=_params("parallel", "arbitrary"),
        name="ffn_swiglu",
    )(x, gain.reshape(1, d), mod, w_gate, w_up, w_down)


def _normmod_kernel(x_ref, g_ref, mod_ref, o_ref):
    m = mod_ref[0]
    o_ref[...] = _norm_mod(x_ref[...], g_ref[...], m[0:1], m[1:2])


def norm_mod(x, gain, mod, rows_per_batch, tm=512):
    t, d = x.shape
    tm = _tile(rows_per_batch, tm)
    tpb = rows_per_batch // tm
    return pl.pallas_call(
        _normmod_kernel,
        grid=(t // tm,),
        in_specs=[pl.BlockSpec((tm, d), lambda i: (i, 0)),
                  pl.BlockSpec((1, d), lambda i: (0, 0)),
                  pl.BlockSpec((1, N_MOD, d), lambda i: (i // tpb, 0, 0))],
        out_specs=pl.BlockSpec((tm, d), lambda i: (i, 0)),
        out_shape=jax.ShapeDtypeStruct((t, d), F32),
        compiler_params=_params("parallel"),
        name="norm_mod",
    )(x, gain.reshape(1, d), mod)


def s5_operators(lam_re, lam_im, log_step, b_re, b_im, c_re, c_im, n_chunks):
    g_all, p = lam_re.shape
    n = b_re.shape[-1]
    sub, tg = S5_SUB, S5_TILE_GROUPS
    nt = g_all // tg
    lam = lax.complex(lam_re.astype(F32), lam_im.astype(F32))
    lam_dt = lam * jnp.exp(log_step.astype(F32))[:, None]
    lam_bar = jnp.exp(lam_dt)
    b_bar = ((lam_bar - 1.0) / lam)[:, :, None] * lax.complex(b_re.astype(F32), b_im.astype(F32))
    c_mat = lax.complex(c_re.astype(F32), c_im.astype(F32))
    par = ((jnp.arange(tg) % 2)[:, None] == jnp.arange(2)[None, :]).astype(F32)
    ones_n = jnp.ones((n,), F32)

    def lay_in(a):
        z = jnp.einsum('qgpn,gr->qgnrp', a.reshape(nt, tg, p, n), par)
        return z.reshape(nt, tg * n, 2 * p)

    def lay_out(a):
        z = jnp.einsum('qgmp,gr->qrpgm', a.reshape(nt, tg, n, p), par)
        return z.reshape(nt, 2 * p, tg * n)

    lam_g = lam_bar.reshape(nt, tg, p)
    lam_in = jnp.einsum('qgp,n,r->qgnrp', lam_g, ones_n.astype(lam_g.dtype),
                        jnp.ones((2,), lam_g.dtype)).reshape(nt, tg * n, 2 * p)
    lam_out = jnp.einsum('qgp,m,r->qrpgm', lam_g, ones_n.astype(lam_g.dtype),
                         jnp.ones((2,), lam_g.dtype)).reshape(nt, 2 * p, tg * n)
    base = jnp.stack([lay_in(b_bar.real), lay_in(b_bar.imag), lam_in.real, lam_in.imag,
                      lay_out(c_mat.real), lay_out(c_mat.imag), lam_out.real, lam_out.imag], axis=1)

    n_lvl = max(1, (n_chunks - 1).bit_length())
    lv = jnp.exp(lam_dt[None] * (sub * 2.0 ** jnp.arange(n_lvl, dtype=F32))[:, None, None])
    lv = lv.reshape(n_lvl, nt, tg * p)
    lam_lv = jnp.stack([lv.real, lv.imag], axis=2).transpose(1, 0, 2, 3)
    return base, lam_lv


def _gelu_tanh(x):
    return 0.5 * x * (1.0 + jnp.tanh(0.7978845608028654 * (x + 0.044715 * (x * x * x))))


def _s5_kernel(u_ref, base_ref, lam_ref, d_ref, o_ref, ucat, bcat, ccat, krev, tpair):
    sub = 2 * ccat.shape[0]
    cw = u_ref.shape[1]
    r = u_ref.shape[0] // sub
    sw = bcat.shape[1] // 2
    pair = base_ref.shape[3]

    @pl.when(pl.program_id(1) == 0)
    def _():
        rb = lax.broadcasted_iota(jnp.int32, (cw, sw), 0) // (2 * S5_GROUP)
        cb = lax.broadcasted_iota(jnp.int32, (cw, sw), 1) // pair
        in_mask = rb == cb
        rc = lax.broadcasted_iota(jnp.int32, (sw, cw), 0) // pair
        cc = lax.broadcasted_iota(jnp.int32, (sw, cw), 1) // (2 * S5_GROUP)
        out_mask = rc == cc
        reps = sw // pair

        def expand(x, axis, mask):
            return jnp.where(mask, jnp.concatenate([x] * reps, axis=axis), 0.0).astype(BF16)

        def cmul(ar, ai, br, bi):
            return ar * br - ai * bi, ar * bi + ai * br

        b_r, b_i, lb_r, lb_i = (base_ref[0, k] for k in range(4))
        for l in reversed(range(sub)):
            bcat[l * cw:(l + 1) * cw, 0:sw] = expand(b_r, 1, in_mask)
            bcat[l * cw:(l + 1) * cw, sw:2 * sw] = expand(b_i, 1, in_mask)
            b_r, b_i = cmul(b_r, b_i, lb_r, lb_i)
        b_now = bcat[(sub - 1) * cw:sub * cw, :]

        c_r, c_i, lc_r, lc_i = (base_ref[0, k] for k in range(4, 8))
        c_now = jnp.concatenate([expand(c_r, 0, out_mask), expand(-c_i, 0, out_mask)], axis=0)
        krev[(sub - 1) * cw:sub * cw, :] = _dot(b_now, c_now).astype(BF16)
        for l in range(sub):
            c_r, c_i = cmul(c_r, c_i, lc_r, lc_i)
            half = slice((l % 2) * cw, (l % 2 + 1) * cw)
            ccat[l // 2, 0:sw, half] = expand(c_r, 0, out_mask)
            ccat[l // 2, sw:2 * sw, half] = expand(-c_i, 0, out_mask)
            if l < sub - 1:
                krev[(sub - 2 - l) * cw:(sub - 1 - l) * cw, :] = _dot(b_now, ccat[l // 2, :, half]).astype(BF16)
        for p in range(sub // 2):
            off = p * (p + 1) * cw
            n0 = (2 * p + 1) * cw
            tpair[off:off + n0, 0:cw] = krev[(sub - 1 - 2 * p) * cw:sub * cw, :]
            tpair[off + n0:off + n0 + cw, 0:cw] = jnp.zeros((cw, cw), BF16)
            tpair[off:off + n0 + cw, cw:2 * cw] = krev[(sub - 2 - 2 * p) * cw:sub * cw, :]

    for l in range(sub):
        ucat[:, l * cw:(l + 1) * cw] = u_ref[pl.ds(l, r, stride=sub), :].astype(BF16)

    v = _dot(ucat[...], bcat[...])
    s_re, s_im = v[:, 0:sw], v[:, sw:2 * sw]
    rowi = lax.broadcasted_iota(jnp.int32, (r, sw), 0)
    shift, lvl = 1, 0
    while shift < r:
        lr = lam_ref[0, lvl, 0:1, :]
        li = lam_ref[0, lvl, 1:2, :]
        keep = rowi >= shift
        p_re = jnp.where(keep, pltpu.roll(s_re, shift, axis=0), 0.0)
        p_im = jnp.where(keep, pltpu.roll(s_im, shift, axis=0), 0.0)
        s_re, s_im = s_re + lr * p_re - li * p_im, s_im + lr * p_im + li * p_re
        shift, lvl = shift * 2, lvl + 1
    first = rowi >= 1
    x_re = jnp.where(first, pltpu.roll(s_re, 1, axis=0), 0.0)
    x_im = jnp.where(first, pltpu.roll(s_im, 1, axis=0), 0.0)
    xb = jnp.concatenate([x_re, x_im], axis=1).astype(BF16)

    for p in range(sub // 2):
        off = p * (p + 1) * cw
        n_in = (2 * p + 2) * cw
        y2 = _dot(ucat[:, 0:n_in], tpair[off:off + n_in, :]) + _dot(xb, ccat[p])
        for l in (2 * p, 2 * p + 1):
            y = y2[:, (l % 2) * cw:(l % 2 + 1) * cw]
            ul = u_ref[pl.ds(l, r, stride=sub), :]
            o_ref[pl.ds(l, r, stride=sub), :] = _gelu_tanh(y + d_ref[...] * ul)


def s5_scan_gelu(u, ops, d_skip, batch, seq):
    base, lam_lv = ops
    t, d = u.shape
    nt, n_base, cw, pair = base.shape
    sub = S5_SUB
    sw = S5_TILE_GROUPS * S5_STATE
    n_lvl = lam_lv.shape[1]
    return pl.pallas_call(
        _s5_kernel,
        grid=(nt, batch),
        in_specs=[pl.BlockSpec((seq, cw), lambda c, b: (b, c)),
                  pl.BlockSpec((1, n_base, cw, pair), lambda c, b: (c, 0, 0, 0)),
                  pl.BlockSpec((1, n_lvl, 2, sw), lambda c, b: (c, 0, 0, 0)),
                  pl.BlockSpec((1, cw), lambda c, b: (0, c))],
        out_specs=pl.BlockSpec((seq, cw), lambda c, b: (b, c)),
        out_shape=jax.ShapeDtypeStruct((t, d), F32),
        scratch_shapes=[pltpu.VMEM((seq // sub, sub * cw), BF16),
                        pltpu.VMEM((sub * cw, 2 * sw), BF16),
                        pltpu.VMEM((sub // 2, 2 * sw, 2 * cw), BF16),
                        pltpu.VMEM((sub * cw, cw), BF16),
                        pltpu.VMEM(((sub // 2) * (sub // 2 + 1) * cw, 2 * cw), BF16)],
        compiler_params=_params("parallel", "arbitrary"),
        name="s5_scan",
    )(u, base, lam_lv, d_skip.reshape(1, d))


def _glu_kernel(g_ref, wa_ref, wb_ref, x_ref, mod_ref, o_ref):
    g = g_ref[...].astype(BF16)
    mix = _dot(g, wa_ref[...]) * jax.nn.sigmoid(_dot(g, wb_ref[...]))
    o_ref[...] = x_ref[...] + mod_ref[0][2:3] * mix


def glu_out(g, w_a, w_b, x, mod, rows_per_batch, tm=512, tn=1024):
    t, d = x.shape
    tm = _tile(rows_per_batch, tm)
    tn = min(tn, d)
    tpb = rows_per_batch // tm
    return pl.pallas_call(
        _glu_kernel,
        grid=(d // tn, t // tm),
        in_specs=[pl.BlockSpec((tm, d), lambda j, i: (i, 0)),
                  pl.BlockSpec((d, tn), lambda j, i: (0, j)),
                  pl.BlockSpec((d, tn), lambda j, i: (0, j)),
                  pl.BlockSpec((tm, tn), lambda j, i: (i, j)),
                  pl.BlockSpec((1, N_MOD, tn), lambda j, i: (i // tpb, 0, j))],
        out_specs=pl.BlockSpec((tm, tn), lambda j, i: (i, j)),
        out_shape=jax.ShapeDtypeStruct((t, d), F32),
        compiler_params=_params("parallel", "parallel"),
        name="glu_out",
    )(g, w_a, w_b, x, mod)


def _router_kernel(x_ref, g_ref, mod_ref, rw_ref, rb_ref, h_ref, r_ref):
    m = mod_ref[0]
    h = _norm_mod(x_ref[...], g_ref[...], m[3:4], m[4:5])
    h_ref[...] = h.astype(BF16)
    logits = jnp.dot(h, rw_ref[...], precision=HI, preferred_element_type=F32) + rb_ref[...]
    lane = lax.broadcasted_iota(jnp.int32, logits.shape, 1)
    logits = jnp.where(lane < N_EXPERTS, logits, NEG_INF)
    m1 = jnp.max(logits, axis=1, keepdims=True)
    i1 = jnp.min(jnp.where(logits == m1, lane, LANES), axis=1, keepdims=True)
    rest = jnp.where(lane == i1, NEG_INF, logits)
    m2 = jnp.max(rest, axis=1, keepdims=True)
    i2 = jnp.min(jnp.where(rest == m2, lane, LANES), axis=1, keepdims=True)
    e2 = jnp.exp(m2 - m1)
    g1 = 1.0 / (1.0 + e2)
    g2 = e2 / (1.0 + e2)
    r_ref[...] = jnp.where(lane == 0, i1.astype(F32),
                           jnp.where(lane == 1, i2.astype(F32),
                                     jnp.where(lane == 2, g1, jnp.where(lane == 3, g2, 0.0))))


def moe_router(x, gain, mod, router_w, router_b, rows_per_batch, tm=512):
    t, d = x.shape
    e = router_w.shape[1]
    tm = _tile(rows_per_batch, tm)
    tpb = rows_per_batch // tm
    rw = jnp.pad(router_w.astype(F32), ((0, 0), (0, LANES - e)))
    rb = jnp.pad(router_b.astype(F32), (0, LANES - e)).reshape(1, LANES)
    return pl.pallas_call(
        _router_kernel,
        grid=(t // tm,),
        in_specs=[pl.BlockSpec((tm, d), lambda i: (i, 0)),
                  pl.BlockSpec((1, d), lambda i: (0, 0)),
                  pl.BlockSpec((1, N_MOD, d), lambda i: (i // tpb, 0, 0)),
                  pl.BlockSpec((d, LANES), lambda i: (0, 0)),
                  pl.BlockSpec((1, LANES), lambda i: (0, 0))],
        out_specs=[pl.BlockSpec((tm, d), lambda i: (i, 0)),
                   pl.BlockSpec((tm, LANES), lambda i: (i, 0))],
        out_shape=[jax.ShapeDtypeStruct((t, d), BF16), jax.ShapeDtypeStruct((t, LANES), F32)],
        compiler_params=_params("parallel"),
        name="moe_router",
    )(x, gain.reshape(1, d), mod, rw, rb)


def moe_dispatch(top_e, rows, sub_rows):
    t = top_e.shape[0]
    n_assign = t * TOP_K
    n_blocks = -(-n_assign // rows) + N_EXPERTS
    e_flat = top_e.reshape(-1)
    onehot = (e_flat[:, None] == jnp.arange(N_EXPERTS, dtype=jnp.int32)[None, :]).astype(jnp.int32)
    csum = jnp.cumsum(onehot, axis=0)
    rank = jnp.sum((csum - onehot) * onehot, axis=1)
    counts = csum[-1]
    padded = (counts + rows - 1) // rows * rows
    pad_end = jnp.cumsum(padded)
    pad_start = pad_end - padded
    dest = jnp.sum(onehot * pad_start[None, :], axis=1) + rank
    tok = jnp.arange(n_assign, dtype=jnp.int32) // TOP_K
    spread = jnp.arange(n_blocks * rows, dtype=jnp.int32) % t
    row_tok = spread.at[dest].set(tok)
    n_active = pad_end[-1] // rows
    blk = jnp.arange(n_blocks, dtype=jnp.int32)
    blk_start = jnp.minimum(blk, n_active - 1) * rows
    block_expert = jnp.minimum(jnp.searchsorted(pad_end, blk_start, side='right'),
                               N_EXPERTS - 1).astype(jnp.int32)
    real_rows = jnp.clip(pad_start[block_expert] + counts[block_expert] - blk_start, 0, rows)
    n_sub = jnp.where(blk < n_active, (real_rows + sub_rows - 1) // sub_rows, 0).astype(jnp.int32)
    return row_tok, dest.reshape(t, TOP_K), block_expert, n_sub


def _expert_kernel(be_ref, ns_ref, x_ref, wg_ref, wu_ref, wd_ref, o_ref, acc_ref, *, sub_rows):
    i = pl.program_id(0)
    f = pl.program_id(1)
    rows = x_ref.shape[0]

    @pl.when(f == 0)
    def _():
        acc_ref[...] = jnp.zeros_like(acc_ref)

    def run(n_rows):
        x = x_ref[0:n_rows, :]
        act = (_silu(_dot(x, wg_ref[0].astype(BF16))) * _dot(x, wu_ref[0].astype(BF16))).astype(BF16)
        acc_ref[0:n_rows, :] += _dot(act, wd_ref[0].astype(BF16))

    for s in range(1, rows // sub_rows + 1):
        pl.when(ns_ref[i] == s)(functools.partial(run, s * sub_rows))

    @pl.when(f == pl.num_programs(1) - 1)
    def _():
        o_ref[...] = acc_ref[...].astype(BF16)


def moe_experts(xg, block_expert, n_sub, w_gate, w_up, w_down, layer, rows, sub_rows, tf=256):
    r_tot, d = xg.shape
    f_dim = w_gate.shape[2]
    tf = min(tf, f_dim)
    nf = f_dim // tf
    n_blocks = r_tot // rows
    e0 = layer * N_EXPERTS

    def f_idx(i, f, ns):
        return jnp.where(ns[i] > 0, f, nf - 1)

    grid_spec = pltpu.PrefetchScalarGridSpec(
        num_scalar_prefetch=2,
        grid=(n_blocks, nf),
        in_specs=[pl.BlockSpec((rows, d), lambda i, f, be, ns: (i, 0)),
                  pl.BlockSpec((1, d, tf), lambda i, f, be, ns: (e0 + be[i], 0, f_idx(i, f, ns))),
                  pl.BlockSpec((1, d, tf), lambda i, f, be, ns: (e0 + be[i], 0, f_idx(i, f, ns))),
                  pl.BlockSpec((1, tf, d), lambda i, f, be, ns: (e0 + be[i], f_idx(i, f, ns), 0))],
        out_specs=pl.BlockSpec((rows, d), lambda i, f, be, ns: (i, 0)),
        scratch_shapes=[pltpu.VMEM((rows, d), F32)],
    )
    return pl.pallas_call(
        functools.partial(_expert_kernel, sub_rows=sub_rows),
        grid_spec=grid_spec,
        out_shape=jax.ShapeDtypeStruct((r_tot, d), BF16),
        compiler_params=_params("arbitrary", "arbitrary"),
        name="moe_experts",
    )(block_expert, n_sub, xg, w_gate, w_up, w_down)


def _combine_kernel(x_ref, y1_ref, y2_ref, r_ref, mod_ref, gf_ref, o_ref, *, final_norm):
    r = r_ref[...]
    ff = r[:, 2:3] * y1_ref[...].astype(F32) + r[:, 3:4] * y2_ref[...].astype(F32)
    x_new = x_ref[...] + mod_ref[0][5:6] * ff
    if final_norm:
        ms = jnp.mean(x_new * x_new, axis=-1, keepdims=True)
        x_new = x_new * lax.rsqrt(ms + EPS) * gf_ref[...]
    o_ref[...] = x_new


def moe_combine(x, y1, y2, route, mod, g_final, rows_per_batch, final_norm, tm=512):
    t, d = x.shape
    tm = _tile(rows_per_batch, tm)
    tpb = rows_per_batch // tm
    row = lambda w: pl.BlockSpec((tm, w), lambda i: (i, 0))
    return pl.pallas_call(
        functools.partial(_combine_kernel, final_norm=final_norm),
        grid=(t // tm,),
        in_specs=[row(d), row(d), row(d), row(LANES),
                  pl.BlockSpec((1, N_MOD, d), lambda i: (i // tpb, 0, 0)),
                  pl.BlockSpec((1, d), lambda i: (0, 0))],
        out_specs=row(d),
        out_shape=jax.ShapeDtypeStruct((t, d), F32),
        compiler_params=_params("parallel"),
        name="moe_combine",
    )(x, y1, y2, route, mod, g_final.reshape(1, d))


def kernel(x, c, w_ada, b_ada, g_mix, g_ffn, g_final, w_in, conv_w, b_igate, b_fgate, mh_gain, w_out,
           ffn_w_gate, ffn_w_up, ffn_w_down, s5_lam_re, s5_lam_im, s5_log_step, s5_b_re, s5_b_im,
           s5_c_re, s5_c_im, s5_d, glu_w_a, glu_w_b, router_w, router_b, exp_w_gate, exp_w_up, exp_w_down):
    batch, seq, d = x.shape
    depth = w_ada.shape[0]
    t = batch * seq
    m_width = mh_gain.shape[1]
    heads_b = (d - m_width) // MOBA_HEAD_DIM
    n_gate = 2 * MLSTM_HEADS

    mods = ada_mod(c, w_ada, b_ada)
    xs = x.reshape(t, d)
    for layer in range(depth):
        i = layer // 2
        mod = mods[layer]
        if layer % 2 == 0:
            w = w_in[i]
            g0 = 4 * m_width
            w_big = jnp.concatenate([w[:, :g0], w[:, g0 + n_gate:]], axis=1).astype(BF16)
            w_gates = jnp.pad(w[:, g0:g0 + n_gate], ((0, 0), (0, LANES - n_gate)))
            proj, gates = in_proj(xs, g_mix[layer], mod, w_big, w_gates, seq)
            hm = mlstm_mix(proj, gates, conv_w[i], b_igate[i], b_fgate[i], mh_gain[i], batch, seq)
            hb = moba_mix(proj, g0 // MOBA_HEAD_DIM, batch, seq, heads_b)
            xs = out_proj(hm, hb, w_out[i].astype(BF16), xs, mod, seq)
            f_pad = -ffn_w_gate.shape[2] % 512
            wg = jnp.pad(ffn_w_gate[i], ((0, 0), (0, f_pad))).astype(BF16)
            wu = jnp.pad(ffn_w_up[i], ((0, 0), (0, f_pad))).astype(BF16)
            wd = jnp.pad(ffn_w_down[i], ((0, f_pad), (0, 0))).astype(BF16)
            xs = ffn_swiglu(xs, g_ffn[layer], mod, wg, wu, wd, seq)
        else:
            u = norm_mod(xs, g_mix[layer], mod, seq)
            ops = s5_operators(s5_lam_re[i], s5_lam_im[i], s5_log_step[i], s5_b_re[i], s5_b_im[i],
                               s5_c_re[i], s5_c_im[i], seq // S5_SUB)
            g = s5_scan_gelu(u, ops, s5_d[i], batch, seq)
            xs = glu_out(g, glu_w_a[i].astype(BF16), glu_w_b[i].astype(BF16), xs, mod, seq)
            h, route = moe_router(xs, g_ffn[layer], mod, router_w[i], router_b[i], seq)
            top_e = route[:, 0:TOP_K].astype(jnp.int32)
            row_tok, pos, block_expert, n_sub = moe_dispatch(top_e, MOE_ROWS, MOE_SUB_ROWS)
            stack = lambda w_: w_.reshape((-1,) + w_.shape[2:])
            y_rows = moe_experts(h[row_tok], block_expert, n_sub, stack(exp_w_gate), stack(exp_w_up),
                                 stack(exp_w_down), i, MOE_ROWS, MOE_SUB_ROWS)
            xs = moe_combine(xs, y_rows[pos[:, 0]], y_rows[pos[:, 1]], route, mod, g_final, seq,
                             final_norm=(layer == depth - 1))
    if depth % 2 == 1:
        raise NotImplementedError("final norm is fused into the last (odd) layer")
    return xs.reshape(batch, seq, d)
```

```python
import functools
import math

import jax
import jax.numpy as jnp
from jax import lax
from jax.experimental import pallas as pl
from jax.experimental.pallas import tpu as pltpu

F32 = jnp.float32
BF16 = jnp.bfloat16
HI = lax.Precision.HIGHEST
NEG_INF = float("-inf")

EPS = 1e-6
N_MOD = 6
MLSTM_HEADS = 4
MLSTM_CHUNK = 128
MLSTM_CONV = 4
MOBA_HEAD_DIM = 128
MOBA_BLOCK = 256
MOBA_TOPK = 3
S5_GROUP = 16
S5_STATE = 64
S5_SUB = 16
S5_TILE_GROUPS = 8
N_EXPERTS = 8
TOP_K = 2
MOE_ROWS = 1024
MOE_SUB_ROWS = 256
LANES = 128


def _tile(n, pref, align=8):
    for cand in range(min(pref, n), 0, -1):
        if n % cand == 0 and cand % align == 0:
            return cand
    raise ValueError(f"no {align}-aligned tile divides {n}")


def _params(*sem):
    return pltpu.CompilerParams(dimension_semantics=sem)


def _dot(a, b):
    return jnp.dot(a, b, preferred_element_type=F32)


def _dot_nt(a, b, precision=None):
    return lax.dot_general(a, b, (((1,), (1,)), ((), ())), precision=precision,
                           preferred_element_type=F32)


def _silu(x):
    return x * jax.nn.sigmoid(x)


def _split_cols(w):
    hi = w.astype(BF16)
    lo = (w - hi.astype(F32)).astype(BF16)
    return jnp.concatenate([hi, lo], axis=1)


def _dot_split(x, w_cat):
    hi = x.astype(BF16)
    lo = (x - hi.astype(F32)).astype(BF16)
    r = _dot(hi, w_cat) + _dot(lo, w_cat)
    n = w_cat.shape[1] // 2
    return r[:, :n] + r[:, n:]


def _norm_mod(x, gain, shift, scale):
    ms = jnp.mean(x * x, axis=-1, keepdims=True)
    y = x * lax.rsqrt(ms + EPS) * gain
    return y * (1.0 + scale) + shift


def _ada_kernel(c_ref, w_ref, b_ref, o_ref):
    c = c_ref[...]
    o_ref[0] = jnp.dot(_silu(c), w_ref[0], precision=HI, preferred_element_type=F32) + b_ref[0]


def ada_mod(c, w_ada, b_ada, tn=2048):
    depth, d, n = w_ada.shape
    b = c.shape[0]
    bp = 8
    cp = jnp.pad(c, ((0, bp - b), (0, 0)))
    out = pl.pallas_call(
        _ada_kernel,
        grid=(depth, n // tn),
        in_specs=[pl.BlockSpec((bp, d), lambda l, j: (0, 0)),
                  pl.BlockSpec((1, d, tn), lambda l, j: (l, 0, j)),
                  pl.BlockSpec((1, 1, tn), lambda l, j: (l, 0, j))],
        out_specs=pl.BlockSpec((1, bp, tn), lambda l, j: (l, 0, j)),
        out_shape=jax.ShapeDtypeStruct((depth, bp, n), F32),
        compiler_params=_params("parallel", "parallel"),
        name="ada_mod",
    )(cp, w_ada, b_ada.reshape(depth, 1, n))
    return out[:, :b].reshape(depth, b, N_MOD, d)


def _inproj_kernel(x_ref, g_ref, mod_ref, w_ref, wg_ref, o_ref, og_ref, h_ref):
    @pl.when(pl.program_id(1) == 0)
    def _():
        m = mod_ref[0]
        h = _norm_mod(x_ref[...], g_ref[...], m[0:1], m[1:2])
        h_ref[...] = h.astype(BF16)
        og_ref[...] = _dot_split(h, wg_ref[...])

    o_ref[...] = _dot(h_ref[...], w_ref[...]).astype(BF16)


def in_proj(x, gain, mod, w_big, w_gates, rows_per_batch, tm=1024, tn=1024):
    t, d = x.shape
    n = w_big.shape[1]
    tm = _tile(rows_per_batch, tm)
    tn = min(tn, n)
    tpb = rows_per_batch // tm
    return pl.pallas_call(
        _inproj_kernel,
        grid=(t // tm, n // tn),
        in_specs=[pl.BlockSpec((tm, d), lambda i, j: (i, 0)),
                  pl.BlockSpec((1, d), lambda i, j: (0, 0)),
                  pl.BlockSpec((1, N_MOD, d), lambda i, j: (i // tpb, 0, 0)),
                  pl.BlockSpec((d, tn), lambda i, j: (0, j)),
                  pl.BlockSpec((d, 2 * LANES), lambda i, j: (0, 0))],
        out_specs=[pl.BlockSpec((tm, tn), lambda i, j: (i, j)),
                   pl.BlockSpec((tm, LANES), lambda i, j: (i, 0))],
        out_shape=[jax.ShapeDtypeStruct((t, n), BF16), jax.ShapeDtypeStruct((t, LANES), F32)],
        scratch_shapes=[pltpu.VMEM((tm, d), BF16)],
        compiler_params=_params("parallel", "arbitrary"),
        name="in_proj",
    )(x, gain.reshape(1, d), mod, w_big, w_gates)


def _mlstm_kernel(bias_ref, q_ref, k_ref, v_ref, o_ref, gi_ref, gf_ref, cwq_ref, cwk_ref, gain_ref,
                  out_ref, qbuf, kbuf, c_st, n_st, m_st):
    chunk = pl.program_id(1)
    L = q_ref.shape[0]
    heads, dh = c_st.shape[0], c_st.shape[1]
    taps = cwq_ref.shape[0]
    halo = 8

    @pl.when(chunk == 0)
    def _():
        qbuf[0:halo] = jnp.zeros((halo, qbuf.shape[1]), F32)
        kbuf[0:halo] = jnp.zeros((halo, kbuf.shape[1]), F32)
        c_st[...] = jnp.zeros_like(c_st)
        n_st[...] = jnp.zeros_like(n_st)
        m_st[...] = jnp.full(m_st.shape, -1e30, F32)

    def conv_silu(src_ref, buf, w_ref):
        buf[halo:halo + L] = src_ref[...].astype(F32)
        w = w_ref[...]
        acc = buf[halo:halo + L] * w[taps - 1:taps]
        for d in range(1, taps):
            acc = acc + buf[pl.ds(halo - d, L), :] * w[taps - 1 - d:taps - d]
        buf[0:halo] = buf[L:L + halo]
        return _silu(acc)

    q_all = conv_silu(q_ref, qbuf, cwq_ref)
    k_all = conv_silu(k_ref, kbuf, cwk_ref) * (dh ** -0.5)

    row = lax.broadcasted_iota(jnp.int32, (L, L), 0)
    col = lax.broadcasted_iota(jnp.int32, (L, L), 1)
    eye = row == col

    def to_col(x_row):
        return jnp.sum(jnp.where(eye, jnp.broadcast_to(x_row, (L, L)), 0.0), axis=1, keepdims=True)

    fz = jnp.concatenate([gf_ref[0, h, 0] + bias_ref[1, h] for h in range(heads)]
                         + [jnp.zeros((8 - heads, L), F32)], axis=0)
    lf_rows = jnp.minimum(fz, 0.0) - jnp.log(1.0 + jnp.exp(-jnp.abs(fz)))
    g_rows = jnp.dot(lf_rows, (row <= col).astype(F32), precision=HI,
                     preferred_element_type=F32)

    for h in range(heads):
        sl = slice(h * dh, (h + 1) * dh)
        q, k, vb = q_all[:, sl], k_all[:, sl], v_ref[:, sl]
        ig_row = gi_ref[0, h, 0] + bias_ref[0, h]
        g_row = g_rows[h:h + 1]
        g_col = to_col(g_row)
        b_row = ig_row - g_row
        d_mat = jnp.where(col <= row, g_col + b_row, NEG_INF)
        m_prev = m_st[h]
        m_inter = g_col + m_prev
        m_t = jnp.maximum(m_inter, jnp.max(d_mat, axis=1, keepdims=True))
        qb = q.astype(BF16)
        kb = k.astype(BF16)
        s = _dot_nt(qb, kb) * jnp.exp(d_mat - m_t)
        decay = jnp.exp(m_inter - m_t)
        num = _dot(s.astype(BF16), vb) + decay * _dot(qb, c_st[h].astype(BF16))
        den = jnp.sum(s, axis=1, keepdims=True) + decay * jnp.sum(q * n_st[h], axis=1, keepdims=True)
        hh = num / jnp.maximum(jnp.abs(den), jnp.exp(-m_t))

        g_last = g_row[:, L - 1:L]
        a_row = g_last + b_row
        m_new = jnp.maximum(g_last + m_prev, jnp.max(a_row, axis=1, keepdims=True))
        w_col = to_col(jnp.exp(a_row - m_new))
        carry = jnp.exp(g_last + m_prev - m_new)
        kw = k * w_col
        c_st[h] = carry * c_st[h] + _dot(kw.T.astype(BF16), vb)
        n_st[h] = carry * n_st[h] + jnp.sum(kw, axis=0, keepdims=True)
        m_st[h] = m_new

        hn = hh * lax.rsqrt(jnp.mean(hh * hh, axis=1, keepdims=True) + EPS) * gain_ref[:, sl]
        out_ref[:, sl] = (hn * jax.nn.sigmoid(o_ref[:, sl].astype(F32))).astype(BF16)


def mlstm_mix(proj, gates, conv_w, b_igate, b_fgate, mh_gain, batch, seq):
    heads, L = MLSTM_HEADS, MLSTM_CHUNK
    width = mh_gain.shape[0]
    dh = width // heads
    nc = seq // L
    t = batch * seq

    def rows(a):
        return a.reshape(batch, nc, L, heads).transpose(0, 3, 1, 2).reshape(batch, heads, nc, 1, L)

    gi = rows(gates[:, 0:heads])
    gf = rows(gates[:, heads:2 * heads])
    bias = jnp.stack([b_igate, b_fgate]).astype(F32)
    blk = lambda off: pl.BlockSpec((L, width), lambda b, c: (b * nc + c, off))
    gspec = pl.BlockSpec((1, heads, 1, 1, L), lambda b, c: (b, 0, c, 0, 0))
    return pl.pallas_call(
        _mlstm_kernel,
        grid=(batch, nc),
        in_specs=[pl.BlockSpec(memory_space=pltpu.SMEM),
                  blk(0), blk(1), blk(2), blk(3), gspec, gspec,
                  pl.BlockSpec((MLSTM_CONV, width), lambda b, c: (0, 0)),
                  pl.BlockSpec((MLSTM_CONV, width), lambda b, c: (0, 1)),
                  pl.BlockSpec((1, width), lambda b, c: (0, 0))],
        out_specs=pl.BlockSpec((L, width), lambda b, c: (b * nc + c, 0)),
        out_shape=jax.ShapeDtypeStruct((t, width), BF16),
        scratch_shapes=[pltpu.VMEM((L + 8, width), F32), pltpu.VMEM((L + 8, width), F32),
                        pltpu.VMEM((heads, dh, dh), F32), pltpu.VMEM((heads, 1, dh), F32),
                        pltpu.VMEM((heads, 1, 1), F32)],
        compiler_params=_params("parallel", "arbitrary"),
        name="mlstm",
    )(bias, proj, proj, proj, proj, gi, gf, conv_w, conv_w, mh_gain.reshape(1, width))


MOBA_MASK_BIAS = -1e9
MOBA_WIDTH_STEP = 4
MOBA_CHUNK_BLOCKS = 2
MOBA_HEAD_GROUP = 2


def _moba_kernel(q_ref, k_ref, v_ref, o_ref, kmean_ref, kaug_ref, qaug_ref, s_ref):
    j = pl.program_id(1)
    blk = o_ref.shape[0]
    n_h, nbp, dh = kmean_ref.shape
    seq = k_ref.shape[0]
    nb = seq // blk
    exp_scale = dh ** -0.5 * math.log2(math.e)

    @pl.when(j == 0)
    def _():
        key_blk = lax.broadcasted_iota(jnp.int32, (seq, LANES), 0) // blk
        lane = lax.broadcasted_iota(jnp.int32, (seq, LANES), 1)
        block_onehot = jnp.where(key_blk == lane, 1.0, 0.0).astype(BF16)
        kmean_ref[...] = jnp.zeros_like(kmean_ref)
        for h in range(n_h):
            hs = slice(h * dh, (h + 1) * dh)
            for b in range(nb):
                kmean_ref[h, b:b + 1, :] = jnp.mean(k_ref[b * blk:(b + 1) * blk, hs].astype(F32), axis=0,
                                                    keepdims=True)
            kaug_ref[h, :, 0:dh] = k_ref[:, hs]
            kaug_ref[h, :, dh:dh + LANES] = block_onehot

            q_all = q_ref[:, hs]
            qaug_ref[h, :, 0:dh] = q_all
            gate_t = _dot_nt(kmean_ref[h], q_all.astype(F32), precision=HI)
            blk_id = lax.broadcasted_iota(jnp.int32, gate_t.shape, 0)
            q_blk = lax.broadcasted_iota(jnp.int32, gate_t.shape, 1) // blk
            valid = blk_id < q_blk
            sc = jnp.where(valid, gate_t, NEG_INF)
            beaten = jnp.zeros(gate_t.shape, F32)
            for b2 in range(nb):
                other = sc[b2:b2 + 1, :]
                wins = (other > sc) | ((other == sc) & (b2 < blk_id))
                beaten = beaten + wins.astype(F32)
            chosen = valid & (beaten < MOBA_TOPK)
            bias_t = jnp.where(chosen, 0.0, MOBA_MASK_BIAS)
            pad = jnp.zeros((LANES - nbp, blk), F32)
            for b in range(nb):
                piece = jnp.concatenate([bias_t[:, b * blk:(b + 1) * blk], pad], axis=0).T
                qaug_ref[h, b * blk:(b + 1) * blk, dh:dh + LANES] = piece.astype(BF16)

    start = pl.multiple_of(j * blk, blk)
    row = lax.broadcasted_iota(jnp.int32, (blk, blk), 0)
    col = lax.broadcasted_iota(jnp.int32, (blk, blk), 1)

    own = []
    for h in range(n_h):
        hs = slice(h * dh, (h + 1) * dh)
        q_aug = qaug_ref[h, pl.ds(start, blk), :]
        s_own = jnp.where(col <= row, _dot_nt(q_aug[:, 0:dh], k_ref[pl.ds(start, blk), hs]), NEG_INF)
        own.append((q_aug, s_own, jnp.max(s_own, axis=1, keepdims=True), v_ref[pl.ds(start, blk), hs]))

    @pl.when(j == 0)
    def _():
        for h, (_, s_own, m_own, v_own) in enumerate(own):
            p = jnp.exp2((s_own - m_own) * exp_scale)
            out = _dot(p.astype(BF16), v_own) / jnp.sum(p, axis=1, keepdims=True)
            o_ref[:, h * dh:(h + 1) * dh] = out.astype(BF16)

    def attend(n_blocks):
        w = n_blocks * blk
        step = MOBA_CHUNK_BLOCKS * blk
        chunks = [(lo_, min(lo_ + step, w)) for lo_ in range(0, w, step)]
        maxes = []
        for h, (q_aug, _, m_own, _) in enumerate(own):
            mx = jnp.full((blk, LANES), NEG_INF, F32)
            for lo_, hi_ in chunks:
                s_c = _dot_nt(q_aug, kaug_ref[h, lo_:hi_, :])
                s_ref[h, :, lo_:hi_] = s_c
                for t_ in range((hi_ - lo_) // LANES):
                    mx = jnp.maximum(mx, s_c[:, t_ * LANES:(t_ + 1) * LANES])
            maxes.append(jnp.maximum(jnp.max(mx, axis=1, keepdims=True), m_own))
        for h, (_, s_own, _, v_own) in enumerate(own):
            hs = slice(h * dh, (h + 1) * dh)
            m = maxes[h]
            p_own = jnp.exp2((s_own - m) * exp_scale)
            acc = _dot(p_own.astype(BF16), v_own)
            lsum = p_own[:, 0:LANES]
            for t_ in range(1, blk // LANES):
                lsum = lsum + p_own[:, t_ * LANES:(t_ + 1) * LANES]
            for lo_, hi_ in chunks:
                p = jnp.exp2((s_ref[h, :, lo_:hi_] - m) * exp_scale)
                for t_ in range((hi_ - lo_) // LANES):
                    lsum = lsum + p[:, t_ * LANES:(t_ + 1) * LANES]
                acc = acc + _dot(p.astype(BF16), v_ref[lo_:hi_, hs])
            o_ref[:, hs] = (acc / jnp.sum(lsum, axis=1, keepdims=True)).astype(BF16)

    lo = 0
    for hi in list(range(MOBA_WIDTH_STEP, nb - 1, MOBA_WIDTH_STEP)) + [nb - 1]:
        pl.when((j > lo) & (j <= hi))(functools.partial(attend, hi))
        lo = hi


def moba_mix(proj, col0, batch, seq, heads):
    dh, blk, grp = MOBA_HEAD_DIM, MOBA_BLOCK, MOBA_HEAD_GROUP
    nb = seq // blk
    nbp = -(-nb // 8) * 8
    t = batch * seq
    n_grp = heads // grp
    c0 = col0 // grp
    wide = lambda off: pl.BlockSpec((seq, grp * dh), lambda g, j: (g // n_grp, c0 + off * n_grp + g % n_grp))
    return pl.pallas_call(
        _moba_kernel,
        grid=(batch * n_grp, nb),
        in_specs=[wide(0), wide(1), wide(2)],
        out_specs=pl.BlockSpec((blk, grp * dh), lambda g, j: ((g // n_grp) * nb + j, g % n_grp)),
        out_shape=jax.ShapeDtypeStruct((t, heads * dh), BF16),
        scratch_shapes=[pltpu.VMEM((grp, nbp, dh), F32), pltpu.VMEM((grp, seq, dh + LANES), BF16),
                        pltpu.VMEM((grp, seq, dh + LANES), BF16), pltpu.VMEM((grp, blk, seq), F32)],
        compiler_params=_params("parallel", "arbitrary"),
        name="moba",
    )(proj, proj, proj)


def _outproj_kernel(hm_ref, hb_ref, w1_ref, w2_ref, x_ref, mod_ref, o_ref):
    acc = _dot(hm_ref[...], w1_ref[...]) + _dot(hb_ref[...], w2_ref[...])
    o_ref[...] = x_ref[...] + mod_ref[0][2:3] * acc


def out_proj(hm, hb, w_out, x, mod, rows_per_batch, tm=512):
    t, d = x.shape
    k1, k2 = hm.shape[1], hb.shape[1]
    tm = _tile(rows_per_batch, tm)
    tpb = rows_per_batch // tm
    return pl.pallas_call(
        _outproj_kernel,
        grid=(t // tm,),
        in_specs=[pl.BlockSpec((tm, k1), lambda i: (i, 0)),
                  pl.BlockSpec((tm, k2), lambda i: (i, 0)),
                  pl.BlockSpec((k1, d), lambda i: (0, 0)),
                  pl.BlockSpec((k2, d), lambda i: (0, 0)),
                  pl.BlockSpec((tm, d), lambda i: (i, 0)),
                  pl.BlockSpec((1, N_MOD, d), lambda i: (i // tpb, 0, 0))],
        out_specs=pl.BlockSpec((tm, d), lambda i: (i, 0)),
        out_shape=jax.ShapeDtypeStruct((t, d), F32),
        compiler_params=_params("parallel"),
        name="out_proj",
    )(hm, hb, w_out[:k1], w_out[k1:], x, mod)


def _ffn_kernel(x_ref, g_ref, mod_ref, wg_ref, wu_ref, wd_ref, o_ref, h_ref, acc_ref):
    f = pl.program_id(1)

    @pl.when(f == 0)
    def _():
        m = mod_ref[0]
        h_ref[...] = _norm_mod(x_ref[...], g_ref[...], m[3:4], m[4:5]).astype(BF16)
        acc_ref[...] = jnp.zeros_like(acc_ref)

    h = h_ref[...]
    act = (_silu(_dot(h, wg_ref[...])) * _dot(h, wu_ref[...])).astype(BF16)
    acc_ref[...] += _dot(act, wd_ref[...])

    @pl.when(f == pl.num_programs(1) - 1)
    def _():
        o_ref[...] = x_ref[...] + mod_ref[0][5:6] * acc_ref[...]


def ffn_swiglu(x, gain, mod, w_gate, w_up, w_down, rows_per_batch, tm=512, tf=512):
    t, d = x.shape
    f_dim = w_gate.shape[1]
    tm = _tile(rows_per_batch, tm)
    tf = min(tf, f_dim)
    tpb = rows_per_batch // tm
    return pl.pallas_call(
        _ffn_kernel,
        grid=(t // tm, f_dim // tf),
        in_specs=[pl.BlockSpec((tm, d), lambda i, f: (i, 0)),
                  pl.BlockSpec((1, d), lambda i, f: (0, 0)),
                  pl.BlockSpec((1, N_MOD, d), lambda i, f: (i // tpb, 0, 0)),
                  pl.BlockSpec((d, tf), lambda i, f: (0, f)),
                  pl.BlockSpec((d, tf), lambda i, f: (0, f)),
                  pl.BlockSpec((tf, d), lambda i, f: (f, 0))],
        out_specs=pl.BlockSpec((tm, d), lambda i, f: (i, 0)),
        out_shape=jax.ShapeDtypeStruct((t, d), F32),
        scratch_shapes=[pltpu.VMEM((tm, d), BF16), pltpu.VMEM((tm, d), F32)],
        compiler_params=_params("parallel", "arbitrary"),
        name="ffn_swiglu",
    )(x, gain.reshape(1, d), mod, w_gate, w_up, w_down)


def _normmod_kernel(x_ref, g_ref, mod_ref, o_ref):
    m = mod_ref[0]
    o_ref[...] = _norm_mod(x_ref[...], g_ref[...], m[0:1], m[1:2])


def norm_mod(x, gain, mod, rows_per_batch, tm=512):
    t, d = x.shape
    tm = _tile(rows_per_batch, tm)
    tpb = rows_per_batch // tm
    return pl.pallas_call(
        _normmod_kernel,
        grid=(t // tm,),
        in_specs=[pl.BlockSpec((tm, d), lambda i: (i, 0)),
                  pl.BlockSpec((1, d), lambda i: (0, 0)),
                  pl.BlockSpec((1, N_MOD, d), lambda i: (i // tpb, 0, 0))],
        out_specs=pl.BlockSpec((tm, d), lambda i: (i, 0)),
        out_shape=jax.ShapeDtypeStruct((t, d), F32),
        compiler_params=_params("parallel"),
        name="norm_mod",
    )(x, gain.reshape(1, d), mod)


def s5_operators(lam_re, lam_im, log_step, b_re, b_im, c_re, c_im, n_chunks):
    g_all, p = lam_re.shape
    n = b_re.shape[-1]
    sub, tg = S5_SUB, S5_TILE_GROUPS
    nt = g_all // tg
    lam = lax.complex(lam_re.astype(F32), lam_im.astype(F32))
    lam_dt = lam * jnp.exp(log_step.astype(F32))[:, None]
    lam_bar = jnp.exp(lam_dt)
    b_bar = ((lam_bar - 1.0) / lam)[:, :, None] * lax.complex(b_re.astype(F32), b_im.astype(F32))
    c_mat = lax.complex(c_re.astype(F32), c_im.astype(F32))
    par = ((jnp.arange(tg) % 2)[:, None] == jnp.arange(2)[None, :]).astype(F32)
    ones_n = jnp.ones((n,), F32)

    def lay_in(a):
        z = jnp.einsum('qgpn,gr->qgnrp', a.reshape(nt, tg, p, n), par)
        return z.reshape(nt, tg * n, 2 * p)

    def lay_out(a):
        z = jnp.einsum('qgmp,gr->qrpgm', a.reshape(nt, tg, n, p), par)
        return z.reshape(nt, 2 * p, tg * n)

    lam_g = lam_bar.reshape(nt, tg, p)
    lam_in = jnp.einsum('qgp,n,r->qgnrp', lam_g, ones_n.astype(lam_g.dtype),
                        jnp.ones((2,), lam_g.dtype)).reshape(nt, tg * n, 2 * p)
    lam_out = jnp.einsum('qgp,m,r->qrpgm', lam_g, ones_n.astype(lam_g.dtype),
                         jnp.ones((2,), lam_g.dtype)).reshape(nt, 2 * p, tg * n)
    base = jnp.stack([lay_in(b_bar.real), lay_in(b_bar.imag), lam_in.real, lam_in.imag,
                      lay_out(c_mat.real), lay_out(c_mat.imag), lam_out.real, lam_out.imag], axis=1)

    n_lvl = max(1, (n_chunks - 1).bit_length())
    lv = jnp.exp(lam_dt[None] * (sub * 2.0 ** jnp.arange(n_lvl, dtype=F32))[:, None, None])
    lv = lv.reshape(n_lvl, nt, tg * p)
    lam_lv = jnp.stack([lv.real, lv.imag], axis=2).transpose(1, 0, 2, 3)
    return base, lam_lv


def _gelu_tanh(x):
    return 0.5 * x * (1.0 + jnp.tanh(0.7978845608028654 * (x + 0.044715 * (x * x * x))))


def _s5_kernel(u_ref, base_ref, lam_ref, d_ref, o_ref, ucat, bcat, ccat, krev, tpair):
    sub = 2 * ccat.shape[0]
    cw = u_ref.shape[1]
    r = u_ref.shape[0] // sub
    sw = bcat.shape[1] // 2
    pair = base_ref.shape[3]

    @pl.when(pl.program_id(1) == 0)
    def _():
        rb = lax.broadcasted_iota(jnp.int32, (cw, sw), 0) // (2 * S5_GROUP)
        cb = lax.broadcasted_iota(jnp.int32, (cw, sw), 1) // pair
        in_mask = rb == cb
        rc = lax.broadcasted_iota(jnp.int32, (sw, cw), 0) // pair
        cc = lax.broadcasted_iota(jnp.int32, (sw, cw), 1) // (2 * S5_GROUP)
        out_mask = rc == cc
        reps = sw // pair

        def expand(x, axis, mask):
            return jnp.where(mask, jnp.concatenate([x] * reps, axis=axis), 0.0).astype(BF16)

        def cmul(ar, ai, br, bi):
            return ar * br - ai * bi, ar * bi + ai * br

        b_r, b_i, lb_r, lb_i = (base_ref[0, k] for k in range(4))
        for l in reversed(range(sub)):
            bcat[l * cw:(l + 1) * cw, 0:sw] = expand(b_r, 1, in_mask)
            bcat[l * cw:(l + 1) * cw, sw:2 * sw] = expand(b_i, 1, in_mask)
            b_r, b_i = cmul(b_r, b_i, lb_r, lb_i)
        b_now = bcat[(sub - 1) * cw:sub * cw, :]

        c_r, c_i, lc_r, lc_i = (base_ref[0, k] for k in range(4, 8))
        c_now = jnp.concatenate([expand(c_r, 0, out_mask), expand(-c_i, 0, out_mask)], axis=0)
        krev[(sub - 1) * cw:sub * cw, :] = _dot(b_now, c_now).astype(BF16)
        for l in range(sub):
            c_r, c_i = cmul(c_r, c_i, lc_r, lc_i)
            half = slice((l % 2) * cw, (l % 2 + 1) * cw)
            ccat[l // 2, 0:sw, half] = expand(c_r, 0, out_mask)
            ccat[l // 2, sw:2 * sw, half] = expand(-c_i, 0, out_mask)
            if l < sub - 1:
                krev[(sub - 2 - l) * cw:(sub - 1 - l) * cw, :] = _dot(b_now, ccat[l // 2, :, half]).astype(BF16)
        for p in range(sub // 2):
            off = p * (p + 1) * cw
            n0 = (2 * p + 1) * cw
            tpair[off:off + n0, 0:cw] = krev[(sub - 1 - 2 * p) * cw:sub * cw, :]
            tpair[off + n0:off + n0 + cw, 0:cw] = jnp.zeros((cw, cw), BF16)
            tpair[off:off + n0 + cw, cw:2 * cw] = krev[(sub - 2 - 2 * p) * cw:sub * cw, :]

    for l in range(sub):
        ucat[:, l * cw:(l + 1) * cw] = u_ref[pl.ds(l, r, stride=sub), :].astype(BF16)

    v = _dot(ucat[...], bcat[...])
    s_re, s_im = v[:, 0:sw], v[:, sw:2 * sw]
    rowi = lax.broadcasted_iota(jnp.int32, (r, sw), 0)
    shift, lvl = 1, 0
    while shift < r:
        lr = lam_ref[0, lvl, 0:1, :]
        li = lam_ref[0, lvl, 1:2, :]
        keep = rowi >= shift
        p_re = jnp.where(keep, pltpu.roll(s_re, shift, axis=0), 0.0)
        p_im = jnp.where(keep, pltpu.roll(s_im, shift, axis=0), 0.0)
        s_re, s_im = s_re + lr * p_re - li * p_im, s_im + lr * p_im + li * p_re
        shift, lvl = shift * 2, lvl + 1
    first = rowi >= 1
    x_re = jnp.where(first, pltpu.roll(s_re, 1, axis=0), 0.0)
    x_im = jnp.where(first, pltpu.roll(s_im, 1, axis=0), 0.0)
    xb = jnp.concatenate([x_re, x_im], axis=1).astype(BF16)

    for p in range(sub // 2):
        off = p * (p + 1) * cw
        n_in = (2 * p + 2) * cw
        y2 = _dot(ucat[:, 0:n_in], tpair[off:off + n_in, :]) + _dot(xb, ccat[p])
        for l in (2 * p, 2 * p + 1):
            y = y2[:, (l % 2) * cw:(l % 2 + 1) * cw]
            ul = u_ref[pl.ds(l, r, stride=sub), :]
            o_ref[pl.ds(l, r, stride=sub), :] = _gelu_tanh(y + d_ref[...] * ul)


def s5_scan_gelu(u, ops, d_skip, batch, seq):
    base, lam_lv = ops
    t, d = u.shape
    nt, n_base, cw, pair = base.shape
    sub = S5_SUB
    sw = S5_TILE_GROUPS * S5_STATE
    n_lvl = lam_lv.shape[1]
    return pl.pallas_call(
        _s5_kernel,
        grid=(nt, batch),
        in_specs=[pl.BlockSpec((seq, cw), lambda c, b: (b, c)),
                  pl.BlockSpec((1, n_base, cw, pair), lambda c, b: (c, 0, 0, 0)),
                  pl.BlockSpec((1, n_lvl, 2, sw), lambda c, b: (c, 0, 0, 0)),
                  pl.BlockSpec((1, cw), lambda c, b: (0, c))],
        out_specs=pl.BlockSpec((seq, cw), lambda c, b: (b, c)),
        out_shape=jax.ShapeDtypeStruct((t, d), F32),
        scratch_shapes=[pltpu.VMEM((seq // sub, sub * cw), BF16),
                        pltpu.VMEM((sub * cw, 2 * sw), BF16),
                        pltpu.VMEM((sub // 2, 2 * sw, 2 * cw), BF16),
                        pltpu.VMEM((sub * cw, cw), BF16),
                        pltpu.VMEM(((sub // 2) * (sub // 2 + 1) * cw, 2 * cw), BF16)],
        compiler_params=_params("parallel", "arbitrary"),
        name="s5_scan",
    )(u, base, lam_lv, d_skip.reshape(1, d))


def _glu_kernel(g_ref, wa_ref, wb_ref, x_ref, mod_ref, o_ref):
    g = g_ref[...].astype(BF16)
    mix = _dot(g, wa_ref[...]) * jax.nn.sigmoid(_dot(g, wb_ref[...]))
    o_ref[...] = x_ref[...] + mod_ref[0][2:3] * mix


def glu_out(g, w_a, w_b, x, mod, rows_per_batch, tm=512, tn=1024):
    t, d = x.shape
    tm = _tile(rows_per_batch, tm)
    tn = min(tn, d)
    tpb = rows_per_batch // tm
    return pl.pallas_call(
        _glu_kernel,
        grid=(d // tn, t // tm),
        in_specs=[pl.BlockSpec((tm, d), lambda j, i: (i, 0)),
                  pl.BlockSpec((d, tn), lambda j, i: (0, j)),
                  pl.BlockSpec((d, tn), lambda j, i: (0, j)),
                  pl.BlockSpec((tm, tn), lambda j, i: (i, j)),
                  pl.BlockSpec((1, N_MOD, tn), lambda j, i: (i // tpb, 0, j))],
        out_specs=pl.BlockSpec((tm, tn), lambda j, i: (i, j)),
        out_shape=jax.ShapeDtypeStruct((t, d), F32),
        compiler_params=_params("parallel", "parallel"),
        name="glu_out",
    )(g, w_a, w_b, x, mod)


def _router_kernel(x_ref, g_ref, mod_ref, rw_ref, rb_ref, h_ref, r_ref):
    m = mod_ref[0]
    h = _norm_mod(x_ref[...], g_ref[...], m[3:4], m[4:5])
    h_ref[...] = h.astype(BF16)
    logits = _dot_split(h, rw_ref[...]) + rb_ref[...]
    lane = lax.broadcasted_iota(jnp.int32, logits.shape, 1)
    logits = jnp.where(lane < N_EXPERTS, logits, NEG_INF)
    m1 = jnp.max(logits, axis=1, keepdims=True)
    i1 = jnp.min(jnp.where(logits == m1, lane, LANES), axis=1, keepdims=True)
    rest = jnp.where(lane == i1, NEG_INF, logits)
    m2 = jnp.max(rest, axis=1, keepdims=True)
    i2 = jnp.min(jnp.where(rest == m2, lane, LANES), axis=1, keepdims=True)
    e2 = jnp.exp(m2 - m1)
    g1 = 1.0 / (1.0 + e2)
    g2 = e2 / (1.0 + e2)
    r_ref[...] = jnp.where(lane == 0, i1.astype(F32),
                           jnp.where(lane == 1, i2.astype(F32),
                                     jnp.where(lane == 2, g1, jnp.where(lane == 3, g2, 0.0))))


def moe_router(x, gain, mod, router_w, router_b, rows_per_batch, tm=512):
    t, d = x.shape
    e = router_w.shape[1]
    tm = _tile(rows_per_batch, tm)
    tpb = rows_per_batch // tm
    rw = _split_cols(jnp.pad(router_w.astype(F32), ((0, 0), (0, LANES - e))))
    rb = jnp.pad(router_b.astype(F32), (0, LANES - e)).reshape(1, LANES)
    return pl.pallas_call(
        _router_kernel,
        grid=(t // tm,),
        in_specs=[pl.BlockSpec((tm, d), lambda i: (i, 0)),
                  pl.BlockSpec((1, d), lambda i: (0, 0)),
                  pl.BlockSpec((1, N_MOD, d), lambda i: (i // tpb, 0, 0)),
                  pl.BlockSpec((d, 2 * LANES), lambda i: (0, 0)),
                  pl.BlockSpec((1, LANES), lambda i: (0, 0))],
        out_specs=[pl.BlockSpec((tm, d), lambda i: (i, 0)),
                   pl.BlockSpec((tm, LANES), lambda i: (i, 0))],
        out_shape=[jax.ShapeDtypeStruct((t, d), BF16), jax.ShapeDtypeStruct((t, LANES), F32)],
        compiler_params=_params("parallel"),
        name="moe_router",
    )(x, gain.reshape(1, d), mod, rw, rb)


def moe_dispatch(top_e, rows, sub_rows):
    t = top_e.shape[0]
    n_assign = t * TOP_K
    n_blocks = -(-n_assign // rows) + N_EXPERTS
    e_flat = top_e.reshape(-1)
    onehot = (e_flat[:, None] == jnp.arange(N_EXPERTS, dtype=jnp.int32)[None, :]).astype(jnp.int32)
    csum = jnp.cumsum(onehot, axis=0)
    rank = jnp.sum((csum - onehot) * onehot, axis=1)
    counts = csum[-1]
    padded = (counts + rows - 1) // rows * rows
    pad_end = jnp.cumsum(padded)
    pad_start = pad_end - padded
    dest = jnp.sum(onehot * pad_start[None, :], axis=1) + rank
    tok = jnp.arange(n_assign, dtype=jnp.int32) // TOP_K
    spread = jnp.arange(n_blocks * rows, dtype=jnp.int32) % t
    row_tok = spread.at[dest].set(tok)
    n_active = pad_end[-1] // rows
    blk = jnp.arange(n_blocks, dtype=jnp.int32)
    blk_start = jnp.minimum(blk, n_active - 1) * rows
    block_expert = jnp.minimum(jnp.searchsorted(pad_end, blk_start, side='right'),
                               N_EXPERTS - 1).astype(jnp.int32)
    real_rows = jnp.clip(pad_start[block_expert] + counts[block_expert] - blk_start, 0, rows)
    n_sub = jnp.where(blk < n_active, (real_rows + sub_rows - 1) // sub_rows, 0).astype(jnp.int32)
    return row_tok, dest.reshape(t, TOP_K), block_expert, n_sub


def _expert_kernel(be_ref, ns_ref, x_ref, wg_ref, wu_ref, wd_ref, o_ref, acc_ref, *, sub_rows):
    i = pl.program_id(0)
    f = pl.program_id(1)
    rows = x_ref.shape[0]

    @pl.when(f == 0)
    def _():
        acc_ref[...] = jnp.zeros_like(acc_ref)

    def run(n_rows):
        x = x_ref[0:n_rows, :]
        act = (_silu(_dot(x, wg_ref[0].astype(BF16))) * _dot(x, wu_ref[0].astype(BF16))).astype(BF16)
        acc_ref[0:n_rows, :] += _dot(act, wd_ref[0].astype(BF16))

    for s in range(1, rows // sub_rows + 1):
        pl.when(ns_ref[i] == s)(functools.partial(run, s * sub_rows))

    @pl.when(f == pl.num_programs(1) - 1)
    def _():
        o_ref[...] = acc_ref[...].astype(BF16)


def moe_experts(xg, block_expert, n_sub, w_gate, w_up, w_down, layer, rows, sub_rows, tf=256):
    r_tot, d = xg.shape
    f_dim = w_gate.shape[2]
    tf = min(tf, f_dim)
    nf = f_dim // tf
    n_blocks = r_tot // rows
    e0 = layer * N_EXPERTS

    def f_idx(i, f, ns):
        return jnp.where(ns[i] > 0, f, nf - 1)

    grid_spec = pltpu.PrefetchScalarGridSpec(
        num_scalar_prefetch=2,
        grid=(n_blocks, nf),
        in_specs=[pl.BlockSpec((rows, d), lambda i, f, be, ns: (i, 0)),
                  pl.BlockSpec((1, d, tf), lambda i, f, be, ns: (e0 + be[i], 0, f_idx(i, f, ns))),
                  pl.BlockSpec((1, d, tf), lambda i, f, be, ns: (e0 + be[i], 0, f_idx(i, f, ns))),
                  pl.BlockSpec((1, tf, d), lambda i, f, be, ns: (e0 + be[i], f_idx(i, f, ns), 0))],
        out_specs=pl.BlockSpec((rows, d), lambda i, f, be, ns: (i, 0)),
        scratch_shapes=[pltpu.VMEM((rows, d), F32)],
    )
    return pl.pallas_call(
        functools.partial(_expert_kernel, sub_rows=sub_rows),
        grid_spec=grid_spec,
        out_shape=jax.ShapeDtypeStruct((r_tot, d), BF16),
        compiler_params=_params("arbitrary", "arbitrary"),
        name="moe_experts",
    )(block_expert, n_sub, xg, w_gate, w_up, w_down)


def _combine_kernel(x_ref, y1_ref, y2_ref, r_ref, mod_ref, gf_ref, o_ref, *, final_norm):
    r = r_ref[...]
    ff = r[:, 2:3] * y1_ref[...].astype(F32) + r[:, 3:4] * y2_ref[...].astype(F32)
    x_new = x_ref[...] + mod_ref[0][5:6] * ff
    if final_norm:
        ms = jnp.mean(x_new * x_new, axis=-1, keepdims=True)
        x_new = x_new * lax.rsqrt(ms + EPS) * gf_ref[...]
    o_ref[...] = x_new


def moe_combine(x, y1, y2, route, mod, g_final, rows_per_batch, final_norm, tm=512):
    t, d = x.shape
    tm = _tile(rows_per_batch, tm)
    tpb = rows_per_batch // tm
    row = lambda w: pl.BlockSpec((tm, w), lambda i: (i, 0))
    return pl.pallas_call(
        functools.partial(_combine_kernel, final_norm=final_norm),
        grid=(t // tm,),
        in_specs=[row(d), row(d), row(d), row(LANES),
                  pl.BlockSpec((1, N_MOD, d), lambda i: (i // tpb, 0, 0)),
                  pl.BlockSpec((1, d), lambda i: (0, 0))],
        out_specs=row(d),
        out_shape=jax.ShapeDtypeStruct((t, d), F32),
        compiler_params=_params("parallel"),
        name="moe_combine",
    )(x, y1, y2, route, mod, g_final.reshape(1, d))


def kernel(x, c, w_ada, b_ada, g_mix, g_ffn, g_final, w_in, conv_w, b_igate, b_fgate, mh_gain, w_out,
           ffn_w_gate, ffn_w_up, ffn_w_down, s5_lam_re, s5_lam_im, s5_log_step, s5_b_re, s5_b_im,
           s5_c_re, s5_c_im, s5_d, glu_w_a, glu_w_b, router_w, router_b, exp_w_gate, exp_w_up, exp_w_down):
    batch, seq, d = x.shape
    depth = w_ada.shape[0]
    t = batch * seq
    m_width = mh_gain.shape[1]
    heads_b = (d - m_width) // MOBA_HEAD_DIM
    n_gate = 2 * MLSTM_HEADS

    mods = ada_mod(c, w_ada, b_ada)
    xs = x.reshape(t, d)
    for layer in range(depth):
        i = layer // 2
        mod = mods[layer]
        if layer % 2 == 0:
            w = w_in[i]
            g0 = 4 * m_width
            w_big = jnp.concatenate([w[:, :g0], w[:, g0 + n_gate:]], axis=1).astype(BF16)
            w_gates = _split_cols(jnp.pad(w[:, g0:g0 + n_gate], ((0, 0), (0, LANES - n_gate))))
            proj, gates = in_proj(xs, g_mix[layer], mod, w_big, w_gates, seq)
            hm = mlstm_mix(proj, gates, conv_w[i], b_igate[i], b_fgate[i], mh_gain[i], batch, seq)
            hb = moba_mix(proj, g0 // MOBA_HEAD_DIM, batch, seq, heads_b)
            xs = out_proj(hm, hb, w_out[i].astype(BF16), xs, mod, seq)
            f_pad = -ffn_w_gate.shape[2] % 512
            wg = jnp.pad(ffn_w_gate[i], ((0, 0), (0, f_pad))).astype(BF16)
            wu = jnp.pad(ffn_w_up[i], ((0, 0), (0, f_pad))).astype(BF16)
            wd = jnp.pad(ffn_w_down[i], ((0, f_pad), (0, 0))).astype(BF16)
            xs = ffn_swiglu(xs, g_ffn[layer], mod, wg, wu, wd, seq)
        else:
            u = norm_mod(xs, g_mix[layer], mod, seq)
            ops = s5_operators(s5_lam_re[i], s5_lam_im[i], s5_log_step[i], s5_b_re[i], s5_b_im[i],
                               s5_c_re[i], s5_c_im[i], seq // S5_SUB)
            g = s5_scan_gelu(u, ops, s5_d[i], batch, seq)
            xs = glu_out(g, glu_w_a[i].astype(BF16), glu_w_b[i].astype(BF16), xs, mod, seq)
            h, route = moe_router(xs, g_ffn[layer], mod, router_w[i], router_b[i], seq)
            top_e = route[:, 0:TOP_K].astype(jnp.int32)
            row_tok, pos, block_expert, n_sub = moe_dispatch(top_e, MOE_ROWS, MOE_SUB_ROWS)
            stack = lambda w_: w_.reshape((-1,) + w_.shape[2:])
            y_rows = moe_experts(h[row_tok], block_expert, n_sub, stack(exp_w_gate), stack(exp_w_up),
                                 stack(exp_w_down), i, MOE_ROWS, MOE_SUB_ROWS)
            xs = moe_combine(xs, y_rows[pos[:, 0]], y_rows[pos[:, 1]], route, mod, g_final, seq,
                             final_norm=(layer == depth - 1))
    if depth % 2 == 1:
        raise NotImplementedError("final norm is fused into the last (odd) layer")
    return xs.reshape(batch, seq, d)
```

```python
import functools
import math

import jax
import jax.numpy as jnp
from jax import lax
from jax.experimental import pallas as pl
from jax.experimental.pallas import tpu as pltpu

F32 = jnp.float32
BF16 = jnp.bfloat16
HI = lax.Precision.HIGHEST
NEG_INF = float("-inf")

EPS = 1e-6
N_MOD = 6
MLSTM_HEADS = 4
MLSTM_CHUNK = 128
MLSTM_CONV = 4
MOBA_HEAD_DIM = 128
MOBA_BLOCK = 256
MOBA_TOPK = 3
S5_GROUP = 16
S5_STATE = 64
S5_SUB = 16
S5_TILE_GROUPS = 8
N_EXPERTS = 8
TOP_K = 2
MOE_ROWS = 1024
MOE_SUB_ROWS = 256
LANES = 128


def _tile(n, pref, align=8):
    for cand in range(min(pref, n), 0, -1):
        if n % cand == 0 and cand % align == 0:
            return cand
    raise ValueError(f"no {align}-aligned tile divides {n}")


def _params(*sem):
    return pltpu.CompilerParams(dimension_semantics=sem)


def _dot(a, b):
    return jnp.dot(a, b, preferred_element_type=F32)


def _dot_nt(a, b, precision=None):
    return lax.dot_general(a, b, (((1,), (1,)), ((), ())), precision=precision,
                           preferred_element_type=F32)


def _silu(x):
    return x * jax.nn.sigmoid(x)


def _split_cols(w):
    hi = w.astype(BF16)
    lo = (w - hi.astype(F32)).astype(BF16)
    return jnp.concatenate([hi, lo], axis=1)


def _dot_split(x, w_cat):
    hi = x.astype(BF16)
    lo = (x - hi.astype(F32)).astype(BF16)
    r = _dot(hi, w_cat) + _dot(lo, w_cat)
    n = w_cat.shape[1] // 2
    return r[:, :n] + r[:, n:]


def _norm_mod(x, gain, shift, scale):
    ms = jnp.mean(x * x, axis=-1, keepdims=True)
    y = x * lax.rsqrt(ms + EPS) * gain
    return y * (1.0 + scale) + shift


def _ada_kernel(c_ref, w_ref, b_ref, o_ref):
    cond = _silu(c_ref[...])
    bp = cond.shape[0]
    c_hi = cond.astype(BF16).astype(F32)
    lhs = jnp.concatenate([c_hi, cond - c_hi], axis=0).astype(BF16)
    w = w_ref[0]
    w_hi = w.astype(BF16)
    w_lo = (w - w_hi.astype(F32)).astype(BF16)
    r = _dot(lhs, w_hi) + _dot(lhs, w_lo)
    o_ref[0] = r[0:bp] + r[bp:2 * bp] + b_ref[0]


def ada_mod(c, w_ada, b_ada, tn=2048):
    depth, d, n = w_ada.shape
    b = c.shape[0]
    bp = 8
    cp = jnp.pad(c, ((0, bp - b), (0, 0)))
    out = pl.pallas_call(
        _ada_kernel,
        grid=(depth, n // tn),
        in_specs=[pl.BlockSpec((bp, d), lambda l, j: (0, 0)),
                  pl.BlockSpec((1, d, tn), lambda l, j: (l, 0, j)),
                  pl.BlockSpec((1, 1, tn), lambda l, j: (l, 0, j))],
        out_specs=pl.BlockSpec((1, bp, tn), lambda l, j: (l, 0, j)),
        out_shape=jax.ShapeDtypeStruct((depth, bp, n), F32),
        compiler_params=_params("parallel", "parallel"),
        name="ada_mod",
    )(cp, w_ada, b_ada.reshape(depth, 1, n))
    return out[:, :b].reshape(depth, b, N_MOD, d)


def _inproj_kernel(x_ref, g_ref, mod_ref, w_ref, wg_ref, o_ref, og_ref, h_ref):
    @pl.when(pl.program_id(1) == 0)
    def _():
        m = mod_ref[0]
        h = _norm_mod(x_ref[...], g_ref[...], m[0:1], m[1:2])
        h_ref[...] = h.astype(BF16)
        og_ref[...] = _dot_split(h, wg_ref[...])

    o_ref[...] = _dot(h_ref[...], w_ref[...]).astype(BF16)


def in_proj(x, gain, mod, w_big, w_gates, rows_per_batch, tm=1024, tn=1024):
    t, d = x.shape
    n = w_big.shape[1]
    tm = _tile(rows_per_batch, tm)
    tn = min(tn, n)
    tpb = rows_per_batch // tm
    return pl.pallas_call(
        _inproj_kernel,
        grid=(t // tm, n // tn),
        in_specs=[pl.BlockSpec((tm, d), lambda i, j: (i, 0)),
                  pl.BlockSpec((1, d), lambda i, j: (0, 0)),
                  pl.BlockSpec((1, N_MOD, d), lambda i, j: (i // tpb, 0, 0)),
                  pl.BlockSpec((d, tn), lambda i, j: (0, j)),
                  pl.BlockSpec((d, 2 * LANES), lambda i, j: (0, 0))],
        out_specs=[pl.BlockSpec((tm, tn), lambda i, j: (i, j)),
                   pl.BlockSpec((tm, LANES), lambda i, j: (i, 0))],
        out_shape=[jax.ShapeDtypeStruct((t, n), BF16), jax.ShapeDtypeStruct((t, LANES), F32)],
        scratch_shapes=[pltpu.VMEM((tm, d), BF16)],
        compiler_params=_params("parallel", "arbitrary"),
        name="in_proj",
    )(x, gain.reshape(1, d), mod, w_big, w_gates)


def _mlstm_kernel(bias_ref, q_ref, k_ref, v_ref, o_ref, gi_ref, gf_ref, cwq_ref, cwk_ref, gain_ref,
                  out_ref, qbuf, kbuf, c_st, n_st, m_st):
    chunk = pl.program_id(1)
    L = q_ref.shape[0]
    heads, dh = c_st.shape[0], c_st.shape[1]
    taps = cwq_ref.shape[0]
    halo = 8

    @pl.when(chunk == 0)
    def _():
        qbuf[0:halo] = jnp.zeros((halo, qbuf.shape[1]), F32)
        kbuf[0:halo] = jnp.zeros((halo, kbuf.shape[1]), F32)
        c_st[...] = jnp.zeros_like(c_st)
        n_st[...] = jnp.zeros_like(n_st)
        m_st[...] = jnp.full(m_st.shape, -1e30, F32)

    def conv_silu(src_ref, buf, w_ref):
        buf[halo:halo + L] = src_ref[...].astype(F32)
        w = w_ref[...]
        acc = buf[halo:halo + L] * w[taps - 1:taps]
        for d in range(1, taps):
            acc = acc + buf[pl.ds(halo - d, L), :] * w[taps - 1 - d:taps - d]
        buf[0:halo] = buf[L:L + halo]
        return _silu(acc)

    q_all = conv_silu(q_ref, qbuf, cwq_ref)
    k_all = conv_silu(k_ref, kbuf, cwk_ref) * (dh ** -0.5)

    row = lax.broadcasted_iota(jnp.int32, (L, L), 0)
    col = lax.broadcasted_iota(jnp.int32, (L, L), 1)
    eye = row == col

    def to_col(x_row):
        return jnp.sum(jnp.where(eye, jnp.broadcast_to(x_row, (L, L)), 0.0), axis=1, keepdims=True)

    fz = jnp.concatenate([gf_ref[0, h, 0] + bias_ref[1, h] for h in range(heads)]
                         + [jnp.zeros((8 - heads, L), F32)], axis=0)
    lf_rows = jnp.minimum(fz, 0.0) - jnp.log(1.0 + jnp.exp(-jnp.abs(fz)))
    g_rows = jnp.dot(lf_rows, (row <= col).astype(F32), precision=HI,
                     preferred_element_type=F32)

    for h in range(heads):
        sl = slice(h * dh, (h + 1) * dh)
        q, k, vb = q_all[:, sl], k_all[:, sl], v_ref[:, sl]
        ig_row = gi_ref[0, h, 0] + bias_ref[0, h]
        g_row = g_rows[h:h + 1]
        g_col = to_col(g_row)
        b_row = ig_row - g_row
        d_mat = jnp.where(col <= row, g_col + b_row, NEG_INF)
        m_prev = m_st[h]
        m_inter = g_col + m_prev
        m_t = jnp.maximum(m_inter, jnp.max(d_mat, axis=1, keepdims=True))
        qb = q.astype(BF16)
        kb = k.astype(BF16)
        s = _dot_nt(qb, kb) * jnp.exp(d_mat - m_t)
        decay = jnp.exp(m_inter - m_t)
        num = _dot(s.astype(BF16), vb) + decay * _dot(qb, c_st[h].astype(BF16))
        den = jnp.sum(s, axis=1, keepdims=True) + decay * jnp.sum(q * n_st[h], axis=1, keepdims=True)
        hh = num / jnp.maximum(jnp.abs(den), jnp.exp(-m_t))

        g_last = g_row[:, L - 1:L]
        a_row = g_last + b_row
        m_new = jnp.maximum(g_last + m_prev, jnp.max(a_row, axis=1, keepdims=True))
        w_col = to_col(jnp.exp(a_row - m_new))
        carry = jnp.exp(g_last + m_prev - m_new)
        kw = k * w_col
        c_st[h] = carry * c_st[h] + _dot(kw.T.astype(BF16), vb)
        n_st[h] = carry * n_st[h] + jnp.sum(kw, axis=0, keepdims=True)
        m_st[h] = m_new

        hn = hh * lax.rsqrt(jnp.mean(hh * hh, axis=1, keepdims=True) + EPS) * gain_ref[:, sl]
        out_ref[:, sl] = (hn * jax.nn.sigmoid(o_ref[:, sl].astype(F32))).astype(BF16)


def mlstm_mix(proj, gates, conv_w, b_igate, b_fgate, mh_gain, batch, seq):
    heads, L = MLSTM_HEADS, MLSTM_CHUNK
    width = mh_gain.shape[0]
    dh = width // heads
    nc = seq // L
    t = batch * seq

    def rows(a):
        return a.reshape(batch, nc, L, heads).transpose(0, 3, 1, 2).reshape(batch, heads, nc, 1, L)

    gi = rows(gates[:, 0:heads])
    gf = rows(gates[:, heads:2 * heads])
    bias = jnp.stack([b_igate, b_fgate]).astype(F32)
    blk = lambda off: pl.BlockSpec((L, width), lambda b, c: (b * nc + c, off))
    gspec = pl.BlockSpec((1, heads, 1, 1, L), lambda b, c: (b, 0, c, 0, 0))
    return pl.pallas_call(
        _mlstm_kernel,
        grid=(batch, nc),
        in_specs=[pl.BlockSpec(memory_space=pltpu.SMEM),
                  blk(0), blk(1), blk(2), blk(3), gspec, gspec,
                  pl.BlockSpec((MLSTM_CONV, width), lambda b, c: (0, 0)),
                  pl.BlockSpec((MLSTM_CONV, width), lambda b, c: (0, 1)),
                  pl.BlockSpec((1, width), lambda b, c: (0, 0))],
        out_specs=pl.BlockSpec((L, width), lambda b, c: (b * nc + c, 0)),
        out_shape=jax.ShapeDtypeStruct((t, width), BF16),
        scratch_shapes=[pltpu.VMEM((L + 8, width), F32), pltpu.VMEM((L + 8, width), F32),
                        pltpu.VMEM((heads, dh, dh), F32), pltpu.VMEM((heads, 1, dh), F32),
                        pltpu.VMEM((heads, 1, 1), F32)],
        compiler_params=_params("parallel", "arbitrary"),
        name="mlstm",
    )(bias, proj, proj, proj, proj, gi, gf, conv_w, conv_w, mh_gain.reshape(1, width))


MOBA_MASK_BIAS = -1e9
MOBA_WIDTH_STEP = 4
MOBA_CHUNK_BLOCKS = 2
MOBA_HEAD_GROUP = 2


def _moba_kernel(q_ref, k_ref, v_ref, o_ref, kmean_ref, kaug_ref, qaug_ref, s_ref):
    j = pl.program_id(1)
    blk = o_ref.shape[0]
    n_h, nbp, dh = kmean_ref.shape
    seq = k_ref.shape[0]
    nb = seq // blk
    exp_scale = dh ** -0.5 * math.log2(math.e)

    @pl.when(j == 0)
    def _():
        key_blk = lax.broadcasted_iota(jnp.int32, (seq, LANES), 0) // blk
        lane = lax.broadcasted_iota(jnp.int32, (seq, LANES), 1)
        block_onehot = jnp.where(key_blk == lane, 1.0, 0.0).astype(BF16)
        kmean_ref[...] = jnp.zeros_like(kmean_ref)
        for h in range(n_h):
            hs = slice(h * dh, (h + 1) * dh)
            for b in range(nb):
                kmean_ref[h, b:b + 1, :] = jnp.mean(k_ref[b * blk:(b + 1) * blk, hs].astype(F32), axis=0,
                                                    keepdims=True)
            kaug_ref[h, :, 0:dh] = k_ref[:, hs]
            kaug_ref[h, :, dh:dh + LANES] = block_onehot

            q_all = q_ref[:, hs]
            qaug_ref[h, :, 0:dh] = q_all
            gate_t = _dot_nt(kmean_ref[h], q_all.astype(F32), precision=HI)
            blk_id = lax.broadcasted_iota(jnp.int32, gate_t.shape, 0)
            q_blk = lax.broadcasted_iota(jnp.int32, gate_t.shape, 1) // blk
            valid = blk_id < q_blk
            sc = jnp.where(valid, gate_t, NEG_INF)
            beaten = jnp.zeros(gate_t.shape, F32)
            for b2 in range(nb):
                other = sc[b2:b2 + 1, :]
                wins = (other > sc) | ((other == sc) & (b2 < blk_id))
                beaten = beaten + wins.astype(F32)
            chosen = valid & (beaten < MOBA_TOPK)
            bias_t = jnp.where(chosen, 0.0, MOBA_MASK_BIAS)
            pad = jnp.zeros((LANES - nbp, blk), F32)
            for b in range(nb):
                piece = jnp.concatenate([bias_t[:, b * blk:(b + 1) * blk], pad], axis=0).T
                qaug_ref[h, b * blk:(b + 1) * blk, dh:dh + LANES] = piece.astype(BF16)

    start = pl.multiple_of(j * blk, blk)
    row = lax.broadcasted_iota(jnp.int32, (blk, blk), 0)
    col = lax.broadcasted_iota(jnp.int32, (blk, blk), 1)

    own = []
    for h in range(n_h):
        hs = slice(h * dh, (h + 1) * dh)
        q_aug = qaug_ref[h, pl.ds(start, blk), :]
        s_own = jnp.where(col <= row, _dot_nt(q_aug[:, 0:dh], k_ref[pl.ds(start, blk), hs]), NEG_INF)
        own.append((q_aug, s_own, jnp.max(s_own, axis=1, keepdims=True), v_ref[pl.ds(start, blk), hs]))

    @pl.when(j == 0)
    def _():
        for h, (_, s_own, m_own, v_own) in enumerate(own):
            p = jnp.exp2((s_own - m_own) * exp_scale)
            out = _dot(p.astype(BF16), v_own) / jnp.sum(p, axis=1, keepdims=True)
            o_ref[:, h * dh:(h + 1) * dh] = out.astype(BF16)

    def attend(n_blocks):
        w = n_blocks * blk
        step = MOBA_CHUNK_BLOCKS * blk
        chunks = [(lo_, min(lo_ + step, w)) for lo_ in range(0, w, step)]
        maxes = []
        for h, (q_aug, _, m_own, _) in enumerate(own):
            mx = jnp.full((blk, LANES), NEG_INF, F32)
            for lo_, hi_ in chunks:
                s_c = _dot_nt(q_aug, kaug_ref[h, lo_:hi_, :])
                s_ref[h, :, lo_:hi_] = s_c
                for t_ in range((hi_ - lo_) // LANES):
                    mx = jnp.maximum(mx, s_c[:, t_ * LANES:(t_ + 1) * LANES])
            maxes.append(jnp.maximum(jnp.max(mx, axis=1, keepdims=True), m_own))
        for h, (_, s_own, _, v_own) in enumerate(own):
            hs = slice(h * dh, (h + 1) * dh)
            m = maxes[h]
            p_own = jnp.exp2((s_own - m) * exp_scale)
            acc = _dot(p_own.astype(BF16), v_own)
            lsum = p_own[:, 0:LANES]
            for t_ in range(1, blk // LANES):
                lsum = lsum + p_own[:, t_ * LANES:(t_ + 1) * LANES]
            for lo_, hi_ in chunks:
                p = jnp.exp2((s_ref[h, :, lo_:hi_] - m) * exp_scale)
                for t_ in range((hi_ - lo_) // LANES):
                    lsum = lsum + p[:, t_ * LANES:(t_ + 1) * LANES]
                acc = acc + _dot(p.astype(BF16), v_ref[lo_:hi_, hs])
            o_ref[:, hs] = (acc / jnp.sum(lsum, axis=1, keepdims=True)).astype(BF16)

    lo = 0
    for hi in list(range(MOBA_WIDTH_STEP, nb - 1, MOBA_WIDTH_STEP)) + [nb - 1]:
        pl.when((j > lo) & (j <= hi))(functools.partial(attend, hi))
        lo = hi


def moba_mix(proj, col0, batch, seq, heads):
    dh, blk, grp = MOBA_HEAD_DIM, MOBA_BLOCK, MOBA_HEAD_GROUP
    nb = seq // blk
    nbp = -(-nb // 8) * 8
    t = batch * seq
    n_grp = heads // grp
    c0 = col0 // grp
    wide = lambda off: pl.BlockSpec((seq, grp * dh), lambda g, j: (g // n_grp, c0 + off * n_grp + g % n_grp))
    return pl.pallas_call(
        _moba_kernel,
        grid=(batch * n_grp, nb),
        in_specs=[wide(0), wide(1), wide(2)],
        out_specs=pl.BlockSpec((blk, grp * dh), lambda g, j: ((g // n_grp) * nb + j, g % n_grp)),
        out_shape=jax.ShapeDtypeStruct((t, heads * dh), BF16),
        scratch_shapes=[pltpu.VMEM((grp, nbp, dh), F32), pltpu.VMEM((grp, seq, dh + LANES), BF16),
                        pltpu.VMEM((grp, seq, dh + LANES), BF16), pltpu.VMEM((grp, blk, seq), F32)],
        compiler_params=_params("parallel", "arbitrary"),
        name="moba",
    )(proj, proj, proj)


def _outproj_kernel(hm_ref, hb_ref, w1_ref, w2_ref, x_ref, mod_ref, o_ref):
    acc = _dot(hm_ref[...], w1_ref[...]) + _dot(hb_ref[...], w2_ref[...])
    o_ref[...] = x_ref[...] + mod_ref[0][2:3] * acc


def out_proj(hm, hb, w_out, x, mod, rows_per_batch, tm=512):
    t, d = x.shape
    k1, k2 = hm.shape[1], hb.shape[1]
    tm = _tile(rows_per_batch, tm)
    tpb = rows_per_batch // tm
    return pl.pallas_call(
        _outproj_kernel,
        grid=(t // tm,),
        in_specs=[pl.BlockSpec((tm, k1), lambda i: (i, 0)),
                  pl.BlockSpec((tm, k2), lambda i: (i, 0)),
                  pl.BlockSpec((k1, d), lambda i: (0, 0)),
                  pl.BlockSpec((k2, d), lambda i: (0, 0)),
                  pl.BlockSpec((tm, d), lambda i: (i, 0)),
                  pl.BlockSpec((1, N_MOD, d), lambda i: (i // tpb, 0, 0))],
        out_specs=pl.BlockSpec((tm, d), lambda i: (i, 0)),
        out_shape=jax.ShapeDtypeStruct((t, d), F32),
        compiler_params=_params("parallel"),
        name="out_proj",
    )(hm, hb, w_out[:k1], w_out[k1:], x, mod)


def _ffn_kernel(x_ref, g_ref, mod_ref, wg_ref, wu_ref, wd_ref, o_ref, h_ref, acc_ref):
    f = pl.program_id(1)

    @pl.when(f == 0)
    def _():
        m = mod_ref[0]
        h_ref[...] = _norm_mod(x_ref[...], g_ref[...], m[3:4], m[4:5]).astype(BF16)
        acc_ref[...] = jnp.zeros_like(acc_ref)

    h = h_ref[...]
    act = (_silu(_dot(h, wg_ref[...])) * _dot(h, wu_ref[...])).astype(BF16)
    acc_ref[...] += _dot(act, wd_ref[...])

    @pl.when(f == pl.num_programs(1) - 1)
    def _():
        o_ref[...] = x_ref[...] + mod_ref[0][5:6] * acc_ref[...]


def ffn_swiglu(x, gain, mod, w_gate, w_up, w_down, rows_per_batch, tm=512, tf=512):
    t, d = x.shape
    f_dim = w_gate.shape[1]
    tm = _tile(rows_per_batch, tm)
    tf = min(tf, f_dim)
    tpb = rows_per_batch // tm
    return pl.pallas_call(
        _ffn_kernel,
        grid=(t // tm, f_dim // tf),
        in_specs=[pl.BlockSpec((tm, d), lambda i, f: (i, 0)),
                  pl.BlockSpec((1, d), lambda i, f: (0, 0)),
                  pl.BlockSpec((1, N_MOD, d), lambda i, f: (i // tpb, 0, 0)),
                  pl.BlockSpec((d, tf), lambda i, f: (0, f)),
                  pl.BlockSpec((d, tf), lambda i, f: (0, f)),
                  pl.BlockSpec((tf, d), lambda i, f: (f, 0))],
        out_specs=pl.BlockSpec((tm, d), lambda i, f: (i, 0)),
        out_shape=jax.ShapeDtypeStruct((t, d), F32),
        scratch_shapes=[pltpu.VMEM((tm, d), BF16), pltpu.VMEM((tm, d), F32)],
        compiler_params=_params("parallel", "arbitrary"),
        name="ffn_swiglu",
    )(x, gain.reshape(1, d), mod, w_gate, w_up, w_down)


def _normmod_kernel(x_ref, g_ref, mod_ref, o_ref):
    m = mod_ref[0]
    y = _norm_mod(x_ref[...], g_ref[...], m[0:1], m[1:2])
    for c in range(o_ref.shape[0]):
        o_ref[c] = y[:, c * LANES:(c + 1) * LANES]


def norm_mod(x, gain, mod, rows_per_batch, tm=512):
    t, d = x.shape
    tm = _tile(rows_per_batch, tm)
    tpb = rows_per_batch // tm
    return pl.pallas_call(
        _normmod_kernel,
        grid=(t // tm,),
        in_specs=[pl.BlockSpec((tm, d), lambda i: (i, 0)),
                  pl.BlockSpec((1, d), lambda i: (0, 0)),
                  pl.BlockSpec((1, N_MOD, d), lambda i: (i // tpb, 0, 0))],
        out_specs=pl.BlockSpec((d // LANES, tm, LANES), lambda i: (0, i, 0)),
        out_shape=jax.ShapeDtypeStruct((d // LANES, t, LANES), F32),
        compiler_params=_params("parallel"),
        name="norm_mod",
    )(x, gain.reshape(1, d), mod)


def s5_operators(lam_re, lam_im, log_step, b_re, b_im, c_re, c_im, n_chunks):
    g_all, p = lam_re.shape
    n = b_re.shape[-1]
    sub, tg = S5_SUB, S5_TILE_GROUPS
    nt = g_all // tg
    lam = lax.complex(lam_re.astype(F32), lam_im.astype(F32))
    lam_dt = lam * jnp.exp(log_step.astype(F32))[:, None]
    lam_bar = jnp.exp(lam_dt)
    b_bar = ((lam_bar - 1.0) / lam)[:, :, None] * lax.complex(b_re.astype(F32), b_im.astype(F32))
    c_mat = lax.complex(c_re.astype(F32), c_im.astype(F32))
    par = ((jnp.arange(tg) % 2)[:, None] == jnp.arange(2)[None, :]).astype(F32)
    ones_n = jnp.ones((n,), F32)

    def lay_in(a):
        z = jnp.einsum('qgpn,gr->qgnrp', a.reshape(nt, tg, p, n), par)
        return z.reshape(nt, tg * n, 2 * p)

    def lay_out(a):
        z = jnp.einsum('qgmp,gr->qrpgm', a.reshape(nt, tg, n, p), par)
        return z.reshape(nt, 2 * p, tg * n)

    lam_g = lam_bar.reshape(nt, tg, p)
    lam_in = jnp.einsum('qgp,n,r->qgnrp', lam_g, ones_n.astype(lam_g.dtype),
                        jnp.ones((2,), lam_g.dtype)).reshape(nt, tg * n, 2 * p)
    lam_out = jnp.einsum('qgp,m,r->qrpgm', lam_g, ones_n.astype(lam_g.dtype),
                         jnp.ones((2,), lam_g.dtype)).reshape(nt, 2 * p, tg * n)
    base = jnp.stack([lay_in(b_bar.real), lay_in(b_bar.imag), lam_in.real, lam_in.imag,
                      lay_out(c_mat.real), lay_out(c_mat.imag), lam_out.real, lam_out.imag], axis=1)

    n_lvl = max(1, (n_chunks - 1).bit_length())
    lv = jnp.exp(lam_dt[None] * (sub * 2.0 ** jnp.arange(n_lvl, dtype=F32))[:, None, None])
    lv = lv.reshape(n_lvl, nt, tg * p)
    lam_lv = jnp.stack([lv.real, lv.imag], axis=2).transpose(1, 0, 2, 3)
    return base, lam_lv


def _gelu_tanh(x):
    return 0.5 * x * (1.0 + jnp.tanh(0.7978845608028654 * (x + 0.044715 * (x * x * x))))


def _s5_kernel(u_ref, base_ref, lam_ref, d_ref, o_ref, ucat, bcat, ccat, krev, tpair):
    sub = 2 * ccat.shape[0]
    cw = u_ref.shape[2]
    r = u_ref.shape[1] // sub
    sw = bcat.shape[1] // 2
    pair = base_ref.shape[3]

    @pl.when(pl.program_id(1) == 0)
    def _():
        rb = lax.broadcasted_iota(jnp.int32, (cw, sw), 0) // (2 * S5_GROUP)
        cb = lax.broadcasted_iota(jnp.int32, (cw, sw), 1) // pair
        in_mask = rb == cb
        rc = lax.broadcasted_iota(jnp.int32, (sw, cw), 0) // pair
        cc = lax.broadcasted_iota(jnp.int32, (sw, cw), 1) // (2 * S5_GROUP)
        out_mask = rc == cc
        reps = sw // pair

        def expand(x, axis, mask):
            return jnp.where(mask, jnp.concatenate([x] * reps, axis=axis), 0.0).astype(BF16)

        def cmul(ar, ai, br, bi):
            return ar * br - ai * bi, ar * bi + ai * br

        b_r, b_i, lb_r, lb_i = (base_ref[0, k] for k in range(4))
        for l in reversed(range(sub)):
            bcat[l * cw:(l + 1) * cw, 0:sw] = expand(b_r, 1, in_mask)
            bcat[l * cw:(l + 1) * cw, sw:2 * sw] = expand(b_i, 1, in_mask)
            b_r, b_i = cmul(b_r, b_i, lb_r, lb_i)
        b_now = bcat[(sub - 1) * cw:sub * cw, :]

        c_r, c_i, lc_r, lc_i = (base_ref[0, k] for k in range(4, 8))
        c_now = jnp.concatenate([expand(c_r, 0, out_mask), expand(-c_i, 0, out_mask)], axis=0)
        krev[(sub - 1) * cw:sub * cw, :] = _dot(b_now, c_now).astype(BF16)
        for l in range(sub):
            c_r, c_i = cmul(c_r, c_i, lc_r, lc_i)
            half = slice((l % 2) * cw, (l % 2 + 1) * cw)
            ccat[l // 2, 0:sw, half] = expand(c_r, 0, out_mask)
            ccat[l // 2, sw:2 * sw, half] = expand(-c_i, 0, out_mask)
            if l < sub - 1:
                krev[(sub - 2 - l) * cw:(sub - 1 - l) * cw, :] = _dot(b_now, ccat[l // 2, :, half]).astype(BF16)
        for p in range(sub // 2):
            off = p * (p + 1) * cw
            n0 = (2 * p + 1) * cw
            tpair[off:off + n0, 0:cw] = krev[(sub - 1 - 2 * p) * cw:sub * cw, :]
            tpair[off + n0:off + n0 + cw, 0:cw] = jnp.zeros((cw, cw), BF16)
            tpair[off:off + n0 + cw, cw:2 * cw] = krev[(sub - 2 - 2 * p) * cw:sub * cw, :]

    for l in range(sub):
        ucat[:, l * cw:(l + 1) * cw] = u_ref[0, pl.ds(l, r, stride=sub), :].astype(BF16)

    v = _dot(ucat[...], bcat[...])
    s_re, s_im = v[:, 0:sw], v[:, sw:2 * sw]
    rowi = lax.broadcasted_iota(jnp.int32, (r, sw), 0)
    shift, lvl = 1, 0
    while shift < r:
        lr = lam_ref[0, lvl, 0:1, :]
        li = lam_ref[0, lvl, 1:2, :]
        keep = rowi >= shift
        p_re = jnp.where(keep, pltpu.roll(s_re, shift, axis=0), 0.0)
        p_im = jnp.where(keep, pltpu.roll(s_im, shift, axis=0), 0.0)
        s_re, s_im = s_re + lr * p_re - li * p_im, s_im + lr * p_im + li * p_re
        shift, lvl = shift * 2, lvl + 1
    first = rowi >= 1
    x_re = jnp.where(first, pltpu.roll(s_re, 1, axis=0), 0.0)
    x_im = jnp.where(first, pltpu.roll(s_im, 1, axis=0), 0.0)
    xb = jnp.concatenate([x_re, x_im], axis=1).astype(BF16)

    for p in range(sub // 2):
        off = p * (p + 1) * cw
        n_in = (2 * p + 2) * cw
        y2 = _dot(ucat[:, 0:n_in], tpair[off:off + n_in, :]) + _dot(xb, ccat[p])
        for l in (2 * p, 2 * p + 1):
            y = y2[:, (l % 2) * cw:(l % 2 + 1) * cw]
            ul = u_ref[0, pl.ds(l, r, stride=sub), :]
            o_ref[0, pl.ds(l, r, stride=sub), :] = _gelu_tanh(y + d_ref[...] * ul)


def s5_scan_gelu(u, ops, d_skip, batch, seq):
    base, lam_lv = ops
    nt, n_base, cw, pair = base.shape
    t = u.shape[1]
    d = nt * cw
    sub = S5_SUB
    sw = S5_TILE_GROUPS * S5_STATE
    n_lvl = lam_lv.shape[1]
    return pl.pallas_call(
        _s5_kernel,
        grid=(nt, batch),
        in_specs=[pl.BlockSpec((1, seq, cw), lambda c, b: (c, b, 0)),
                  pl.BlockSpec((1, n_base, cw, pair), lambda c, b: (c, 0, 0, 0)),
                  pl.BlockSpec((1, n_lvl, 2, sw), lambda c, b: (c, 0, 0, 0)),
                  pl.BlockSpec((1, cw), lambda c, b: (0, c))],
        out_specs=pl.BlockSpec((1, seq, cw), lambda c, b: (c, b, 0)),
        out_shape=jax.ShapeDtypeStruct((nt, t, cw), F32),
        scratch_shapes=[pltpu.VMEM((seq // sub, sub * cw), BF16),
                        pltpu.VMEM((sub * cw, 2 * sw), BF16),
                        pltpu.VMEM((sub // 2, 2 * sw, 2 * cw), BF16),
                        pltpu.VMEM((sub * cw, cw), BF16),
                        pltpu.VMEM(((sub // 2) * (sub // 2 + 1) * cw, 2 * cw), BF16)],
        compiler_params=_params("parallel", "arbitrary"),
        name="s5_scan",
    )(u, base, lam_lv, d_skip.reshape(1, d))


def _glu_kernel(g_ref, wa_ref, wb_ref, x_ref, mod_ref, o_ref):
    g = jnp.concatenate([g_ref[c] for c in range(g_ref.shape[0])], axis=1).astype(BF16)
    mix = _dot(g, wa_ref[...]) * jax.nn.sigmoid(_dot(g, wb_ref[...]))
    o_ref[...] = x_ref[...] + mod_ref[0][2:3] * mix


def glu_out(g, w_a, w_b, x, mod, rows_per_batch, tm=512, tn=1024):
    t, d = x.shape
    tm = _tile(rows_per_batch, tm)
    tn = min(tn, d)
    tpb = rows_per_batch // tm
    return pl.pallas_call(
        _glu_kernel,
        grid=(d // tn, t // tm),
        in_specs=[pl.BlockSpec((g.shape[0], tm, g.shape[2]), lambda j, i: (0, i, 0)),
                  pl.BlockSpec((d, tn), lambda j, i: (0, j)),
                  pl.BlockSpec((d, tn), lambda j, i: (0, j)),
                  pl.BlockSpec((tm, tn), lambda j, i: (i, j)),
                  pl.BlockSpec((1, N_MOD, tn), lambda j, i: (i // tpb, 0, j))],
        out_specs=pl.BlockSpec((tm, tn), lambda j, i: (i, j)),
        out_shape=jax.ShapeDtypeStruct((t, d), F32),
        compiler_params=_params("parallel", "parallel"),
        name="glu_out",
    )(g, w_a, w_b, x, mod)


def _router_kernel(x_ref, g_ref, mod_ref, rw_ref, rb_ref, h_ref, r_ref):
    m = mod_ref[0]
    h = _norm_mod(x_ref[...], g_ref[...], m[3:4], m[4:5])
    h_ref[...] = h.astype(BF16)
    logits = _dot_split(h, rw_ref[...]) + rb_ref[...]
    lane = lax.broadcasted_iota(jnp.int32, logits.shape, 1)
    logits = jnp.where(lane < N_EXPERTS, logits, NEG_INF)
    m1 = jnp.max(logits, axis=1, keepdims=True)
    i1 = jnp.min(jnp.where(logits == m1, lane, LANES), axis=1, keepdims=True)
    rest = jnp.where(lane == i1, NEG_INF, logits)
    m2 = jnp.max(rest, axis=1, keepdims=True)
    i2 = jnp.min(jnp.where(rest == m2, lane, LANES), axis=1, keepdims=True)
    e2 = jnp.exp(m2 - m1)
    g1 = 1.0 / (1.0 + e2)
    g2 = e2 / (1.0 + e2)
    r_ref[...] = jnp.where(lane == 0, i1.astype(F32),
                           jnp.where(lane == 1, i2.astype(F32),
                                     jnp.where(lane == 2, g1, jnp.where(lane == 3, g2, 0.0))))


def moe_router(x, gain, mod, router_w, router_b, rows_per_batch, tm=512):
    t, d = x.shape
    e = router_w.shape[1]
    tm = _tile(rows_per_batch, tm)
    tpb = rows_per_batch // tm
    rw = _split_cols(jnp.pad(router_w.astype(F32), ((0, 0), (0, LANES - e))))
    rb = jnp.pad(router_b.astype(F32), (0, LANES - e)).reshape(1, LANES)
    return pl.pallas_call(
        _router_kernel,
        grid=(t // tm,),
        in_specs=[pl.BlockSpec((tm, d), lambda i: (i, 0)),
                  pl.BlockSpec((1, d), lambda i: (0, 0)),
                  pl.BlockSpec((1, N_MOD, d), lambda i: (i // tpb, 0, 0)),
                  pl.BlockSpec((d, 2 * LANES), lambda i: (0, 0)),
                  pl.BlockSpec((1, LANES), lambda i: (0, 0))],
        out_specs=[pl.BlockSpec((tm, d), lambda i: (i, 0)),
                   pl.BlockSpec((tm, LANES), lambda i: (i, 0))],
        out_shape=[jax.ShapeDtypeStruct((t, d), BF16), jax.ShapeDtypeStruct((t, LANES), F32)],
        compiler_params=_params("parallel"),
        name="moe_router",
    )(x, gain.reshape(1, d), mod, rw, rb)


def moe_dispatch(top_e, rows, sub_rows):
    t = top_e.shape[0]
    n_assign = t * TOP_K
    n_blocks = -(-n_assign // rows) + N_EXPERTS
    e_flat = top_e.reshape(-1)
    onehot = (e_flat[:, None] == jnp.arange(N_EXPERTS, dtype=jnp.int32)[None, :]).astype(jnp.int32)
    csum = jnp.cumsum(onehot, axis=0)
    rank = jnp.sum((csum - onehot) * onehot, axis=1)
    counts = csum[-1]
    padded = (counts + rows - 1) // rows * rows
    pad_end = jnp.cumsum(padded)
    pad_start = pad_end - padded
    dest = jnp.sum(onehot * pad_start[None, :], axis=1) + rank
    tok = jnp.arange(n_assign, dtype=jnp.int32) // TOP_K
    spread = jnp.arange(n_blocks * rows, dtype=jnp.int32) % t
    row_tok = spread.at[dest].set(tok)
    n_active = pad_end[-1] // rows
    blk = jnp.arange(n_blocks, dtype=jnp.int32)
    blk_start = jnp.minimum(blk, n_active - 1) * rows
    block_expert = jnp.minimum(jnp.sum((blk_start[:, None] >= pad_end[None, :]).astype(jnp.int32), axis=1),
                               N_EXPERTS - 1)
    real_rows = jnp.clip(pad_start[block_expert] + counts[block_expert] - blk_start, 0, rows)
    n_sub = jnp.where(blk < n_active, (real_rows + sub_rows - 1) // sub_rows, 0).astype(jnp.int32)
    return row_tok, dest.reshape(t, TOP_K), block_expert, n_sub


def _expert_kernel(be_ref, ns_ref, x_ref, wg_ref, wu_ref, wd_ref, o_ref, acc_ref, *, sub_rows):
    i = pl.program_id(0)
    f = pl.program_id(1)
    rows = x_ref.shape[0]

    @pl.when(f == 0)
    def _():
        acc_ref[...] = jnp.zeros_like(acc_ref)

    def run(n_rows):
        x = x_ref[0:n_rows, :]
        act = (_silu(_dot(x, wg_ref[0].astype(BF16))) * _dot(x, wu_ref[0].astype(BF16))).astype(BF16)
        acc_ref[0:n_rows, :] += _dot(act, wd_ref[0].astype(BF16))

    for s in range(1, rows // sub_rows + 1):
        pl.when(ns_ref[i] == s)(functools.partial(run, s * sub_rows))

    @pl.when(f == pl.num_programs(1) - 1)
    def _():
        o_ref[...] = acc_ref[...].astype(BF16)


def moe_experts(xg, block_expert, n_sub, w_gate, w_up, w_down, layer, rows, sub_rows, tf=512):
    r_tot, d = xg.shape
    f_dim = w_gate.shape[2]
    tf = min(tf, f_dim)
    nf = f_dim // tf
    n_blocks = r_tot // rows
    e0 = layer * N_EXPERTS

    def f_idx(i, f, ns):
        return jnp.where(ns[i] > 0, f, nf - 1)

    grid_spec = pltpu.PrefetchScalarGridSpec(
        num_scalar_prefetch=2,
        grid=(n_blocks, nf),
        in_specs=[pl.BlockSpec((rows, d), lambda i, f, be, ns: (i, 0), pipeline_mode=pl.Buffered(1)),
                  pl.BlockSpec((1, d, tf), lambda i, f, be, ns: (e0 + be[i], 0, f_idx(i, f, ns))),
                  pl.BlockSpec((1, d, tf), lambda i, f, be, ns: (e0 + be[i], 0, f_idx(i, f, ns))),
                  pl.BlockSpec((1, tf, d), lambda i, f, be, ns: (e0 + be[i], f_idx(i, f, ns), 0))],
        out_specs=pl.BlockSpec((rows, d), lambda i, f, be, ns: (i, 0)),
        scratch_shapes=[pltpu.VMEM((rows, d), F32)],
    )
    return pl.pallas_call(
        functools.partial(_expert_kernel, sub_rows=sub_rows),
        grid_spec=grid_spec,
        out_shape=jax.ShapeDtypeStruct((r_tot, d), BF16),
        compiler_params=_params("arbitrary", "arbitrary"),
        name="moe_experts",
    )(block_expert, n_sub, xg, w_gate, w_up, w_down)


def _combine_kernel(x_ref, y1_ref, y2_ref, r_ref, mod_ref, gf_ref, o_ref, *, final_norm):
    r = r_ref[...]
    ff = r[:, 2:3] * y1_ref[...].astype(F32) + r[:, 3:4] * y2_ref[...].astype(F32)
    x_new = x_ref[...] + mod_ref[0][5:6] * ff
    if final_norm:
        ms = jnp.mean(x_new * x_new, axis=-1, keepdims=True)
        x_new = x_new * lax.rsqrt(ms + EPS) * gf_ref[...]
    o_ref[...] = x_new


def moe_combine(x, y1, y2, route, mod, g_final, rows_per_batch, final_norm, tm=512):
    t, d = x.shape
    tm = _tile(rows_per_batch, tm)
    tpb = rows_per_batch // tm
    row = lambda w: pl.BlockSpec((tm, w), lambda i: (i, 0))
    return pl.pallas_call(
        functools.partial(_combine_kernel, final_norm=final_norm),
        grid=(t // tm,),
        in_specs=[row(d), row(d), row(d), row(LANES),
                  pl.BlockSpec((1, N_MOD, d), lambda i: (i // tpb, 0, 0)),
                  pl.BlockSpec((1, d), lambda i: (0, 0))],
        out_specs=row(d),
        out_shape=jax.ShapeDtypeStruct((t, d), F32),
        compiler_params=_params("parallel"),
        name="moe_combine",
    )(x, y1, y2, route, mod, g_final.reshape(1, d))


def kernel(x, c, w_ada, b_ada, g_mix, g_ffn, g_final, w_in, conv_w, b_igate, b_fgate, mh_gain, w_out,
           ffn_w_gate, ffn_w_up, ffn_w_down, s5_lam_re, s5_lam_im, s5_log_step, s5_b_re, s5_b_im,
           s5_c_re, s5_c_im, s5_d, glu_w_a, glu_w_b, router_w, router_b, exp_w_gate, exp_w_up, exp_w_down):
    batch, seq, d = x.shape
    depth = w_ada.shape[0]
    t = batch * seq
    m_width = mh_gain.shape[1]
    heads_b = (d - m_width) // MOBA_HEAD_DIM
    n_gate = 2 * MLSTM_HEADS

    mods = ada_mod(c, w_ada, b_ada)
    xs = x.reshape(t, d)
    for layer in range(depth):
        i = layer // 2
        mod = mods[layer]
        if layer % 2 == 0:
            w = w_in[i]
            g0 = 4 * m_width
            w_big = jnp.concatenate([w[:, :g0], w[:, g0 + n_gate:]], axis=1).astype(BF16)
            w_gates = _split_cols(jnp.pad(w[:, g0:g0 + n_gate], ((0, 0), (0, LANES - n_gate))))
            proj, gates = in_proj(xs, g_mix[layer], mod, w_big, w_gates, seq)
            hm = mlstm_mix(proj, gates, conv_w[i], b_igate[i], b_fgate[i], mh_gain[i], batch, seq)
            hb = moba_mix(proj, g0 // MOBA_HEAD_DIM, batch, seq, heads_b)
            xs = out_proj(hm, hb, w_out[i].astype(BF16), xs, mod, seq)
            f_pad = -ffn_w_gate.shape[2] % 512
            wg = jnp.pad(ffn_w_gate[i], ((0, 0), (0, f_pad))).astype(BF16)
            wu = jnp.pad(ffn_w_up[i], ((0, 0), (0, f_pad))).astype(BF16)
            wd = jnp.pad(ffn_w_down[i], ((0, f_pad), (0, 0))).astype(BF16)
            xs = ffn_swiglu(xs, g_ffn[layer], mod, wg, wu, wd, seq)
        else:
            u = norm_mod(xs, g_mix[layer], mod, seq)
            ops = s5_operators(s5_lam_re[i], s5_lam_im[i], s5_log_step[i], s5_b_re[i], s5_b_im[i],
                               s5_c_re[i], s5_c_im[i], seq // S5_SUB)
            g = s5_scan_gelu(u, ops, s5_d[i], batch, seq)
            xs = glu_out(g, glu_w_a[i].astype(BF16), glu_w_b[i].astype(BF16), xs, mod, seq)
            h, route = moe_router(xs, g_ffn[layer], mod, router_w[i], router_b[i], seq)
            top_e = route[:, 0:TOP_K].astype(jnp.int32)
            row_tok, pos, block_expert, n_sub = moe_dispatch(top_e, MOE_ROWS, MOE_SUB_ROWS)
            stack = lambda w_: w_.reshape((-1,) + w_.shape[2:])
            y_rows = moe_experts(h[row_tok], block_expert, n_sub, stack(exp_w_gate), stack(exp_w_up),
                                 stack(exp_w_down), i, MOE_ROWS, MOE_SUB_ROWS)
            xs = moe_combine(xs, y_rows[pos[:, 0]], y_rows[pos[:, 1]], route, mod, g_final, seq,
                             final_norm=(layer == depth - 1))
    if depth % 2 == 1:
        raise NotImplementedError("final norm is fused into the last (odd) layer")
    return xs.reshape(batch, seq, d)
```

```python
import functools
import math

import jax
import jax.numpy as jnp
from jax import lax
from jax.experimental import pallas as pl
from jax.experimental.pallas import tpu as pltpu

F32 = jnp.float32
BF16 = jnp.bfloat16
HI = lax.Precision.HIGHEST
NEG_INF = float("-inf")

EPS = 1e-6
N_MOD = 6
MLSTM_HEADS = 4
MLSTM_CHUNK = 128
MLSTM_CONV = 4
MOBA_HEAD_DIM = 128
MOBA_BLOCK = 256
MOBA_TOPK = 3
S5_GROUP = 16
S5_STATE = 64
S5_SUB = 16
S5_TILE_GROUPS = 8
N_EXPERTS = 8
TOP_K = 2
MOE_ROWS = 1024
MOE_SUB_ROWS = 256
LANES = 128


def _tile(n, pref, align=8):
    for cand in range(min(pref, n), 0, -1):
        if n % cand == 0 and cand % align == 0:
            return cand
    raise ValueError(f"no {align}-aligned tile divides {n}")


def _params(*sem):
    return pltpu.CompilerParams(dimension_semantics=sem)


def _dot(a, b):
    return jnp.dot(a, b, preferred_element_type=F32)


def _dot_nt(a, b, precision=None):
    return lax.dot_general(a, b, (((1,), (1,)), ((), ())), precision=precision,
                           preferred_element_type=F32)


def _silu(x):
    return x * jax.nn.sigmoid(x)


def _split_cols(w):
    hi = w.astype(BF16)
    lo = (w - hi.astype(F32)).astype(BF16)
    return jnp.concatenate([hi, lo], axis=1)


def _dot_split(x, w_cat):
    hi = x.astype(BF16)
    lo = (x - hi.astype(F32)).astype(BF16)
    r = _dot(hi, w_cat) + _dot(lo, w_cat)
    n = w_cat.shape[1] // 2
    return r[:, :n] + r[:, n:]


def _norm_mod(x, gain, shift, scale):
    ms = jnp.mean(x * x, axis=-1, keepdims=True)
    y = x * lax.rsqrt(ms + EPS) * gain
    return y * (1.0 + scale) + shift


def _ada_kernel(c_ref, w_ref, b_ref, o_ref):
    cond = _silu(c_ref[...])
    bp = cond.shape[0]
    c_hi = cond.astype(BF16).astype(F32)
    lhs = jnp.concatenate([c_hi, cond - c_hi], axis=0).astype(BF16)
    w = w_ref[0]
    w_hi = w.astype(BF16)
    w_lo = (w - w_hi.astype(F32)).astype(BF16)
    r = _dot(lhs, w_hi) + _dot(lhs, w_lo)
    o_ref[0] = r[0:bp] + r[bp:2 * bp] + b_ref[0]


def ada_mod(c, w_ada, b_ada, tn=2048):
    depth, d, n = w_ada.shape
    b = c.shape[0]
    bp = 8
    cp = jnp.pad(c, ((0, bp - b), (0, 0)))
    out = pl.pallas_call(
        _ada_kernel,
        grid=(depth, n // tn),
        in_specs=[pl.BlockSpec((bp, d), lambda l, j: (0, 0)),
                  pl.BlockSpec((1, d, tn), lambda l, j: (l, 0, j)),
                  pl.BlockSpec((1, 1, tn), lambda l, j: (l, 0, j))],
        out_specs=pl.BlockSpec((1, bp, tn), lambda l, j: (l, 0, j)),
        out_shape=jax.ShapeDtypeStruct((depth, bp, n), F32),
        compiler_params=_params("parallel", "parallel"),
        name="ada_mod",
    )(cp, w_ada, b_ada.reshape(depth, 1, n))
    return out[:, :b].reshape(depth, b, N_MOD, d)


def _inproj_kernel(x_ref, g_ref, mod_ref, w_ref, wg_ref, o_ref, og_ref, h_ref):
    @pl.when(pl.program_id(1) == 0)
    def _():
        m = mod_ref[0]
        h = _norm_mod(x_ref[...], g_ref[...], m[0:1], m[1:2])
        h_ref[...] = h.astype(BF16)
        og_ref[...] = _dot_split(h, wg_ref[...])

    o_ref[...] = _dot(h_ref[...], w_ref[...]).astype(BF16)


def in_proj(x, gain, mod, w_big, w_gates, rows_per_batch, tm=1024, tn=1024):
    t, d = x.shape
    n = w_big.shape[1]
    tm = _tile(rows_per_batch, tm)
    tn = min(tn, n)
    tpb = rows_per_batch // tm
    return pl.pallas_call(
        _inproj_kernel,
        grid=(t // tm, n // tn),
        in_specs=[pl.BlockSpec((tm, d), lambda i, j: (i, 0)),
                  pl.BlockSpec((1, d), lambda i, j: (0, 0)),
                  pl.BlockSpec((1, N_MOD, d), lambda i, j: (i // tpb, 0, 0)),
                  pl.BlockSpec((d, tn), lambda i, j: (0, j)),
                  pl.BlockSpec((d, 2 * LANES), lambda i, j: (0, 0))],
        out_specs=[pl.BlockSpec((tm, tn), lambda i, j: (i, j)),
                   pl.BlockSpec((tm, LANES), lambda i, j: (i, 0))],
        out_shape=[jax.ShapeDtypeStruct((t, n), BF16), jax.ShapeDtypeStruct((t, LANES), F32)],
        scratch_shapes=[pltpu.VMEM((tm, d), BF16)],
        compiler_params=_params("parallel", "arbitrary"),
        name="in_proj",
    )(x, gain.reshape(1, d), mod, w_big, w_gates)


def _mlstm_kernel(bias_ref, q_ref, k_ref, v_ref, o_ref, gi_ref, gf_ref, cwq_ref, cwk_ref, gain_ref,
                  out_ref, qbuf, kbuf, c_st, n_st, m_st):
    chunk = pl.program_id(1)
    L = q_ref.shape[0]
    heads, dh = c_st.shape[0], c_st.shape[1]
    taps = cwq_ref.shape[0]
    halo = 8

    @pl.when(chunk == 0)
    def _():
        qbuf[0:halo] = jnp.zeros((halo, qbuf.shape[1]), F32)
        kbuf[0:halo] = jnp.zeros((halo, kbuf.shape[1]), F32)
        c_st[...] = jnp.zeros_like(c_st)
        n_st[...] = jnp.zeros_like(n_st)
        m_st[...] = jnp.full(m_st.shape, -1e30, F32)

    def conv_silu(src_ref, buf, w_ref):
        buf[halo:halo + L] = src_ref[...].astype(F32)
        w = w_ref[...]
        acc = buf[halo:halo + L] * w[taps - 1:taps]
        for d in range(1, taps):
            acc = acc + buf[pl.ds(halo - d, L), :] * w[taps - 1 - d:taps - d]
        buf[0:halo] = buf[L:L + halo]
        return _silu(acc)

    q_all = conv_silu(q_ref, qbuf, cwq_ref)
    k_all = conv_silu(k_ref, kbuf, cwk_ref) * (dh ** -0.5)

    row = lax.broadcasted_iota(jnp.int32, (L, L), 0)
    col = lax.broadcasted_iota(jnp.int32, (L, L), 1)
    eye = row == col

    def to_col(x_row):
        return jnp.sum(jnp.where(eye, jnp.broadcast_to(x_row, (L, L)), 0.0), axis=1, keepdims=True)

    fz = jnp.concatenate([gf_ref[0, h, 0] + bias_ref[1, h] for h in range(heads)]
                         + [jnp.zeros((8 - heads, L), F32)], axis=0)
    lf_rows = jnp.minimum(fz, 0.0) - jnp.log(1.0 + jnp.exp(-jnp.abs(fz)))
    g_rows = jnp.dot(lf_rows, (row <= col).astype(F32), precision=HI,
                     preferred_element_type=F32)

    for h in range(heads):
        sl = slice(h * dh, (h + 1) * dh)
        q, k, vb = q_all[:, sl], k_all[:, sl], v_ref[:, sl]
        ig_row = gi_ref[0, h, 0] + bias_ref[0, h]
        g_row = g_rows[h:h + 1]
        g_col = to_col(g_row)
        b_row = ig_row - g_row
        d_mat = jnp.where(col <= row, g_col + b_row, NEG_INF)
        m_prev = m_st[h]
        m_inter = g_col + m_prev
        m_t = jnp.maximum(m_inter, jnp.max(d_mat, axis=1, keepdims=True))
        qb = q.astype(BF16)
        kb = k.astype(BF16)
        s = _dot_nt(qb, kb) * jnp.exp(d_mat - m_t)
        decay = jnp.exp(m_inter - m_t)
        num = _dot(s.astype(BF16), vb) + decay * _dot(qb, c_st[h].astype(BF16))
        den = jnp.sum(s, axis=1, keepdims=True) + decay * jnp.sum(q * n_st[h], axis=1, keepdims=True)
        hh = num / jnp.maximum(jnp.abs(den), jnp.exp(-m_t))

        g_last = g_row[:, L - 1:L]
        a_row = g_last + b_row
        m_new = jnp.maximum(g_last + m_prev, jnp.max(a_row, axis=1, keepdims=True))
        w_col = to_col(jnp.exp(a_row - m_new))
        carry = jnp.exp(g_last + m_prev - m_new)
        kw = k * w_col
        c_st[h] = carry * c_st[h] + _dot(kw.T.astype(BF16), vb)
        n_st[h] = carry * n_st[h] + jnp.sum(kw, axis=0, keepdims=True)
        m_st[h] = m_new

        hn = hh * lax.rsqrt(jnp.mean(hh * hh, axis=1, keepdims=True) + EPS) * gain_ref[:, sl]
        out_ref[:, sl] = (hn * jax.nn.sigmoid(o_ref[:, sl].astype(F32))).astype(BF16)


def mlstm_mix(proj, gates, conv_w, b_igate, b_fgate, mh_gain, batch, seq):
    heads, L = MLSTM_HEADS, MLSTM_CHUNK
    width = mh_gain.shape[0]
    dh = width // heads
    nc = seq // L
    t = batch * seq

    def rows(a):
        return a.reshape(batch, nc, L, heads).transpose(0, 3, 1, 2).reshape(batch, heads, nc, 1, L)

    gi = rows(gates[:, 0:heads])
    gf = rows(gates[:, heads:2 * heads])
    bias = jnp.stack([b_igate, b_fgate]).astype(F32)
    blk = lambda off: pl.BlockSpec((L, width), lambda b, c: (b * nc + c, off))
    gspec = pl.BlockSpec((1, heads, 1, 1, L), lambda b, c: (b, 0, c, 0, 0))
    return pl.pallas_call(
        _mlstm_kernel,
        grid=(batch, nc),
        in_specs=[pl.BlockSpec(memory_space=pltpu.SMEM),
                  blk(0), blk(1), blk(2), blk(3), gspec, gspec,
                  pl.BlockSpec((MLSTM_CONV, width), lambda b, c: (0, 0)),
                  pl.BlockSpec((MLSTM_CONV, width), lambda b, c: (0, 1)),
                  pl.BlockSpec((1, width), lambda b, c: (0, 0))],
        out_specs=pl.BlockSpec((L, width), lambda b, c: (b * nc + c, 0)),
        out_shape=jax.ShapeDtypeStruct((t, width), BF16),
        scratch_shapes=[pltpu.VMEM((L + 8, width), F32), pltpu.VMEM((L + 8, width), F32),
                        pltpu.VMEM((heads, dh, dh), F32), pltpu.VMEM((heads, 1, dh), F32),
                        pltpu.VMEM((heads, 1, 1), F32)],
        compiler_params=_params("parallel", "arbitrary"),
        name="mlstm",
    )(bias, proj, proj, proj, proj, gi, gf, conv_w, conv_w, mh_gain.reshape(1, width))


MOBA_MASK_BIAS = -1e9
MOBA_WIDTH_STEP = 4
MOBA_CHUNK_BLOCKS = 2
MOBA_HEAD_GROUP = 2


def _moba_kernel(q_ref, k_ref, v_ref, o_ref, kmean_ref, kaug_ref, qaug_ref, s_ref):
    j = pl.program_id(1)
    blk = o_ref.shape[0]
    n_h, nbp, dh = kmean_ref.shape
    seq = k_ref.shape[0]
    nb = seq // blk
    exp_scale = dh ** -0.5 * math.log2(math.e)

    @pl.when(j == 0)
    def _():
        key_blk = lax.broadcasted_iota(jnp.int32, (seq, LANES), 0) // blk
        lane = lax.broadcasted_iota(jnp.int32, (seq, LANES), 1)
        block_onehot = jnp.where(key_blk == lane, 1.0, 0.0).astype(BF16)
        kmean_ref[...] = jnp.zeros_like(kmean_ref)
        for h in range(n_h):
            hs = slice(h * dh, (h + 1) * dh)
            for b in range(nb):
                kmean_ref[h, b:b + 1, :] = jnp.mean(k_ref[b * blk:(b + 1) * blk, hs].astype(F32), axis=0,
                                                    keepdims=True)
            kaug_ref[h, :, 0:dh] = k_ref[:, hs]
            kaug_ref[h, :, dh:dh + LANES] = block_onehot

            q_all = q_ref[:, hs]
            qaug_ref[h, :, 0:dh] = q_all
            gate_t = _dot_nt(kmean_ref[h], q_all.astype(F32), precision=HI)
            blk_id = lax.broadcasted_iota(jnp.int32, gate_t.shape, 0)
            q_blk = lax.broadcasted_iota(jnp.int32, gate_t.shape, 1) // blk
            valid = blk_id < q_blk
            sc = jnp.where(valid, gate_t, NEG_INF)
            beaten = jnp.zeros(gate_t.shape, F32)
            for b2 in range(nb):
                other = sc[b2:b2 + 1, :]
                wins = (other > sc) | ((other == sc) & (b2 < blk_id))
                beaten = beaten + wins.astype(F32)
            chosen = valid & (beaten < MOBA_TOPK)
            bias_t = jnp.where(chosen, 0.0, MOBA_MASK_BIAS)
            pad = jnp.zeros((LANES - nbp, blk), F32)
            for b in range(nb):
                piece = jnp.concatenate([bias_t[:, b * blk:(b + 1) * blk], pad], axis=0).T
                qaug_ref[h, b * blk:(b + 1) * blk, dh:dh + LANES] = piece.astype(BF16)

    start = pl.multiple_of(j * blk, blk)
    row = lax.broadcasted_iota(jnp.int32, (blk, blk), 0)
    col = lax.broadcasted_iota(jnp.int32, (blk, blk), 1)

    own = []
    for h in range(n_h):
        hs = slice(h * dh, (h + 1) * dh)
        q_aug = qaug_ref[h, pl.ds(start, blk), :]
        s_own = jnp.where(col <= row, _dot_nt(q_aug[:, 0:dh], k_ref[pl.ds(start, blk), hs]), NEG_INF)
        own.append((q_aug, s_own, jnp.max(s_own, axis=1, keepdims=True), v_ref[pl.ds(start, blk), hs]))

    @pl.when(j == 0)
    def _():
        for h, (_, s_own, m_own, v_own) in enumerate(own):
            p = jnp.exp2((s_own - m_own) * exp_scale)
            out = _dot(p.astype(BF16), v_own) / jnp.sum(p, axis=1, keepdims=True)
            o_ref[:, h * dh:(h + 1) * dh] = out.astype(BF16)

    def attend(n_blocks):
        w = n_blocks * blk
        step = MOBA_CHUNK_BLOCKS * blk
        chunks = [(lo_, min(lo_ + step, w)) for lo_ in range(0, w, step)]
        maxes = []
        for h, (q_aug, _, m_own, _) in enumerate(own):
            mx = jnp.full((blk, LANES), NEG_INF, F32)
            for lo_, hi_ in chunks:
                s_c = _dot_nt(q_aug, kaug_ref[h, lo_:hi_, :])
                s_ref[h, :, lo_:hi_] = s_c
                for t_ in range((hi_ - lo_) // LANES):
                    mx = jnp.maximum(mx, s_c[:, t_ * LANES:(t_ + 1) * LANES])
            maxes.append(jnp.maximum(jnp.max(mx, axis=1, keepdims=True), m_own))
        for h, (_, s_own, _, v_own) in enumerate(own):
            hs = slice(h * dh, (h + 1) * dh)
            m = maxes[h]
            p_own = jnp.exp2((s_own - m) * exp_scale)
            acc = _dot(p_own.astype(BF16), v_own)
            lsum = p_own[:, 0:LANES]
            for t_ in range(1, blk // LANES):
                lsum = lsum + p_own[:, t_ * LANES:(t_ + 1) * LANES]
            for lo_, hi_ in chunks:
                p = jnp.exp2((s_ref[h, :, lo_:hi_] - m) * exp_scale)
                for t_ in range((hi_ - lo_) // LANES):
                    lsum = lsum + p[:, t_ * LANES:(t_ + 1) * LANES]
                acc = acc + _dot(p.astype(BF16), v_ref[lo_:hi_, hs])
            o_ref[:, hs] = (acc / jnp.sum(lsum, axis=1, keepdims=True)).astype(BF16)

    lo = 0
    for hi in list(range(MOBA_WIDTH_STEP, nb - 1, MOBA_WIDTH_STEP)) + [nb - 1]:
        pl.when((j > lo) & (j <= hi))(functools.partial(attend, hi))
        lo = hi


def moba_mix(proj, col0, batch, seq, heads):
    dh, blk, grp = MOBA_HEAD_DIM, MOBA_BLOCK, MOBA_HEAD_GROUP
    nb = seq // blk
    nbp = -(-nb // 8) * 8
    t = batch * seq
    n_grp = heads // grp
    c0 = col0 // grp
    wide = lambda off: pl.BlockSpec((seq, grp * dh), lambda g, j: (g // n_grp, c0 + off * n_grp + g % n_grp))
    return pl.pallas_call(
        _moba_kernel,
        grid=(batch * n_grp, nb),
        in_specs=[wide(0), wide(1), wide(2)],
        out_specs=pl.BlockSpec((blk, grp * dh), lambda g, j: ((g // n_grp) * nb + j, g % n_grp)),
        out_shape=jax.ShapeDtypeStruct((t, heads * dh), BF16),
        scratch_shapes=[pltpu.VMEM((grp, nbp, dh), F32), pltpu.VMEM((grp, seq, dh + LANES), BF16),
                        pltpu.VMEM((grp, seq, dh + LANES), BF16), pltpu.VMEM((grp, blk, seq), F32)],
        compiler_params=_params("parallel", "arbitrary"),
        name="moba",
    )(proj, proj, proj)


def _outproj_kernel(hm_ref, hb_ref, w1_ref, w2_ref, x_ref, mod_ref, o_ref):
    acc = _dot(hm_ref[...], w1_ref[...]) + _dot(hb_ref[...], w2_ref[...])
    o_ref[...] = x_ref[...] + mod_ref[0][2:3] * acc


def out_proj(hm, hb, w_out, x, mod, rows_per_batch, tm=512):
    t, d = x.shape
    k1, k2 = hm.shape[1], hb.shape[1]
    tm = _tile(rows_per_batch, tm)
    tpb = rows_per_batch // tm
    return pl.pallas_call(
        _outproj_kernel,
        grid=(t // tm,),
        in_specs=[pl.BlockSpec((tm, k1), lambda i: (i, 0)),
                  pl.BlockSpec((tm, k2), lambda i: (i, 0)),
                  pl.BlockSpec((k1, d), lambda i: (0, 0)),
                  pl.BlockSpec((k2, d), lambda i: (0, 0)),
                  pl.BlockSpec((tm, d), lambda i: (i, 0)),
                  pl.BlockSpec((1, N_MOD, d), lambda i: (i // tpb, 0, 0))],
        out_specs=pl.BlockSpec((tm, d), lambda i: (i, 0)),
        out_shape=jax.ShapeDtypeStruct((t, d), F32),
        compiler_params=_params("parallel"),
        name="out_proj",
    )(hm, hb, w_out[:k1], w_out[k1:], x, mod)


def _ffn_kernel(x_ref, g_ref, mod_ref, wg_ref, wu_ref, wd_ref, g2_ref, mod2_ref, o_ref, u_ref, h_ref, acc_ref):
    f = pl.program_id(1)

    @pl.when(f == 0)
    def _():
        m = mod_ref[0]
        h_ref[...] = _norm_mod(x_ref[...], g_ref[...], m[3:4], m[4:5]).astype(BF16)
        acc_ref[...] = jnp.zeros_like(acc_ref)

    h = h_ref[...]
    act = (_silu(_dot(h, wg_ref[...])) * _dot(h, wu_ref[...])).astype(BF16)
    acc_ref[...] += _dot(act, wd_ref[...])

    @pl.when(f == pl.num_programs(1) - 1)
    def _():
        x_new = x_ref[...] + mod_ref[0][5:6] * acc_ref[...]
        o_ref[...] = x_new
        m2 = mod2_ref[0]
        u = _norm_mod(x_new, g2_ref[...], m2[0:1], m2[1:2])
        for c in range(u_ref.shape[0]):
            u_ref[c] = u[:, c * LANES:(c + 1) * LANES]


def ffn_swiglu(x, gain, mod, w_gate, w_up, w_down, next_gain, next_mod, rows_per_batch, tm=512, tf=512):
    t, d = x.shape
    f_dim = w_gate.shape[1]
    tm = _tile(rows_per_batch, tm)
    tf = min(tf, f_dim)
    tpb = rows_per_batch // tm
    return pl.pallas_call(
        _ffn_kernel,
        grid=(t // tm, f_dim // tf),
        in_specs=[pl.BlockSpec((tm, d), lambda i, f: (i, 0)),
                  pl.BlockSpec((1, d), lambda i, f: (0, 0)),
                  pl.BlockSpec((1, N_MOD, d), lambda i, f: (i // tpb, 0, 0)),
                  pl.BlockSpec((d, tf), lambda i, f: (0, f)),
                  pl.BlockSpec((d, tf), lambda i, f: (0, f)),
                  pl.BlockSpec((tf, d), lambda i, f: (f, 0)),
                  pl.BlockSpec((1, d), lambda i, f: (0, 0)),
                  pl.BlockSpec((1, N_MOD, d), lambda i, f: (i // tpb, 0, 0))],
        out_specs=[pl.BlockSpec((tm, d), lambda i, f: (i, 0)),
                   pl.BlockSpec((d // LANES, tm, LANES), lambda i, f: (0, i, 0))],
        out_shape=[jax.ShapeDtypeStruct((t, d), F32), jax.ShapeDtypeStruct((d // LANES, t, LANES), F32)],
        scratch_shapes=[pltpu.VMEM((tm, d), BF16), pltpu.VMEM((tm, d), F32)],
        compiler_params=_params("parallel", "arbitrary"),
        name="ffn_swiglu",
    )(x, gain.reshape(1, d), mod, w_gate, w_up, w_down, next_gain.reshape(1, d), next_mod)


def s5_operators(lam_re, lam_im, log_step, b_re, b_im, c_re, c_im, n_chunks):
    g_all, p = lam_re.shape
    n = b_re.shape[-1]
    sub, tg = S5_SUB, S5_TILE_GROUPS
    nt = g_all // tg
    lam = lax.complex(lam_re.astype(F32), lam_im.astype(F32))
    lam_dt = lam * jnp.exp(log_step.astype(F32))[:, None]
    lam_bar = jnp.exp(lam_dt)
    b_bar = ((lam_bar - 1.0) / lam)[:, :, None] * lax.complex(b_re.astype(F32), b_im.astype(F32))
    c_mat = lax.complex(c_re.astype(F32), c_im.astype(F32))
    par = ((jnp.arange(tg) % 2)[:, None] == jnp.arange(2)[None, :]).astype(F32)
    ones_n = jnp.ones((n,), F32)

    def lay_in(a):
        z = jnp.einsum('qgpn,gr->qgnrp', a.reshape(nt, tg, p, n), par)
        return z.reshape(nt, tg * n, 2 * p)

    def lay_out(a):
        z = jnp.einsum('qgmp,gr->qrpgm', a.reshape(nt, tg, n, p), par)
        return z.reshape(nt, 2 * p, tg * n)

    lam_g = lam_bar.reshape(nt, tg, p)
    lam_in = jnp.einsum('qgp,n,r->qgnrp', lam_g, ones_n.astype(lam_g.dtype),
                        jnp.ones((2,), lam_g.dtype)).reshape(nt, tg * n, 2 * p)
    lam_out = jnp.einsum('qgp,m,r->qrpgm', lam_g, ones_n.astype(lam_g.dtype),
                         jnp.ones((2,), lam_g.dtype)).reshape(nt, 2 * p, tg * n)
    base = jnp.stack([lay_in(b_bar.real), lay_in(b_bar.imag), lam_in.real, lam_in.imag,
                      lay_out(c_mat.real), lay_out(c_mat.imag), lam_out.real, lam_out.imag], axis=1)

    n_lvl = max(1, (n_chunks - 1).bit_length())
    lv = jnp.exp(lam_dt[None] * (sub * 2.0 ** jnp.arange(n_lvl, dtype=F32))[:, None, None])
    lv = lv.reshape(n_lvl, nt, tg * p)
    lam_lv = jnp.stack([lv.real, lv.imag], axis=2).transpose(1, 0, 2, 3)
    return base, lam_lv


def _gelu_tanh(x):
    return 0.5 * x * (1.0 + jnp.tanh(0.7978845608028654 * (x + 0.044715 * (x * x * x))))


def _s5_kernel(u_ref, base_ref, lam_ref, d_ref, o_ref, ucat, bcat, ccat, krev, tpair):
    sub = 2 * ccat.shape[0]
    cw = u_ref.shape[2]
    r = u_ref.shape[1] // sub
    sw = bcat.shape[1] // 2
    pair = base_ref.shape[3]

    @pl.when(pl.program_id(1) == 0)
    def _():
        rb = lax.broadcasted_iota(jnp.int32, (cw, sw), 0) // (2 * S5_GROUP)
        cb = lax.broadcasted_iota(jnp.int32, (cw, sw), 1) // pair
        in_mask = rb == cb
        rc = lax.broadcasted_iota(jnp.int32, (sw, cw), 0) // pair
        cc = lax.broadcasted_iota(jnp.int32, (sw, cw), 1) // (2 * S5_GROUP)
        out_mask = rc == cc
        reps = sw // pair

        def expand(x, axis, mask):
            return jnp.where(mask, jnp.concatenate([x] * reps, axis=axis), 0.0).astype(BF16)

        def cmul(ar, ai, br, bi):
            return ar * br - ai * bi, ar * bi + ai * br

        b_r, b_i, lb_r, lb_i = (base_ref[0, k] for k in range(4))
        for l in reversed(range(sub)):
            bcat[l * cw:(l + 1) * cw, 0:sw] = expand(b_r, 1, in_mask)
            bcat[l * cw:(l + 1) * cw, sw:2 * sw] = expand(b_i, 1, in_mask)
            b_r, b_i = cmul(b_r, b_i, lb_r, lb_i)
        b_now = bcat[(sub - 1) * cw:sub * cw, :]

        c_r, c_i, lc_r, lc_i = (base_ref[0, k] for k in range(4, 8))
        c_now = jnp.concatenate([expand(c_r, 0, out_mask), expand(-c_i, 0, out_mask)], axis=0)
        krev[(sub - 1) * cw:sub * cw, :] = _dot(b_now, c_now).astype(BF16)
        for l in range(sub):
            c_r, c_i = cmul(c_r, c_i, lc_r, lc_i)
            half = slice((l % 2) * cw, (l % 2 + 1) * cw)
            ccat[l // 2, 0:sw, half] = expand(c_r, 0, out_mask)
            ccat[l // 2, sw:2 * sw, half] = expand(-c_i, 0, out_mask)
            if l < sub - 1:
                krev[(sub - 2 - l) * cw:(sub - 1 - l) * cw, :] = _dot(b_now, ccat[l // 2, :, half]).astype(BF16)
        for p in range(sub // 2):
            off = p * (p + 1) * cw
            n0 = (2 * p + 1) * cw
            tpair[off:off + n0, 0:cw] = krev[(sub - 1 - 2 * p) * cw:sub * cw, :]
            tpair[off + n0:off + n0 + cw, 0:cw] = jnp.zeros((cw, cw), BF16)
            tpair[off:off + n0 + cw, cw:2 * cw] = krev[(sub - 2 - 2 * p) * cw:sub * cw, :]

    for l in range(sub):
        ucat[:, l * cw:(l + 1) * cw] = u_ref[0, pl.ds(l, r, stride=sub), :].astype(BF16)

    v = _dot(ucat[...], bcat[...])
    s_re, s_im = v[:, 0:sw], v[:, sw:2 * sw]
    rowi = lax.broadcasted_iota(jnp.int32, (r, sw), 0)
    shift, lvl = 1, 0
    while shift < r:
        lr = lam_ref[0, lvl, 0:1, :]
        li = lam_ref[0, lvl, 1:2, :]
        keep = rowi >= shift
        p_re = jnp.where(keep, pltpu.roll(s_re, shift, axis=0), 0.0)
        p_im = jnp.where(keep, pltpu.roll(s_im, shift, axis=0), 0.0)
        s_re, s_im = s_re + lr * p_re - li * p_im, s_im + lr * p_im + li * p_re
        shift, lvl = shift * 2, lvl + 1
    first = rowi >= 1
    x_re = jnp.where(first, pltpu.roll(s_re, 1, axis=0), 0.0)
    x_im = jnp.where(first, pltpu.roll(s_im, 1, axis=0), 0.0)
    xb = jnp.concatenate([x_re, x_im], axis=1).astype(BF16)

    for p in range(sub // 2):
        off = p * (p + 1) * cw
        n_in = (2 * p + 2) * cw
        y2 = _dot(ucat[:, 0:n_in], tpair[off:off + n_in, :]) + _dot(xb, ccat[p])
        for l in (2 * p, 2 * p + 1):
            y = y2[:, (l % 2) * cw:(l % 2 + 1) * cw]
            ul = u_ref[0, pl.ds(l, r, stride=sub), :]
            o_ref[0, pl.ds(l, r, stride=sub), :] = _gelu_tanh(y + d_ref[...] * ul)


def s5_scan_gelu(u, ops, d_skip, batch, seq):
    base, lam_lv = ops
    nt, n_base, cw, pair = base.shape
    t = u.shape[1]
    d = nt * cw
    sub = S5_SUB
    sw = S5_TILE_GROUPS * S5_STATE
    n_lvl = lam_lv.shape[1]
    return pl.pallas_call(
        _s5_kernel,
        grid=(nt, batch),
        in_specs=[pl.BlockSpec((1, seq, cw), lambda c, b: (c, b, 0)),
                  pl.BlockSpec((1, n_base, cw, pair), lambda c, b: (c, 0, 0, 0)),
                  pl.BlockSpec((1, n_lvl, 2, sw), lambda c, b: (c, 0, 0, 0)),
                  pl.BlockSpec((1, cw), lambda c, b: (0, c))],
        out_specs=pl.BlockSpec((1, seq, cw), lambda c, b: (c, b, 0)),
        out_shape=jax.ShapeDtypeStruct((nt, t, cw), F32),
        scratch_shapes=[pltpu.VMEM((seq // sub, sub * cw), BF16),
                        pltpu.VMEM((sub * cw, 2 * sw), BF16),
                        pltpu.VMEM((sub // 2, 2 * sw, 2 * cw), BF16),
                        pltpu.VMEM((sub * cw, cw), BF16),
                        pltpu.VMEM(((sub // 2) * (sub // 2 + 1) * cw, 2 * cw), BF16)],
        compiler_params=_params("parallel", "arbitrary"),
        name="s5_scan",
    )(u, base, lam_lv, d_skip.reshape(1, d))


def _glu_kernel(g_ref, wa_ref, wb_ref, x_ref, mod_ref, o_ref):
    g = jnp.concatenate([g_ref[c] for c in range(g_ref.shape[0])], axis=1).astype(BF16)
    mix = _dot(g, wa_ref[...]) * jax.nn.sigmoid(_dot(g, wb_ref[...]))
    o_ref[...] = x_ref[...] + mod_ref[0][2:3] * mix


def glu_out(g, w_a, w_b, x, mod, rows_per_batch, tm=512, tn=1024):
    t, d = x.shape
    tm = _tile(rows_per_batch, tm)
    tn = min(tn, d)
    tpb = rows_per_batch // tm
    return pl.pallas_call(
        _glu_kernel,
        grid=(d // tn, t // tm),
        in_specs=[pl.BlockSpec((g.shape[0], tm, g.shape[2]), lambda j, i: (0, i, 0)),
                  pl.BlockSpec((d, tn), lambda j, i: (0, j)),
                  pl.BlockSpec((d, tn), lambda j, i: (0, j)),
                  pl.BlockSpec((tm, tn), lambda j, i: (i, j)),
                  pl.BlockSpec((1, N_MOD, tn), lambda j, i: (i // tpb, 0, j))],
        out_specs=pl.BlockSpec((tm, tn), lambda j, i: (i, j)),
        out_shape=jax.ShapeDtypeStruct((t, d), F32),
        compiler_params=_params("parallel", "parallel"),
        name="glu_out",
    )(g, w_a, w_b, x, mod)


def _router_kernel(x_ref, g_ref, mod_ref, rw_ref, rb_ref, h_ref, r_ref):
    m = mod_ref[0]
    h = _norm_mod(x_ref[...], g_ref[...], m[3:4], m[4:5])
    h_ref[...] = h.astype(BF16)
    logits = _dot_split(h, rw_ref[...]) + rb_ref[...]
    lane = lax.broadcasted_iota(jnp.int32, logits.shape, 1)
    logits = jnp.where(lane < N_EXPERTS, logits, NEG_INF)
    m1 = jnp.max(logits, axis=1, keepdims=True)
    i1 = jnp.min(jnp.where(logits == m1, lane, LANES), axis=1, keepdims=True)
    rest = jnp.where(lane == i1, NEG_INF, logits)
    m2 = jnp.max(rest, axis=1, keepdims=True)
    i2 = jnp.min(jnp.where(rest == m2, lane, LANES), axis=1, keepdims=True)
    e2 = jnp.exp(m2 - m1)
    g1 = 1.0 / (1.0 + e2)
    g2 = e2 / (1.0 + e2)
    r_ref[...] = jnp.where(lane == 0, i1.astype(F32),
                           jnp.where(lane == 1, i2.astype(F32),
                                     jnp.where(lane == 2, g1, jnp.where(lane == 3, g2, 0.0))))


def moe_router(x, gain, mod, router_w, router_b, rows_per_batch, tm=512):
    t, d = x.shape
    e = router_w.shape[1]
    tm = _tile(rows_per_batch, tm)
    tpb = rows_per_batch // tm
    rw = _split_cols(jnp.pad(router_w.astype(F32), ((0, 0), (0, LANES - e))))
    rb = jnp.pad(router_b.astype(F32), (0, LANES - e)).reshape(1, LANES)
    return pl.pallas_call(
        _router_kernel,
        grid=(t // tm,),
        in_specs=[pl.BlockSpec((tm, d), lambda i: (i, 0)),
                  pl.BlockSpec((1, d), lambda i: (0, 0)),
                  pl.BlockSpec((1, N_MOD, d), lambda i: (i // tpb, 0, 0)),
                  pl.BlockSpec((d, 2 * LANES), lambda i: (0, 0)),
                  pl.BlockSpec((1, LANES), lambda i: (0, 0))],
        out_specs=[pl.BlockSpec((tm, d), lambda i: (i, 0)),
                   pl.BlockSpec((tm, LANES), lambda i: (i, 0))],
        out_shape=[jax.ShapeDtypeStruct((t, d), BF16), jax.ShapeDtypeStruct((t, LANES), F32)],
        compiler_params=_params("parallel"),
        name="moe_router",
    )(x, gain.reshape(1, d), mod, rw, rb)


def moe_dispatch(top_e, rows, sub_rows):
    t = top_e.shape[0]
    n_assign = t * TOP_K
    n_blocks = -(-n_assign // rows) + N_EXPERTS
    e_flat = top_e.reshape(-1)
    onehot = (e_flat[:, None] == jnp.arange(N_EXPERTS, dtype=jnp.int32)[None, :]).astype(jnp.int32)
    csum = jnp.cumsum(onehot, axis=0)
    rank = jnp.sum((csum - onehot) * onehot, axis=1)
    counts = csum[-1]
    padded = (counts + rows - 1) // rows * rows
    pad_end = jnp.cumsum(padded)
    pad_start = pad_end - padded
    dest = jnp.sum(onehot * pad_start[None, :], axis=1) + rank
    tok = jnp.arange(n_assign, dtype=jnp.int32) // TOP_K
    spread = jnp.arange(n_blocks * rows, dtype=jnp.int32) % t
    row_tok = spread.at[dest].set(tok)
    n_active = pad_end[-1] // rows
    blk = jnp.arange(n_blocks, dtype=jnp.int32)
    blk_start = jnp.minimum(blk, n_active - 1) * rows
    block_expert = jnp.minimum(jnp.sum((blk_start[:, None] >= pad_end[None, :]).astype(jnp.int32), axis=1),
                               N_EXPERTS - 1)
    real_rows = jnp.clip(pad_start[block_expert] + counts[block_expert] - blk_start, 0, rows)
    n_sub = jnp.where(blk < n_active, (real_rows + sub_rows - 1) // sub_rows, 0).astype(jnp.int32)
    return row_tok, dest.reshape(t, TOP_K), block_expert, n_sub


def _expert_kernel(be_ref, ns_ref, x_ref, wg_ref, wu_ref, wd_ref, o_ref, acc_ref, *, sub_rows):
    i = pl.program_id(0)
    f = pl.program_id(1)
    rows = x_ref.shape[0]

    @pl.when(f == 0)
    def _():
        acc_ref[...] = jnp.zeros_like(acc_ref)

    def run(n_rows):
        x = x_ref[0:n_rows, :]
        act = (_silu(_dot(x, wg_ref[0].astype(BF16))) * _dot(x, wu_ref[0].astype(BF16))).astype(BF16)
        acc_ref[0:n_rows, :] += _dot(act, wd_ref[0].astype(BF16))

    for s in range(1, rows // sub_rows + 1):
        pl.when(ns_ref[i] == s)(functools.partial(run, s * sub_rows))

    @pl.when(f == pl.num_programs(1) - 1)
    def _():
        o_ref[...] = acc_ref[...].astype(BF16)


def moe_experts(xg, block_expert, n_sub, w_gate, w_up, w_down, layer, rows, sub_rows, tf=512):
    r_tot, d = xg.shape
    f_dim = w_gate.shape[2]
    tf = min(tf, f_dim)
    nf = f_dim // tf
    n_blocks = r_tot // rows
    e0 = layer * N_EXPERTS

    def f_idx(i, f, ns):
        return jnp.where(ns[i] > 0, f, nf - 1)

    grid_spec = pltpu.PrefetchScalarGridSpec(
        num_scalar_prefetch=2,
        grid=(n_blocks, nf),
        in_specs=[pl.BlockSpec((rows, d), lambda i, f, be, ns: (i, 0), pipeline_mode=pl.Buffered(1)),
                  pl.BlockSpec((1, d, tf), lambda i, f, be, ns: (e0 + be[i], 0, f_idx(i, f, ns))),
                  pl.BlockSpec((1, d, tf), lambda i, f, be, ns: (e0 + be[i], 0, f_idx(i, f, ns))),
                  pl.BlockSpec((1, tf, d), lambda i, f, be, ns: (e0 + be[i], f_idx(i, f, ns), 0))],
        out_specs=pl.BlockSpec((rows, d), lambda i, f, be, ns: (i, 0)),
        scratch_shapes=[pltpu.VMEM((rows, d), F32)],
    )
    return pl.pallas_call(
        functools.partial(_expert_kernel, sub_rows=sub_rows),
        grid_spec=grid_spec,
        out_shape=jax.ShapeDtypeStruct((r_tot, d), BF16),
        compiler_params=_params("arbitrary", "arbitrary"),
        name="moe_experts",
    )(block_expert, n_sub, xg, w_gate, w_up, w_down)


def _combine_kernel(x_ref, y_ref, r_ref, mod_ref, gf_ref, o_ref, *, final_norm):
    r = r_ref[...]
    d = x_ref.shape[1]
    ff = r[:, 2:3] * y_ref[:, 0:d].astype(F32) + r[:, 3:4] * y_ref[:, d:2 * d].astype(F32)
    x_new = x_ref[...] + mod_ref[0][5:6] * ff
    if final_norm:
        ms = jnp.mean(x_new * x_new, axis=-1, keepdims=True)
        x_new = x_new * lax.rsqrt(ms + EPS) * gf_ref[...]
    o_ref[...] = x_new


def moe_combine(x, y12, route, mod, g_final, rows_per_batch, final_norm, tm=512):
    t, d = x.shape
    tm = _tile(rows_per_batch, tm)
    tpb = rows_per_batch // tm
    row = lambda w: pl.BlockSpec((tm, w), lambda i: (i, 0))
    return pl.pallas_call(
        functools.partial(_combine_kernel, final_norm=final_norm),
        grid=(t // tm,),
        in_specs=[row(d), row(2 * d), row(LANES),
                  pl.BlockSpec((1, N_MOD, d), lambda i: (i // tpb, 0, 0)),
                  pl.BlockSpec((1, d), lambda i: (0, 0))],
        out_specs=row(d),
        out_shape=jax.ShapeDtypeStruct((t, d), F32),
        compiler_params=_params("parallel"),
        name="moe_combine",
    )(x, y12, route, mod, g_final.reshape(1, d))


def kernel(x, c, w_ada, b_ada, g_mix, g_ffn, g_final, w_in, conv_w, b_igate, b_fgate, mh_gain, w_out,
           ffn_w_gate, ffn_w_up, ffn_w_down, s5_lam_re, s5_lam_im, s5_log_step, s5_b_re, s5_b_im,
           s5_c_re, s5_c_im, s5_d, glu_w_a, glu_w_b, router_w, router_b, exp_w_gate, exp_w_up, exp_w_down):
    batch, seq, d = x.shape
    depth = w_ada.shape[0]
    t = batch * seq
    m_width = mh_gain.shape[1]
    heads_b = (d - m_width) // MOBA_HEAD_DIM
    n_gate = 2 * MLSTM_HEADS

    mods = ada_mod(c, w_ada, b_ada)
    xs = x.reshape(t, d)
    for layer in range(depth):
        i = layer // 2
        mod = mods[layer]
        if layer % 2 == 0:
            w = w_in[i]
            g0 = 4 * m_width
            w_big = jnp.concatenate([w[:, :g0], w[:, g0 + n_gate:]], axis=1).astype(BF16)
            w_gates = _split_cols(jnp.pad(w[:, g0:g0 + n_gate], ((0, 0), (0, LANES - n_gate))))
            proj, gates = in_proj(xs, g_mix[layer], mod, w_big, w_gates, seq)
            hm = mlstm_mix(proj, gates, conv_w[i], b_igate[i], b_fgate[i], mh_gain[i], batch, seq)
            hb = moba_mix(proj, g0 // MOBA_HEAD_DIM, batch, seq, heads_b)
            xs = out_proj(hm, hb, w_out[i].astype(BF16), xs, mod, seq)
            f_pad = -ffn_w_gate.shape[2] % 512
            wg = jnp.pad(ffn_w_gate[i], ((0, 0), (0, f_pad))).astype(BF16)
            wu = jnp.pad(ffn_w_up[i], ((0, 0), (0, f_pad))).astype(BF16)
            wd = jnp.pad(ffn_w_down[i], ((0, f_pad), (0, 0))).astype(BF16)
            xs, u = ffn_swiglu(xs, g_ffn[layer], mod, wg, wu, wd, g_mix[layer + 1], mods[layer + 1], seq)
        else:
            ops = s5_operators(s5_lam_re[i], s5_lam_im[i], s5_log_step[i], s5_b_re[i], s5_b_im[i],
                               s5_c_re[i], s5_c_im[i], seq // S5_SUB)
            g = s5_scan_gelu(u, ops, s5_d[i], batch, seq)
            xs = glu_out(g, glu_w_a[i].astype(BF16), glu_w_b[i].astype(BF16), xs, mod, seq)
            h, route = moe_router(xs, g_ffn[layer], mod, router_w[i], router_b[i], seq)
            top_e = route[:, 0:TOP_K].astype(jnp.int32)
            row_tok, pos, block_expert, n_sub = moe_dispatch(top_e, MOE_ROWS, MOE_SUB_ROWS)
            stack = lambda w_: w_.reshape((-1,) + w_.shape[2:])
            y_rows = moe_experts(h[row_tok], block_expert, n_sub, stack(exp_w_gate), stack(exp_w_up),
                                 stack(exp_w_down), i, MOE_ROWS, MOE_SUB_ROWS)
            xs = moe_combine(xs, y_rows[pos.reshape(-1)].reshape(t, TOP_K * d), route, mod, g_final, seq,
                             final_norm=(layer == depth - 1))
    if depth % 2 == 1:
        raise NotImplementedError("layers come in (even, odd) pairs: the S5 input and the final norm are fused")
    return xs.reshape(batch, seq, d)
```

```python
import functools
import math

import jax
import jax.numpy as jnp
from jax import lax
from jax.experimental import pallas as pl
from jax.experimental.pallas import tpu as pltpu

F32 = jnp.float32
BF16 = jnp.bfloat16
HI = lax.Precision.HIGHEST
NEG_INF = float("-inf")

EPS = 1e-6
N_MOD = 6
MLSTM_HEADS = 4
MLSTM_CHUNK = 128
MLSTM_CONV = 4
MOBA_HEAD_DIM = 128
MOBA_BLOCK = 256
MOBA_TOPK = 3
S5_GROUP = 16
S5_STATE = 64
S5_SUB = 16
S5_TILE_GROUPS = 8
N_EXPERTS = 8
TOP_K = 2
MOE_ROWS = 1024
MOE_SUB_ROWS = 256
LANES = 128


def _tile(n, pref, align=8):
    for cand in range(min(pref, n), 0, -1):
        if n % cand == 0 and cand % align == 0:
            return cand
    raise ValueError(f"no {align}-aligned tile divides {n}")


def _params(*sem):
    return pltpu.CompilerParams(dimension_semantics=sem)


def _dot(a, b):
    return jnp.dot(a, b, preferred_element_type=F32)


def _dot_nt(a, b, precision=None):
    return lax.dot_general(a, b, (((1,), (1,)), ((), ())), precision=precision,
                           preferred_element_type=F32)


def _silu(x):
    return x * jax.nn.sigmoid(x)


def _split_cols(w):
    hi = w.astype(BF16)
    lo = (w - hi.astype(F32)).astype(BF16)
    return jnp.concatenate([hi, lo], axis=1)


def _dot_split(x, w_cat):
    hi = x.astype(BF16)
    lo = (x - hi.astype(F32)).astype(BF16)
    r = _dot(hi, w_cat) + _dot(lo, w_cat)
    n = w_cat.shape[1] // 2
    return r[:, :n] + r[:, n:]


def _norm_mod(x, gain, shift, scale):
    ms = jnp.mean(x * x, axis=-1, keepdims=True)
    y = x * lax.rsqrt(ms + EPS) * gain
    return y * (1.0 + scale) + shift


def _ada_kernel(c_ref, w_ref, b_ref, o_ref):
    cond = _silu(c_ref[...])
    bp = cond.shape[0]
    c_hi = cond.astype(BF16).astype(F32)
    lhs = jnp.concatenate([c_hi, cond - c_hi], axis=0).astype(BF16)
    w = w_ref[0]
    w_hi = w.astype(BF16)
    w_lo = (w - w_hi.astype(F32)).astype(BF16)
    r = _dot(lhs, w_hi) + _dot(lhs, w_lo)
    o_ref[0] = r[0:bp] + r[bp:2 * bp] + b_ref[0]


def ada_mod(c, w_ada, b_ada, tn=2048):
    depth, d, n = w_ada.shape
    b = c.shape[0]
    bp = 8
    cp = jnp.pad(c, ((0, bp - b), (0, 0)))
    out = pl.pallas_call(
        _ada_kernel,
        grid=(depth, n // tn),
        in_specs=[pl.BlockSpec((bp, d), lambda l, j: (0, 0)),
                  pl.BlockSpec((1, d, tn), lambda l, j: (l, 0, j)),
                  pl.BlockSpec((1, 1, tn), lambda l, j: (l, 0, j))],
        out_specs=pl.BlockSpec((1, bp, tn), lambda l, j: (l, 0, j)),
        out_shape=jax.ShapeDtypeStruct((depth, bp, n), F32),
        compiler_params=_params("parallel", "parallel"),
        name="ada_mod",
    )(cp, w_ada, b_ada.reshape(depth, 1, n))
    return out[:, :b].reshape(depth, b, N_MOD, d)


def _inproj_kernel(x_ref, g_ref, mod_ref, w_ref, wg_ref, o_ref, og_ref, h_ref):
    @pl.when(pl.program_id(1) == 0)
    def _():
        m = mod_ref[0]
        h = _norm_mod(x_ref[...], g_ref[...], m[0:1], m[1:2])
        h_ref[...] = h.astype(BF16)
        og_ref[...] = _dot_split(h, wg_ref[...])

    o_ref[...] = _dot(h_ref[...], w_ref[...]).astype(BF16)


def in_proj(x, gain, mod, w_big, w_gates, rows_per_batch, tm=1024, tn=1024):
    t, d = x.shape
    n = w_big.shape[1]
    tm = _tile(rows_per_batch, tm)
    tn = min(tn, n)
    tpb = rows_per_batch // tm
    return pl.pallas_call(
        _inproj_kernel,
        grid=(t // tm, n // tn),
        in_specs=[pl.BlockSpec((tm, d), lambda i, j: (i, 0)),
                  pl.BlockSpec((1, d), lambda i, j: (0, 0)),
                  pl.BlockSpec((1, N_MOD, d), lambda i, j: (i // tpb, 0, 0)),
                  pl.BlockSpec((d, tn), lambda i, j: (0, j)),
                  pl.BlockSpec((d, 2 * LANES), lambda i, j: (0, 0))],
        out_specs=[pl.BlockSpec((tm, tn), lambda i, j: (i, j)),
                   pl.BlockSpec((tm, LANES), lambda i, j: (i, 0))],
        out_shape=[jax.ShapeDtypeStruct((t, n), BF16), jax.ShapeDtypeStruct((t, LANES), F32)],
        scratch_shapes=[pltpu.VMEM((tm, d), BF16)],
        compiler_params=_params("parallel", "arbitrary"),
        name="in_proj",
    )(x, gain.reshape(1, d), mod, w_big, w_gates)


def _mlstm_kernel(bias_ref, q_ref, k_ref, v_ref, o_ref, gi_ref, gf_ref, cwq_ref, cwk_ref, gain_ref,
                  out_ref, qbuf, kbuf, c_st, n_st, m_st):
    chunk = pl.program_id(1)
    L = q_ref.shape[0]
    heads, dh = c_st.shape[0], c_st.shape[1]
    taps = cwq_ref.shape[0]
    halo = 8

    @pl.when(chunk == 0)
    def _():
        qbuf[0:halo] = jnp.zeros((halo, qbuf.shape[1]), F32)
        kbuf[0:halo] = jnp.zeros((halo, kbuf.shape[1]), F32)
        c_st[...] = jnp.zeros_like(c_st)
        n_st[...] = jnp.zeros_like(n_st)
        m_st[...] = jnp.full(m_st.shape, -1e30, F32)

    def conv_silu(src_ref, buf, w_ref):
        buf[halo:halo + L] = src_ref[...].astype(F32)
        w = w_ref[...]
        acc = buf[halo:halo + L] * w[taps - 1:taps]
        for d in range(1, taps):
            acc = acc + buf[pl.ds(halo - d, L), :] * w[taps - 1 - d:taps - d]
        buf[0:halo] = buf[L:L + halo]
        return _silu(acc)

    q_all = conv_silu(q_ref, qbuf, cwq_ref)
    k_all = conv_silu(k_ref, kbuf, cwk_ref) * (dh ** -0.5)

    row = lax.broadcasted_iota(jnp.int32, (L, L), 0)
    col = lax.broadcasted_iota(jnp.int32, (L, L), 1)
    eye = row == col

    def to_col(x_row):
        return jnp.sum(jnp.where(eye, jnp.broadcast_to(x_row, (L, L)), 0.0), axis=1, keepdims=True)

    fz = jnp.concatenate([gf_ref[0, h, 0] + bias_ref[1, h] for h in range(heads)]
                         + [jnp.zeros((8 - heads, L), F32)], axis=0)
    lf_rows = jnp.minimum(fz, 0.0) - jnp.log(1.0 + jnp.exp(-jnp.abs(fz)))
    g_rows = jnp.dot(lf_rows, (row <= col).astype(F32), precision=HI,
                     preferred_element_type=F32)

    for h in range(heads):
        sl = slice(h * dh, (h + 1) * dh)
        q, k, vb = q_all[:, sl], k_all[:, sl], v_ref[:, sl]
        ig_row = gi_ref[0, h, 0] + bias_ref[0, h]
        g_row = g_rows[h:h + 1]
        g_col = to_col(g_row)
        b_row = ig_row - g_row
        d_mat = jnp.where(col <= row, g_col + b_row, NEG_INF)
        m_prev = m_st[h]
        m_inter = g_col + m_prev
        m_t = jnp.maximum(m_inter, jnp.max(d_mat, axis=1, keepdims=True))
        qb = q.astype(BF16)
        kb = k.astype(BF16)
        s = _dot_nt(qb, kb) * jnp.exp(d_mat - m_t)
        decay = jnp.exp(m_inter - m_t)
        num = _dot(s.astype(BF16), vb) + decay * _dot(qb, c_st[h].astype(BF16))
        den = jnp.sum(s, axis=1, keepdims=True) + decay * jnp.sum(q * n_st[h], axis=1, keepdims=True)
        hh = num / jnp.maximum(jnp.abs(den), jnp.exp(-m_t))

        g_last = g_row[:, L - 1:L]
        a_row = g_last + b_row
        m_new = jnp.maximum(g_last + m_prev, jnp.max(a_row, axis=1, keepdims=True))
        w_col = to_col(jnp.exp(a_row - m_new))
        carry = jnp.exp(g_last + m_prev - m_new)
        kw = k * w_col
        c_st[h] = carry * c_st[h] + _dot(kw.T.astype(BF16), vb)
        n_st[h] = carry * n_st[h] + jnp.sum(kw, axis=0, keepdims=True)
        m_st[h] = m_new

        hn = hh * lax.rsqrt(jnp.mean(hh * hh, axis=1, keepdims=True) + EPS) * gain_ref[:, sl]
        out_ref[:, sl] = (hn * jax.nn.sigmoid(o_ref[:, sl].astype(F32))).astype(BF16)


def mlstm_mix(proj, gates, conv_w, b_igate, b_fgate, mh_gain, batch, seq):
    heads, L = MLSTM_HEADS, MLSTM_CHUNK
    width = mh_gain.shape[0]
    dh = width // heads
    nc = seq // L
    t = batch * seq

    def rows(a):
        return a.reshape(batch, nc, L, heads).transpose(0, 3, 1, 2).reshape(batch, heads, nc, 1, L)

    gi = rows(gates[:, 0:heads])
    gf = rows(gates[:, heads:2 * heads])
    bias = jnp.stack([b_igate, b_fgate]).astype(F32)
    blk = lambda off: pl.BlockSpec((L, width), lambda b, c: (b * nc + c, off))
    gspec = pl.BlockSpec((1, heads, 1, 1, L), lambda b, c: (b, 0, c, 0, 0))
    return pl.pallas_call(
        _mlstm_kernel,
        grid=(batch, nc),
        in_specs=[pl.BlockSpec(memory_space=pltpu.SMEM),
                  blk(0), blk(1), blk(2), blk(3), gspec, gspec,
                  pl.BlockSpec((MLSTM_CONV, width), lambda b, c: (0, 0)),
                  pl.BlockSpec((MLSTM_CONV, width), lambda b, c: (0, 1)),
                  pl.BlockSpec((1, width), lambda b, c: (0, 0))],
        out_specs=pl.BlockSpec((L, width), lambda b, c: (b * nc + c, 0)),
        out_shape=jax.ShapeDtypeStruct((t, width), BF16),
        scratch_shapes=[pltpu.VMEM((L + 8, width), F32), pltpu.VMEM((L + 8, width), F32),
                        pltpu.VMEM((heads, dh, dh), F32), pltpu.VMEM((heads, 1, dh), F32),
                        pltpu.VMEM((heads, 1, 1), F32)],
        compiler_params=_params("parallel", "arbitrary"),
        name="mlstm",
    )(bias, proj, proj, proj, proj, gi, gf, conv_w, conv_w, mh_gain.reshape(1, width))


MOBA_MASK_BIAS = -1e9
MOBA_WIDTH_STEP = 2
MOBA_CHUNK_BLOCKS = 2
MOBA_HEAD_GROUP = 2


def _moba_kernel(q_ref, k_ref, v_ref, o_ref, kmean_ref, kaug_ref, qaug_ref, s_ref):
    j = pl.program_id(1)
    blk = o_ref.shape[0]
    n_h, nbp, dh = kmean_ref.shape
    seq = k_ref.shape[0]
    nb = seq // blk
    exp_scale = dh ** -0.5 * math.log2(math.e)

    @pl.when(j == 0)
    def _():
        key_blk = lax.broadcasted_iota(jnp.int32, (seq, LANES), 0) // blk
        lane = lax.broadcasted_iota(jnp.int32, (seq, LANES), 1)
        block_onehot = jnp.where(key_blk == lane, 1.0, 0.0).astype(BF16)
        kmean_ref[...] = jnp.zeros_like(kmean_ref)
        for h in range(n_h):
            hs = slice(h * dh, (h + 1) * dh)
            for b in range(nb):
                kmean_ref[h, b:b + 1, :] = jnp.mean(k_ref[b * blk:(b + 1) * blk, hs].astype(F32), axis=0,
                                                    keepdims=True)
            kaug_ref[h, :, 0:dh] = k_ref[:, hs]
            kaug_ref[h, :, dh:dh + LANES] = block_onehot

            q_all = q_ref[:, hs]
            qaug_ref[h, :, 0:dh] = q_all
            gate_t = _dot_nt(kmean_ref[h], q_all.astype(F32), precision=HI)
            blk_id = lax.broadcasted_iota(jnp.int32, gate_t.shape, 0)
            q_blk = lax.broadcasted_iota(jnp.int32, gate_t.shape, 1) // blk
            valid = blk_id < q_blk
            sc = jnp.where(valid, gate_t, NEG_INF)
            beaten = jnp.zeros(gate_t.shape, F32)
            for b2 in range(nb):
                other = sc[b2:b2 + 1, :]
                wins = (other > sc) | ((other == sc) & (b2 < blk_id))
                beaten = beaten + wins.astype(F32)
            chosen = valid & (beaten < MOBA_TOPK)
            bias_t = jnp.where(chosen, 0.0, MOBA_MASK_BIAS)
            pad = jnp.zeros((LANES - nbp, blk), F32)
            for b in range(nb):
                piece = jnp.concatenate([bias_t[:, b * blk:(b + 1) * blk], pad], axis=0).T
                qaug_ref[h, b * blk:(b + 1) * blk, dh:dh + LANES] = piece.astype(BF16)

    start = pl.multiple_of(j * blk, blk)
    row = lax.broadcasted_iota(jnp.int32, (blk, blk), 0)
    col = lax.broadcasted_iota(jnp.int32, (blk, blk), 1)

    own = []
    for h in range(n_h):
        hs = slice(h * dh, (h + 1) * dh)
        q_aug = qaug_ref[h, pl.ds(start, blk), :]
        s_own = jnp.where(col <= row, _dot_nt(q_aug[:, 0:dh], k_ref[pl.ds(start, blk), hs]), NEG_INF)
        own.append((q_aug, s_own, jnp.max(s_own, axis=1, keepdims=True), v_ref[pl.ds(start, blk), hs]))

    @pl.when(j == 0)
    def _():
        for h, (_, s_own, m_own, v_own) in enumerate(own):
            p = jnp.exp2((s_own - m_own) * exp_scale)
            out = _dot(p.astype(BF16), v_own) / jnp.sum(p, axis=1, keepdims=True)
            o_ref[:, h * dh:(h + 1) * dh] = out.astype(BF16)

    def attend(n_blocks):
        w = n_blocks * blk
        step = MOBA_CHUNK_BLOCKS * blk
        chunks = [(lo_, min(lo_ + step, w)) for lo_ in range(0, w, step)]
        maxes = []
        for h, (q_aug, _, m_own, _) in enumerate(own):
            mx = jnp.full((blk, LANES), NEG_INF, F32)
            for lo_, hi_ in chunks:
                s_c = _dot_nt(q_aug, kaug_ref[h, lo_:hi_, :])
                s_ref[h, :, lo_:hi_] = s_c
                for t_ in range((hi_ - lo_) // LANES):
                    mx = jnp.maximum(mx, s_c[:, t_ * LANES:(t_ + 1) * LANES])
            maxes.append(jnp.maximum(jnp.max(mx, axis=1, keepdims=True), m_own))
        for h, (_, s_own, _, v_own) in enumerate(own):
            hs = slice(h * dh, (h + 1) * dh)
            m = maxes[h]
            p_own = jnp.exp2((s_own - m) * exp_scale)
            acc = _dot(p_own.astype(BF16), v_own)
            lsum = p_own[:, 0:LANES]
            for t_ in range(1, blk // LANES):
                lsum = lsum + p_own[:, t_ * LANES:(t_ + 1) * LANES]
            for lo_, hi_ in chunks:
                p = jnp.exp2((s_ref[h, :, lo_:hi_] - m) * exp_scale)
                for t_ in range((hi_ - lo_) // LANES):
                    lsum = lsum + p[:, t_ * LANES:(t_ + 1) * LANES]
                acc = acc + _dot(p.astype(BF16), v_ref[lo_:hi_, hs])
            o_ref[:, hs] = (acc / jnp.sum(lsum, axis=1, keepdims=True)).astype(BF16)

    lo = 0
    for hi in list(range(MOBA_WIDTH_STEP, nb - 1, MOBA_WIDTH_STEP)) + [nb - 1]:
        pl.when((j > lo) & (j <= hi))(functools.partial(attend, hi))
        lo = hi


def moba_mix(proj, col0, batch, seq, heads):
    dh, blk, grp = MOBA_HEAD_DIM, MOBA_BLOCK, MOBA_HEAD_GROUP
    nb = seq // blk
    nbp = -(-nb // 8) * 8
    t = batch * seq
    n_grp = heads // grp
    c0 = col0 // grp
    wide = lambda off: pl.BlockSpec((seq, grp * dh), lambda g, j: (g // n_grp, c0 + off * n_grp + g % n_grp))
    return pl.pallas_call(
        _moba_kernel,
        grid=(batch * n_grp, nb),
        in_specs=[wide(0), wide(1), wide(2)],
        out_specs=pl.BlockSpec((blk, grp * dh), lambda g, j: ((g // n_grp) * nb + j, g % n_grp)),
        out_shape=jax.ShapeDtypeStruct((t, heads * dh), BF16),
        scratch_shapes=[pltpu.VMEM((grp, nbp, dh), F32), pltpu.VMEM((grp, seq, dh + LANES), BF16),
                        pltpu.VMEM((grp, seq, dh + LANES), BF16), pltpu.VMEM((grp, blk, seq), F32)],
        compiler_params=_params("parallel", "arbitrary"),
        name="moba",
    )(proj, proj, proj)


def _outproj_kernel(hm_ref, hb_ref, w1_ref, w2_ref, x_ref, mod_ref, o_ref):
    acc = _dot(hm_ref[...], w1_ref[...]) + _dot(hb_ref[...], w2_ref[...])
    o_ref[...] = x_ref[...] + mod_ref[0][2:3] * acc


def out_proj(hm, hb, w_out, x, mod, rows_per_batch, tm=512):
    t, d = x.shape
    k1, k2 = hm.shape[1], hb.shape[1]
    tm = _tile(rows_per_batch, tm)
    tpb = rows_per_batch // tm
    return pl.pallas_call(
        _outproj_kernel,
        grid=(t // tm,),
        in_specs=[pl.BlockSpec((tm, k1), lambda i: (i, 0)),
                  pl.BlockSpec((tm, k2), lambda i: (i, 0)),
                  pl.BlockSpec((k1, d), lambda i: (0, 0)),
                  pl.BlockSpec((k2, d), lambda i: (0, 0)),
                  pl.BlockSpec((tm, d), lambda i: (i, 0)),
                  pl.BlockSpec((1, N_MOD, d), lambda i: (i // tpb, 0, 0))],
        out_specs=pl.BlockSpec((tm, d), lambda i: (i, 0)),
        out_shape=jax.ShapeDtypeStruct((t, d), F32),
        compiler_params=_params("parallel"),
        name="out_proj",
    )(hm, hb, w_out[:k1], w_out[k1:], x, mod)


def _ffn_kernel(x_ref, g_ref, mod_ref, wg_ref, wu_ref, wd_ref, g2_ref, mod2_ref, o_ref, u_ref, h_ref, acc_ref):
    f = pl.program_id(1)

    @pl.when(f == 0)
    def _():
        m = mod_ref[0]
        h_ref[...] = _norm_mod(x_ref[...], g_ref[...], m[3:4], m[4:5]).astype(BF16)
        acc_ref[...] = jnp.zeros_like(acc_ref)

    h = h_ref[...]
    act = (_silu(_dot(h, wg_ref[...])) * _dot(h, wu_ref[...])).astype(BF16)
    acc_ref[...] += _dot(act, wd_ref[...])

    @pl.when(f == pl.num_programs(1) - 1)
    def _():
        x_new = x_ref[...] + mod_ref[0][5:6] * acc_ref[...]
        o_ref[...] = x_new
        m2 = mod2_ref[0]
        u = _norm_mod(x_new, g2_ref[...], m2[0:1], m2[1:2])
        for c in range(u_ref.shape[0]):
            u_ref[c] = u[:, c * LANES:(c + 1) * LANES]


def ffn_swiglu(x, gain, mod, w_gate, w_up, w_down, next_gain, next_mod, rows_per_batch, tm=512, tf=512):
    t, d = x.shape
    f_dim = w_gate.shape[1]
    tm = _tile(rows_per_batch, tm)
    tf = min(tf, f_dim)
    tpb = rows_per_batch // tm
    return pl.pallas_call(
        _ffn_kernel,
        grid=(t // tm, f_dim // tf),
        in_specs=[pl.BlockSpec((tm, d), lambda i, f: (i, 0)),
                  pl.BlockSpec((1, d), lambda i, f: (0, 0)),
                  pl.BlockSpec((1, N_MOD, d), lambda i, f: (i // tpb, 0, 0)),
                  pl.BlockSpec((d, tf), lambda i, f: (0, f)),
                  pl.BlockSpec((d, tf), lambda i, f: (0, f)),
                  pl.BlockSpec((tf, d), lambda i, f: (f, 0)),
                  pl.BlockSpec((1, d), lambda i, f: (0, 0)),
                  pl.BlockSpec((1, N_MOD, d), lambda i, f: (i // tpb, 0, 0))],
        out_specs=[pl.BlockSpec((tm, d), lambda i, f: (i, 0)),
                   pl.BlockSpec((d // LANES, tm, LANES), lambda i, f: (0, i, 0))],
        out_shape=[jax.ShapeDtypeStruct((t, d), F32), jax.ShapeDtypeStruct((d // LANES, t, LANES), F32)],
        scratch_shapes=[pltpu.VMEM((tm, d), BF16), pltpu.VMEM((tm, d), F32)],
        compiler_params=_params("parallel", "arbitrary"),
        name="ffn_swiglu",
    )(x, gain.reshape(1, d), mod, w_gate, w_up, w_down, next_gain.reshape(1, d), next_mod)


def s5_operators(lam_re, lam_im, log_step, b_re, b_im, c_re, c_im, n_chunks):
    g_all, p = lam_re.shape
    n = b_re.shape[-1]
    sub, tg = S5_SUB, S5_TILE_GROUPS
    nt = g_all // tg
    lam = lax.complex(lam_re.astype(F32), lam_im.astype(F32))
    lam_dt = lam * jnp.exp(log_step.astype(F32))[:, None]
    lam_bar = jnp.exp(lam_dt)
    b_bar = ((lam_bar - 1.0) / lam)[:, :, None] * lax.complex(b_re.astype(F32), b_im.astype(F32))
    c_mat = lax.complex(c_re.astype(F32), c_im.astype(F32))
    par = ((jnp.arange(tg) % 2)[:, None] == jnp.arange(2)[None, :]).astype(F32)
    ones_n = jnp.ones((n,), F32)

    def lay_in(a):
        z = jnp.einsum('qgpn,gr->qgnrp', a.reshape(nt, tg, p, n), par)
        return z.reshape(nt, tg * n, 2 * p)

    def lay_out(a):
        z = jnp.einsum('qgmp,gr->qrpgm', a.reshape(nt, tg, n, p), par)
        return z.reshape(nt, 2 * p, tg * n)

    lam_g = lam_bar.reshape(nt, tg, p)
    lam_in = jnp.einsum('qgp,n,r->qgnrp', lam_g, ones_n.astype(lam_g.dtype),
                        jnp.ones((2,), lam_g.dtype)).reshape(nt, tg * n, 2 * p)
    lam_out = jnp.einsum('qgp,m,r->qrpgm', lam_g, ones_n.astype(lam_g.dtype),
                         jnp.ones((2,), lam_g.dtype)).reshape(nt, 2 * p, tg * n)
    base = jnp.stack([lay_in(b_bar.real), lay_in(b_bar.imag), lam_in.real, lam_in.imag,
                      lay_out(c_mat.real), lay_out(c_mat.imag), lam_out.real, lam_out.imag], axis=1)

    n_lvl = max(1, (n_chunks - 1).bit_length())
    lv = jnp.exp(lam_dt[None] * (sub * 2.0 ** jnp.arange(n_lvl, dtype=F32))[:, None, None])
    lv = lv.reshape(n_lvl, nt, tg * p)
    lam_lv = jnp.stack([lv.real, lv.imag], axis=2).transpose(1, 0, 2, 3)
    return base, lam_lv


def _gelu_tanh(x):
    return 0.5 * x * (1.0 + jnp.tanh(0.7978845608028654 * (x + 0.044715 * (x * x * x))))


def _s5_kernel(u_ref, base_ref, lam_ref, d_ref, o_ref, ucat, bcat, ccat, krev, tpair):
    sub = 2 * ccat.shape[0]
    cw = u_ref.shape[2]
    r = u_ref.shape[1] // sub
    sw = bcat.shape[1] // 2
    pair = base_ref.shape[3]

    @pl.when(pl.program_id(1) == 0)
    def _():
        rb = lax.broadcasted_iota(jnp.int32, (cw, sw), 0) // (2 * S5_GROUP)
        cb = lax.broadcasted_iota(jnp.int32, (cw, sw), 1) // pair
        in_mask = rb == cb
        rc = lax.broadcasted_iota(jnp.int32, (sw, cw), 0) // pair
        cc = lax.broadcasted_iota(jnp.int32, (sw, cw), 1) // (2 * S5_GROUP)
        out_mask = rc == cc
        reps = sw // pair

        def expand(x, axis, mask):
            return jnp.where(mask, jnp.concatenate([x] * reps, axis=axis), 0.0).astype(BF16)

        def cmul(ar, ai, br, bi):
            return ar * br - ai * bi, ar * bi + ai * br

        b_r, b_i, lb_r, lb_i = (base_ref[0, k] for k in range(4))
        for l in reversed(range(sub)):
            bcat[l * cw:(l + 1) * cw, 0:sw] = expand(b_r, 1, in_mask)
            bcat[l * cw:(l + 1) * cw, sw:2 * sw] = expand(b_i, 1, in_mask)
            b_r, b_i = cmul(b_r, b_i, lb_r, lb_i)
        b_now = bcat[(sub - 1) * cw:sub * cw, :]

        c_r, c_i, lc_r, lc_i = (base_ref[0, k] for k in range(4, 8))
        c_now = jnp.concatenate([expand(c_r, 0, out_mask), expand(-c_i, 0, out_mask)], axis=0)
        krev[(sub - 1) * cw:sub * cw, :] = _dot(b_now, c_now).astype(BF16)
        for l in range(sub):
            c_r, c_i = cmul(c_r, c_i, lc_r, lc_i)
            half = slice((l % 2) * cw, (l % 2 + 1) * cw)
            ccat[l // 2, 0:sw, half] = expand(c_r, 0, out_mask)
            ccat[l // 2, sw:2 * sw, half] = expand(-c_i, 0, out_mask)
            if l < sub - 1:
                krev[(sub - 2 - l) * cw:(sub - 1 - l) * cw, :] = _dot(b_now, ccat[l // 2, :, half]).astype(BF16)
        for p in range(sub // 2):
            off = p * (p + 1) * cw
            n0 = (2 * p + 1) * cw
            tpair[off:off + n0, 0:cw] = krev[(sub - 1 - 2 * p) * cw:sub * cw, :]
            tpair[off + n0:off + n0 + cw, 0:cw] = jnp.zeros((cw, cw), BF16)
            tpair[off:off + n0 + cw, cw:2 * cw] = krev[(sub - 2 - 2 * p) * cw:sub * cw, :]

    for l in range(sub):
        ucat[:, l * cw:(l + 1) * cw] = u_ref[0, pl.ds(l, r, stride=sub), :].astype(BF16)

    v = _dot(ucat[...], bcat[...])
    s_re, s_im = v[:, 0:sw], v[:, sw:2 * sw]
    rowi = lax.broadcasted_iota(jnp.int32, (r, sw), 0)
    shift, lvl = 1, 0
    while shift < r:
        lr = lam_ref[0, lvl, 0:1, :]
        li = lam_ref[0, lvl, 1:2, :]
        keep = rowi >= shift
        p_re = jnp.where(keep, pltpu.roll(s_re, shift, axis=0), 0.0)
        p_im = jnp.where(keep, pltpu.roll(s_im, shift, axis=0), 0.0)
        s_re, s_im = s_re + lr * p_re - li * p_im, s_im + lr * p_im + li * p_re
        shift, lvl = shift * 2, lvl + 1
    first = rowi >= 1
    x_re = jnp.where(first, pltpu.roll(s_re, 1, axis=0), 0.0)
    x_im = jnp.where(first, pltpu.roll(s_im, 1, axis=0), 0.0)
    xb = jnp.concatenate([x_re, x_im], axis=1).astype(BF16)

    for p in range(sub // 2):
        off = p * (p + 1) * cw
        n_in = (2 * p + 2) * cw
        y2 = _dot(ucat[:, 0:n_in], tpair[off:off + n_in, :]) + _dot(xb, ccat[p])
        for l in (2 * p, 2 * p + 1):
            y = y2[:, (l % 2) * cw:(l % 2 + 1) * cw]
            ul = u_ref[0, pl.ds(l, r, stride=sub), :]
            o_ref[0, pl.ds(l, r, stride=sub), :] = _gelu_tanh(y + d_ref[...] * ul)


def s5_scan_gelu(u, ops, d_skip, batch, seq):
    base, lam_lv = ops
    nt, n_base, cw, pair = base.shape
    t = u.shape[1]
    d = nt * cw
    sub = S5_SUB
    sw = S5_TILE_GROUPS * S5_STATE
    n_lvl = lam_lv.shape[1]
    return pl.pallas_call(
        _s5_kernel,
        grid=(nt, batch),
        in_specs=[pl.BlockSpec((1, seq, cw), lambda c, b: (c, b, 0)),
                  pl.BlockSpec((1, n_base, cw, pair), lambda c, b: (c, 0, 0, 0)),
                  pl.BlockSpec((1, n_lvl, 2, sw), lambda c, b: (c, 0, 0, 0)),
                  pl.BlockSpec((1, cw), lambda c, b: (0, c))],
        out_specs=pl.BlockSpec((1, seq, cw), lambda c, b: (c, b, 0)),
        out_shape=jax.ShapeDtypeStruct((nt, t, cw), F32),
        scratch_shapes=[pltpu.VMEM((seq // sub, sub * cw), BF16),
                        pltpu.VMEM((sub * cw, 2 * sw), BF16),
                        pltpu.VMEM((sub // 2, 2 * sw, 2 * cw), BF16),
                        pltpu.VMEM((sub * cw, cw), BF16),
                        pltpu.VMEM(((sub // 2) * (sub // 2 + 1) * cw, 2 * cw), BF16)],
        compiler_params=_params("parallel", "arbitrary"),
        name="s5_scan",
    )(u, base, lam_lv, d_skip.reshape(1, d))


def _glu_kernel(g_ref, wa_ref, wb_ref, x_ref, mod_ref, o_ref):
    g = jnp.concatenate([g_ref[c] for c in range(g_ref.shape[0])], axis=1).astype(BF16)
    mix = _dot(g, wa_ref[...]) * jax.nn.sigmoid(_dot(g, wb_ref[...]))
    o_ref[...] = x_ref[...] + mod_ref[0][2:3] * mix


def glu_out(g, w_a, w_b, x, mod, rows_per_batch, tm=512, tn=1024):
    t, d = x.shape
    tm = _tile(rows_per_batch, tm)
    tn = min(tn, d)
    tpb = rows_per_batch // tm
    return pl.pallas_call(
        _glu_kernel,
        grid=(d // tn, t // tm),
        in_specs=[pl.BlockSpec((g.shape[0], tm, g.shape[2]), lambda j, i: (0, i, 0)),
                  pl.BlockSpec((d, tn), lambda j, i: (0, j)),
                  pl.BlockSpec((d, tn), lambda j, i: (0, j)),
                  pl.BlockSpec((tm, tn), lambda j, i: (i, j)),
                  pl.BlockSpec((1, N_MOD, tn), lambda j, i: (i // tpb, 0, j))],
        out_specs=pl.BlockSpec((tm, tn), lambda j, i: (i, j)),
        out_shape=jax.ShapeDtypeStruct((t, d), F32),
        compiler_params=_params("parallel", "parallel"),
        name="glu_out",
    )(g, w_a, w_b, x, mod)


def _router_kernel(x_ref, g_ref, mod_ref, rw_ref, rb_ref, h_ref, r_ref):
    m = mod_ref[0]
    h = _norm_mod(x_ref[...], g_ref[...], m[3:4], m[4:5])
    h_ref[...] = h.astype(BF16)
    logits = _dot_split(h, rw_ref[...]) + rb_ref[...]
    lane = lax.broadcasted_iota(jnp.int32, logits.shape, 1)
    logits = jnp.where(lane < N_EXPERTS, logits, NEG_INF)
    m1 = jnp.max(logits, axis=1, keepdims=True)
    i1 = jnp.min(jnp.where(logits == m1, lane, LANES), axis=1, keepdims=True)
    rest = jnp.where(lane == i1, NEG_INF, logits)
    m2 = jnp.max(rest, axis=1, keepdims=True)
    i2 = jnp.min(jnp.where(rest == m2, lane, LANES), axis=1, keepdims=True)
    e2 = jnp.exp(m2 - m1)
    g1 = 1.0 / (1.0 + e2)
    g2 = e2 / (1.0 + e2)
    r_ref[...] = jnp.where(lane == 0, i1.astype(F32),
                           jnp.where(lane == 1, i2.astype(F32),
                                     jnp.where(lane == 2, g1, jnp.where(lane == 3, g2, 0.0))))


def moe_router(x, gain, mod, router_w, router_b, rows_per_batch, tm=512):
    t, d = x.shape
    e = router_w.shape[1]
    tm = _tile(rows_per_batch, tm)
    tpb = rows_per_batch // tm
    rw = _split_cols(jnp.pad(router_w.astype(F32), ((0, 0), (0, LANES - e))))
    rb = jnp.pad(router_b.astype(F32), (0, LANES - e)).reshape(1, LANES)
    return pl.pallas_call(
        _router_kernel,
        grid=(t // tm,),
        in_specs=[pl.BlockSpec((tm, d), lambda i: (i, 0)),
                  pl.BlockSpec((1, d), lambda i: (0, 0)),
                  pl.BlockSpec((1, N_MOD, d), lambda i: (i // tpb, 0, 0)),
                  pl.BlockSpec((d, 2 * LANES), lambda i: (0, 0)),
                  pl.BlockSpec((1, LANES), lambda i: (0, 0))],
        out_specs=[pl.BlockSpec((tm, d), lambda i: (i, 0)),
                   pl.BlockSpec((tm, LANES), lambda i: (i, 0))],
        out_shape=[jax.ShapeDtypeStruct((t, d), BF16), jax.ShapeDtypeStruct((t, LANES), F32)],
        compiler_params=_params("parallel"),
        name="moe_router",
    )(x, gain.reshape(1, d), mod, rw, rb)


def moe_dispatch(top_e, rows, sub_rows):
    t = top_e.shape[0]
    n_assign = t * TOP_K
    n_blocks = -(-n_assign // rows) + N_EXPERTS
    e_flat = top_e.reshape(-1)
    onehot = (e_flat[:, None] == jnp.arange(N_EXPERTS, dtype=jnp.int32)[None, :]).astype(jnp.int32)
    csum = jnp.cumsum(onehot, axis=0)
    rank = jnp.sum((csum - onehot) * onehot, axis=1)
    counts = csum[-1]
    padded = (counts + rows - 1) // rows * rows
    pad_end = jnp.cumsum(padded)
    pad_start = pad_end - padded
    dest = jnp.sum(onehot * pad_start[None, :], axis=1) + rank
    tok = jnp.arange(n_assign, dtype=jnp.int32) // TOP_K
    spread = jnp.arange(n_blocks * rows, dtype=jnp.int32) % t
    row_tok = spread.at[dest].set(tok)
    n_active = pad_end[-1] // rows
    blk = jnp.arange(n_blocks, dtype=jnp.int32)
    blk_start = jnp.minimum(blk, n_active - 1) * rows
    block_expert = jnp.minimum(jnp.sum((blk_start[:, None] >= pad_end[None, :]).astype(jnp.int32), axis=1),
                               N_EXPERTS - 1)
    real_rows = jnp.clip(pad_start[block_expert] + counts[block_expert] - blk_start, 0, rows)
    n_sub = jnp.where(blk < n_active, (real_rows + sub_rows - 1) // sub_rows, 0).astype(jnp.int32)
    return row_tok, dest.reshape(t, TOP_K), block_expert, n_sub


def _expert_kernel(be_ref, ns_ref, x_ref, wg_ref, wu_ref, wd_ref, o_ref, acc_ref, *, sub_rows):
    i = pl.program_id(0)
    f = pl.program_id(1)
    rows = x_ref.shape[0]

    @pl.when(f == 0)
    def _():
        acc_ref[...] = jnp.zeros_like(acc_ref)

    def run(n_rows):
        x = x_ref[0:n_rows, :]
        act = (_silu(_dot(x, wg_ref[0].astype(BF16))) * _dot(x, wu_ref[0].astype(BF16))).astype(BF16)
        acc_ref[0:n_rows, :] += _dot(act, wd_ref[0].astype(BF16))

    for s in range(1, rows // sub_rows + 1):
        pl.when(ns_ref[i] == s)(functools.partial(run, s * sub_rows))

    @pl.when(f == pl.num_programs(1) - 1)
    def _():
        o_ref[...] = acc_ref[...].astype(BF16)


def moe_experts(xg, block_expert, n_sub, w_gate, w_up, w_down, layer, rows, sub_rows, tf=512):
    r_tot, d = xg.shape
    f_dim = w_gate.shape[2]
    tf = min(tf, f_dim)
    nf = f_dim // tf
    n_blocks = r_tot // rows
    e0 = layer * N_EXPERTS

    def f_idx(i, f, ns):
        return jnp.where(ns[i] > 0, f, nf - 1)

    grid_spec = pltpu.PrefetchScalarGridSpec(
        num_scalar_prefetch=2,
        grid=(n_blocks, nf),
        in_specs=[pl.BlockSpec((rows, d), lambda i, f, be, ns: (i, 0), pipeline_mode=pl.Buffered(1)),
                  pl.BlockSpec((1, d, tf), lambda i, f, be, ns: (e0 + be[i], 0, f_idx(i, f, ns))),
                  pl.BlockSpec((1, d, tf), lambda i, f, be, ns: (e0 + be[i], 0, f_idx(i, f, ns))),
                  pl.BlockSpec((1, tf, d), lambda i, f, be, ns: (e0 + be[i], f_idx(i, f, ns), 0))],
        out_specs=pl.BlockSpec((rows, d), lambda i, f, be, ns: (i, 0)),
        scratch_shapes=[pltpu.VMEM((rows, d), F32)],
    )
    return pl.pallas_call(
        functools.partial(_expert_kernel, sub_rows=sub_rows),
        grid_spec=grid_spec,
        out_shape=jax.ShapeDtypeStruct((r_tot, d), BF16),
        compiler_params=_params("arbitrary", "arbitrary"),
        name="moe_experts",
    )(block_expert, n_sub, xg, w_gate, w_up, w_down)


def _combine_kernel(x_ref, y1_ref, y2_ref, r_ref, mod_ref, gf_ref, o_ref, *, final_norm):
    r = r_ref[...]
    ff = r[:, 2:3] * y1_ref[...].astype(F32) + r[:, 3:4] * y2_ref[...].astype(F32)
    x_new = x_ref[...] + mod_ref[0][5:6] * ff
    if final_norm:
        ms = jnp.mean(x_new * x_new, axis=-1, keepdims=True)
        x_new = x_new * lax.rsqrt(ms + EPS) * gf_ref[...]
    o_ref[...] = x_new


def moe_combine(x, y1, y2, route, mod, g_final, rows_per_batch, final_norm, tm=512):
    t, d = x.shape
    tm = _tile(rows_per_batch, tm)
    tpb = rows_per_batch // tm
    row = lambda w: pl.BlockSpec((tm, w), lambda i: (i, 0))
    return pl.pallas_call(
        functools.partial(_combine_kernel, final_norm=final_norm),
        grid=(t // tm,),
        in_specs=[row(d), row(d), row(d), row(LANES),
                  pl.BlockSpec((1, N_MOD, d), lambda i: (i // tpb, 0, 0)),
                  pl.BlockSpec((1, d), lambda i: (0, 0))],
        out_specs=row(d),
        out_shape=jax.ShapeDtypeStruct((t, d), F32),
        compiler_params=_params("parallel"),
        name="moe_combine",
    )(x, y1, y2, route, mod, g_final.reshape(1, d))


def kernel(x, c, w_ada, b_ada, g_mix, g_ffn, g_final, w_in, conv_w, b_igate, b_fgate, mh_gain, w_out,
           ffn_w_gate, ffn_w_up, ffn_w_down, s5_lam_re, s5_lam_im, s5_log_step, s5_b_re, s5_b_im,
           s5_c_re, s5_c_im, s5_d, glu_w_a, glu_w_b, router_w, router_b, exp_w_gate, exp_w_up, exp_w_down):
    batch, seq, d = x.shape
    depth = w_ada.shape[0]
    t = batch * seq
    m_width = mh_gain.shape[1]
    heads_b = (d - m_width) // MOBA_HEAD_DIM
    n_gate = 2 * MLSTM_HEADS

    mods = ada_mod(c, w_ada, b_ada)
    xs = x.reshape(t, d)
    for layer in range(depth):
        i = layer // 2
        mod = mods[layer]
        if layer % 2 == 0:
            w = w_in[i]
            g0 = 4 * m_width
            w_big = jnp.concatenate([w[:, :g0], w[:, g0 + n_gate:]], axis=1).astype(BF16)
            w_gates = _split_cols(jnp.pad(w[:, g0:g0 + n_gate], ((0, 0), (0, LANES - n_gate))))
            proj, gates = in_proj(xs, g_mix[layer], mod, w_big, w_gates, seq)
            hm = mlstm_mix(proj, gates, conv_w[i], b_igate[i], b_fgate[i], mh_gain[i], batch, seq)
            hb = moba_mix(proj, g0 // MOBA_HEAD_DIM, batch, seq, heads_b)
            xs = out_proj(hm, hb, w_out[i].astype(BF16), xs, mod, seq)
            f_pad = -ffn_w_gate.shape[2] % 512
            wg = jnp.pad(ffn_w_gate[i], ((0, 0), (0, f_pad))).astype(BF16)
            wu = jnp.pad(ffn_w_up[i], ((0, 0), (0, f_pad))).astype(BF16)
            wd = jnp.pad(ffn_w_down[i], ((0, f_pad), (0, 0))).astype(BF16)
            xs, u = ffn_swiglu(xs, g_ffn[layer], mod, wg, wu, wd, g_mix[layer + 1], mods[layer + 1], seq)
        else:
            ops = s5_operators(s5_lam_re[i], s5_lam_im[i], s5_log_step[i], s5_b_re[i], s5_b_im[i],
                               s5_c_re[i], s5_c_im[i], seq // S5_SUB)
            g = s5_scan_gelu(u, ops, s5_d[i], batch, seq)
            xs = glu_out(g, glu_w_a[i].astype(BF16), glu_w_b[i].astype(BF16), xs, mod, seq)
            h, route = moe_router(xs, g_ffn[layer], mod, router_w[i], router_b[i], seq)
            top_e = route[:, 0:TOP_K].astype(jnp.int32)
            row_tok, pos, block_expert, n_sub = moe_dispatch(top_e, MOE_ROWS, MOE_SUB_ROWS)
            stack = lambda w_: w_.reshape((-1,) + w_.shape[2:])
            y_rows = moe_experts(h[row_tok], block_expert, n_sub, stack(exp_w_gate), stack(exp_w_up),
                                 stack(exp_w_down), i, MOE_ROWS, MOE_SUB_ROWS)
            xs = moe_combine(xs, y_rows[pos[:, 0]], y_rows[pos[:, 1]], route, mod, g_final, seq,
                             final_norm=(layer == depth - 1))
    if depth % 2 == 1:
        raise NotImplementedError("layers come in (even, odd) pairs: the S5 input and the final norm are fused")
    return xs.reshape(batch, seq, d)
```

```python
import functools
import math

import jax
import jax.numpy as jnp
from jax import lax
from jax.experimental import pallas as pl
from jax.experimental.pallas import tpu as pltpu

F32 = jnp.float32
BF16 = jnp.bfloat16
HI = lax.Precision.HIGHEST
NEG_INF = float("-inf")

EPS = 1e-6
N_MOD = 6
MLSTM_HEADS = 4
MLSTM_CHUNK = 128
MLSTM_CONV = 4
MOBA_HEAD_DIM = 128
MOBA_BLOCK = 256
MOBA_TOPK = 3
S5_GROUP = 16
S5_STATE = 64
S5_SUB = 16
S5_TILE_GROUPS = 8
N_EXPERTS = 8
TOP_K = 2
MOE_ROWS = 1024
MOE_SUB_ROWS = 128
LANES = 128


def _tile(n, pref, align=8):
    for cand in range(min(pref, n), 0, -1):
        if n % cand == 0 and cand % align == 0:
            return cand
    raise ValueError(f"no {align}-aligned tile divides {n}")


def _params(*sem):
    return pltpu.CompilerParams(dimension_semantics=sem)


def _dot(a, b):
    return jnp.dot(a, b, preferred_element_type=F32)


def _dot_nt(a, b, precision=None):
    return lax.dot_general(a, b, (((1,), (1,)), ((), ())), precision=precision,
                           preferred_element_type=F32)


def _silu(x):
    return x * jax.nn.sigmoid(x)


def _split_cols(w):
    hi = w.astype(BF16)
    lo = (w - hi.astype(F32)).astype(BF16)
    return jnp.concatenate([hi, lo], axis=1)


def _dot_split(x, w_cat):
    hi = x.astype(BF16)
    lo = (x - hi.astype(F32)).astype(BF16)
    r = _dot(hi, w_cat) + _dot(lo, w_cat)
    n = w_cat.shape[1] // 2
    return r[:, :n] + r[:, n:]


def _norm_mod(x, gain, shift, scale):
    ms = jnp.mean(x * x, axis=-1, keepdims=True)
    y = x * lax.rsqrt(ms + EPS) * gain
    return y * (1.0 + scale) + shift


def _ada_kernel(c_ref, w_ref, b_ref, o_ref):
    cond = _silu(c_ref[...])
    bp = cond.shape[0]
    c_hi = cond.astype(BF16).astype(F32)
    lhs = jnp.concatenate([c_hi, cond - c_hi], axis=0).astype(BF16)
    w = w_ref[0]
    w_hi = w.astype(BF16)
    w_lo = (w - w_hi.astype(F32)).astype(BF16)
    r = _dot(lhs, w_hi) + _dot(lhs, w_lo)
    o_ref[0] = r[0:bp] + r[bp:2 * bp] + b_ref[0]


def ada_mod(c, w_ada, b_ada, tn=2048):
    depth, d, n = w_ada.shape
    b = c.shape[0]
    bp = 8
    cp = jnp.pad(c, ((0, bp - b), (0, 0)))
    out = pl.pallas_call(
        _ada_kernel,
        grid=(depth, n // tn),
        in_specs=[pl.BlockSpec((bp, d), lambda l, j: (0, 0)),
                  pl.BlockSpec((1, d, tn), lambda l, j: (l, 0, j)),
                  pl.BlockSpec((1, 1, tn), lambda l, j: (l, 0, j))],
        out_specs=pl.BlockSpec((1, bp, tn), lambda l, j: (l, 0, j)),
        out_shape=jax.ShapeDtypeStruct((depth, bp, n), F32),
        compiler_params=_params("parallel", "parallel"),
        name="ada_mod",
    )(cp, w_ada, b_ada.reshape(depth, 1, n))
    return out[:, :b].reshape(depth, b, N_MOD, d)


def _inproj_kernel(x_ref, g_ref, mod_ref, w_ref, wg_ref, o_ref, og_ref, h_ref):
    @pl.when(pl.program_id(1) == 0)
    def _():
        m = mod_ref[0]
        h = _norm_mod(x_ref[...], g_ref[...], m[0:1], m[1:2])
        h_ref[...] = h.astype(BF16)
        og_ref[...] = _dot_split(h, wg_ref[...])

    o_ref[...] = _dot(h_ref[...], w_ref[...]).astype(BF16)


def in_proj(x, gain, mod, w_big, w_gates, rows_per_batch, tm=1024, tn=1024):
    t, d = x.shape
    n = w_big.shape[1]
    tm = _tile(rows_per_batch, tm)
    tn = min(tn, n)
    tpb = rows_per_batch // tm
    return pl.pallas_call(
        _inproj_kernel,
        grid=(t // tm, n // tn),
        in_specs=[pl.BlockSpec((tm, d), lambda i, j: (i, 0)),
                  pl.BlockSpec((1, d), lambda i, j: (0, 0)),
                  pl.BlockSpec((1, N_MOD, d), lambda i, j: (i // tpb, 0, 0)),
                  pl.BlockSpec((d, tn), lambda i, j: (0, j)),
                  pl.BlockSpec((d, 2 * LANES), lambda i, j: (0, 0))],
        out_specs=[pl.BlockSpec((tm, tn), lambda i, j: (i, j)),
                   pl.BlockSpec((tm, LANES), lambda i, j: (i, 0))],
        out_shape=[jax.ShapeDtypeStruct((t, n), BF16), jax.ShapeDtypeStruct((t, LANES), F32)],
        scratch_shapes=[pltpu.VMEM((tm, d), BF16)],
        compiler_params=_params("parallel", "arbitrary"),
        name="in_proj",
    )(x, gain.reshape(1, d), mod, w_big, w_gates)


def _mlstm_kernel(bias_ref, q_ref, k_ref, v_ref, o_ref, gi_ref, gf_ref, cwq_ref, cwk_ref, gain_ref,
                  out_ref, qbuf, kbuf, c_st, n_st, m_st):
    chunk = pl.program_id(1)
    L = q_ref.shape[0]
    heads, dh = c_st.shape[0], c_st.shape[1]
    taps = cwq_ref.shape[0]
    halo = 8

    @pl.when(chunk == 0)
    def _():
        qbuf[0:halo] = jnp.zeros((halo, qbuf.shape[1]), F32)
        kbuf[0:halo] = jnp.zeros((halo, kbuf.shape[1]), F32)
        c_st[...] = jnp.zeros_like(c_st)
        n_st[...] = jnp.zeros_like(n_st)
        m_st[...] = jnp.full(m_st.shape, -1e30, F32)

    def conv_silu(src_ref, buf, w_ref):
        buf[halo:halo + L] = src_ref[...].astype(F32)
        w = w_ref[...]
        acc = buf[halo:halo + L] * w[taps - 1:taps]
        for d in range(1, taps):
            acc = acc + buf[pl.ds(halo - d, L), :] * w[taps - 1 - d:taps - d]
        buf[0:halo] = buf[L:L + halo]
        return _silu(acc)

    q_all = conv_silu(q_ref, qbuf, cwq_ref)
    k_all = conv_silu(k_ref, kbuf, cwk_ref) * (dh ** -0.5)

    row = lax.broadcasted_iota(jnp.int32, (L, L), 0)
    col = lax.broadcasted_iota(jnp.int32, (L, L), 1)
    eye = row == col

    def to_col(x_row):
        return jnp.sum(jnp.where(eye, jnp.broadcast_to(x_row, (L, L)), 0.0), axis=1, keepdims=True)

    fz = jnp.concatenate([gf_ref[0, h, 0] + bias_ref[1, h] for h in range(heads)]
                         + [jnp.zeros((8 - heads, L), F32)], axis=0)
    lf_rows = jnp.minimum(fz, 0.0) - jnp.log(1.0 + jnp.exp(-jnp.abs(fz)))
    g_rows = jnp.dot(lf_rows, (row <= col).astype(F32), precision=HI,
                     preferred_element_type=F32)

    for h in range(heads):
        sl = slice(h * dh, (h + 1) * dh)
        q, k, vb = q_all[:, sl], k_all[:, sl], v_ref[:, sl]
        ig_row = gi_ref[0, h, 0] + bias_ref[0, h]
        g_row = g_rows[h:h + 1]
        g_col = to_col(g_row)
        b_row = ig_row - g_row
        d_mat = jnp.where(col <= row, g_col + b_row, NEG_INF)
        m_prev = m_st[h]
        m_inter = g_col + m_prev
        m_t = jnp.maximum(m_inter, jnp.max(d_mat, axis=1, keepdims=True))
        qb = q.astype(BF16)
        kb = k.astype(BF16)
        s = _dot_nt(qb, kb) * jnp.exp(d_mat - m_t)
        decay = jnp.exp(m_inter - m_t)
        num = _dot(s.astype(BF16), vb) + decay * _dot(qb, c_st[h].astype(BF16))
        den = jnp.sum(s, axis=1, keepdims=True) + decay * jnp.sum(q * n_st[h], axis=1, keepdims=True)
        hh = num / jnp.maximum(jnp.abs(den), jnp.exp(-m_t))

        g_last = g_row[:, L - 1:L]
        a_row = g_last + b_row
        m_new = jnp.maximum(g_last + m_prev, jnp.max(a_row, axis=1, keepdims=True))
        w_col = to_col(jnp.exp(a_row - m_new))
        carry = jnp.exp(g_last + m_prev - m_new)
        kw = k * w_col
        c_st[h] = carry * c_st[h] + _dot(kw.T.astype(BF16), vb)
        n_st[h] = carry * n_st[h] + jnp.sum(kw, axis=0, keepdims=True)
        m_st[h] = m_new

        hn = hh * lax.rsqrt(jnp.mean(hh * hh, axis=1, keepdims=True) + EPS) * gain_ref[:, sl]
        out_ref[:, sl] = (hn * jax.nn.sigmoid(o_ref[:, sl].astype(F32))).astype(BF16)


def mlstm_mix(proj, gates, conv_w, b_igate, b_fgate, mh_gain, batch, seq):
    heads, L = MLSTM_HEADS, MLSTM_CHUNK
    width = mh_gain.shape[0]
    dh = width // heads
    nc = seq // L
    t = batch * seq

    def rows(a):
        return a.reshape(batch, nc, L, heads).transpose(0, 3, 1, 2).reshape(batch, heads, nc, 1, L)

    gi = rows(gates[:, 0:heads])
    gf = rows(gates[:, heads:2 * heads])
    bias = jnp.stack([b_igate, b_fgate]).astype(F32)
    blk = lambda off: pl.BlockSpec((L, width), lambda b, c: (b * nc + c, off))
    gspec = pl.BlockSpec((1, heads, 1, 1, L), lambda b, c: (b, 0, c, 0, 0))
    return pl.pallas_call(
        _mlstm_kernel,
        grid=(batch, nc),
        in_specs=[pl.BlockSpec(memory_space=pltpu.SMEM),
                  blk(0), blk(1), blk(2), blk(3), gspec, gspec,
                  pl.BlockSpec((MLSTM_CONV, width), lambda b, c: (0, 0)),
                  pl.BlockSpec((MLSTM_CONV, width), lambda b, c: (0, 1)),
                  pl.BlockSpec((1, width), lambda b, c: (0, 0))],
        out_specs=pl.BlockSpec((L, width), lambda b, c: (b * nc + c, 0)),
        out_shape=jax.ShapeDtypeStruct((t, width), BF16),
        scratch_shapes=[pltpu.VMEM((L + 8, width), F32), pltpu.VMEM((L + 8, width), F32),
                        pltpu.VMEM((heads, dh, dh), F32), pltpu.VMEM((heads, 1, dh), F32),
                        pltpu.VMEM((heads, 1, 1), F32)],
        compiler_params=_params("parallel", "arbitrary"),
        name="mlstm",
    )(bias, proj, proj, proj, proj, gi, gf, conv_w, conv_w, mh_gain.reshape(1, width))


MOBA_MASK_BIAS = -1e9
MOBA_WIDTH_STEP = 2
MOBA_CHUNK_BLOCKS = 2
MOBA_HEAD_GROUP = 2


def _moba_kernel(q_ref, k_ref, v_ref, o_ref, kmean_ref, kaug_ref, qaug_ref, s_ref):
    j = pl.program_id(1)
    blk = o_ref.shape[0]
    n_h, nbp, dh = kmean_ref.shape
    seq = k_ref.shape[0]
    nb = seq // blk
    exp_scale = dh ** -0.5 * math.log2(math.e)

    @pl.when(j == 0)
    def _():
        key_blk = lax.broadcasted_iota(jnp.int32, (seq, LANES), 0) // blk
        lane = lax.broadcasted_iota(jnp.int32, (seq, LANES), 1)
        block_onehot = jnp.where(key_blk == lane, 1.0, 0.0).astype(BF16)
        kmean_ref[...] = jnp.zeros_like(kmean_ref)
        for h in range(n_h):
            hs = slice(h * dh, (h + 1) * dh)
            for b in range(nb):
                kmean_ref[h, b:b + 1, :] = jnp.mean(k_ref[b * blk:(b + 1) * blk, hs].astype(F32), axis=0,
                                                    keepdims=True)
            kaug_ref[h, :, 0:dh] = k_ref[:, hs]
            kaug_ref[h, :, dh:dh + LANES] = block_onehot

            q_all = q_ref[:, hs]
            qaug_ref[h, :, 0:dh] = q_all
            gate_t = _dot_nt(kmean_ref[h], q_all.astype(F32), precision=HI)
            blk_id = lax.broadcasted_iota(jnp.int32, gate_t.shape, 0)
            q_blk = lax.broadcasted_iota(jnp.int32, gate_t.shape, 1) // blk
            valid = blk_id < q_blk
            sc = jnp.where(valid, gate_t, NEG_INF)
            beaten = jnp.zeros(gate_t.shape, F32)
            for b2 in range(nb):
                other = sc[b2:b2 + 1, :]
                wins = (other > sc) | ((other == sc) & (b2 < blk_id))
                beaten = beaten + wins.astype(F32)
            chosen = valid & (beaten < MOBA_TOPK)
            bias_t = jnp.where(chosen, 0.0, MOBA_MASK_BIAS)
            pad = jnp.zeros((LANES - nbp, blk), F32)
            for b in range(nb):
                piece = jnp.concatenate([bias_t[:, b * blk:(b + 1) * blk], pad], axis=0).T
                qaug_ref[h, b * blk:(b + 1) * blk, dh:dh + LANES] = piece.astype(BF16)

    start = pl.multiple_of(j * blk, blk)
    row = lax.broadcasted_iota(jnp.int32, (blk, blk), 0)
    col = lax.broadcasted_iota(jnp.int32, (blk, blk), 1)

    own = []
    for h in range(n_h):
        hs = slice(h * dh, (h + 1) * dh)
        q_aug = qaug_ref[h, pl.ds(start, blk), :]
        s_own = jnp.where(col <= row, _dot_nt(q_aug[:, 0:dh], k_ref[pl.ds(start, blk), hs]), NEG_INF)
        own.append((q_aug, s_own, jnp.max(s_own, axis=1, keepdims=True), v_ref[pl.ds(start, blk), hs]))

    @pl.when(j == 0)
    def _():
        for h, (_, s_own, m_own, v_own) in enumerate(own):
            p = jnp.exp2((s_own - m_own) * exp_scale)
            out = _dot(p.astype(BF16), v_own) / jnp.sum(p, axis=1, keepdims=True)
            o_ref[:, h * dh:(h + 1) * dh] = out.astype(BF16)

    def attend(n_blocks):
        w = n_blocks * blk
        step = MOBA_CHUNK_BLOCKS * blk
        chunks = [(lo_, min(lo_ + step, w)) for lo_ in range(0, w, step)]
        maxes = []
        for h, (q_aug, _, m_own, _) in enumerate(own):
            mx = jnp.full((blk, LANES), NEG_INF, F32)
            for lo_, hi_ in chunks:
                s_c = _dot_nt(q_aug, kaug_ref[h, lo_:hi_, :])
                s_ref[h, :, lo_:hi_] = s_c
                for t_ in range((hi_ - lo_) // LANES):
                    mx = jnp.maximum(mx, s_c[:, t_ * LANES:(t_ + 1) * LANES])
            maxes.append(jnp.maximum(jnp.max(mx, axis=1, keepdims=True), m_own))
        for h, (_, s_own, _, v_own) in enumerate(own):
            hs = slice(h * dh, (h + 1) * dh)
            m = maxes[h]
            p_own = jnp.exp2((s_own - m) * exp_scale)
            acc = _dot(p_own.astype(BF16), v_own)
            lsum = p_own[:, 0:LANES]
            for t_ in range(1, blk // LANES):
                lsum = lsum + p_own[:, t_ * LANES:(t_ + 1) * LANES]
            for lo_, hi_ in chunks:
                p = jnp.exp2((s_ref[h, :, lo_:hi_] - m) * exp_scale)
                for t_ in range((hi_ - lo_) // LANES):
                    lsum = lsum + p[:, t_ * LANES:(t_ + 1) * LANES]
                acc = acc + _dot(p.astype(BF16), v_ref[lo_:hi_, hs])
            o_ref[:, hs] = (acc / jnp.sum(lsum, axis=1, keepdims=True)).astype(BF16)

    lo = 0
    for hi in list(range(MOBA_WIDTH_STEP, nb - 1, MOBA_WIDTH_STEP)) + [nb - 1]:
        pl.when((j > lo) & (j <= hi))(functools.partial(attend, hi))
        lo = hi


def moba_mix(proj, col0, batch, seq, heads):
    dh, blk, grp = MOBA_HEAD_DIM, MOBA_BLOCK, MOBA_HEAD_GROUP
    nb = seq // blk
    nbp = -(-nb // 8) * 8
    t = batch * seq
    n_grp = heads // grp
    c0 = col0 // grp
    wide = lambda off: pl.BlockSpec((seq, grp * dh), lambda g, j: (g // n_grp, c0 + off * n_grp + g % n_grp))
    return pl.pallas_call(
        _moba_kernel,
        grid=(batch * n_grp, nb),
        in_specs=[wide(0), wide(1), wide(2)],
        out_specs=pl.BlockSpec((blk, grp * dh), lambda g, j: ((g // n_grp) * nb + j, g % n_grp)),
        out_shape=jax.ShapeDtypeStruct((t, heads * dh), BF16),
        scratch_shapes=[pltpu.VMEM((grp, nbp, dh), F32), pltpu.VMEM((grp, seq, dh + LANES), BF16),
                        pltpu.VMEM((grp, seq, dh + LANES), BF16), pltpu.VMEM((grp, blk, seq), F32)],
        compiler_params=_params("parallel", "arbitrary"),
        name="moba",
    )(proj, proj, proj)


def _outproj_kernel(hm_ref, hb_ref, w1_ref, w2_ref, x_ref, mod_ref, o_ref):
    acc = _dot(hm_ref[...], w1_ref[...]) + _dot(hb_ref[...], w2_ref[...])
    o_ref[...] = x_ref[...] + mod_ref[0][2:3] * acc


def out_proj(hm, hb, w_out, x, mod, rows_per_batch, tm=512):
    t, d = x.shape
    k1, k2 = hm.shape[1], hb.shape[1]
    tm = _tile(rows_per_batch, tm)
    tpb = rows_per_batch // tm
    return pl.pallas_call(
        _outproj_kernel,
        grid=(t // tm,),
        in_specs=[pl.BlockSpec((tm, k1), lambda i: (i, 0)),
                  pl.BlockSpec((tm, k2), lambda i: (i, 0)),
                  pl.BlockSpec((k1, d), lambda i: (0, 0)),
                  pl.BlockSpec((k2, d), lambda i: (0, 0)),
                  pl.BlockSpec((tm, d), lambda i: (i, 0)),
                  pl.BlockSpec((1, N_MOD, d), lambda i: (i // tpb, 0, 0))],
        out_specs=pl.BlockSpec((tm, d), lambda i: (i, 0)),
        out_shape=jax.ShapeDtypeStruct((t, d), F32),
        compiler_params=_params("parallel"),
        name="out_proj",
    )(hm, hb, w_out[:k1], w_out[k1:], x, mod)


def _ffn_kernel(x_ref, g_ref, mod_ref, wg_ref, wu_ref, wd_ref, g2_ref, mod2_ref, o_ref, u_ref, h_ref, acc_ref):
    f = pl.program_id(1)

    @pl.when(f == 0)
    def _():
        m = mod_ref[0]
        h_ref[...] = _norm_mod(x_ref[...], g_ref[...], m[3:4], m[4:5]).astype(BF16)
        acc_ref[...] = jnp.zeros_like(acc_ref)

    h = h_ref[...]
    act = (_silu(_dot(h, wg_ref[...])) * _dot(h, wu_ref[...])).astype(BF16)
    acc_ref[...] += _dot(act, wd_ref[...])

    @pl.when(f == pl.num_programs(1) - 1)
    def _():
        x_new = x_ref[...] + mod_ref[0][5:6] * acc_ref[...]
        o_ref[...] = x_new
        m2 = mod2_ref[0]
        u = _norm_mod(x_new, g2_ref[...], m2[0:1], m2[1:2])
        for c in range(u_ref.shape[0]):
            u_ref[c] = u[:, c * LANES:(c + 1) * LANES]


def ffn_swiglu(x, gain, mod, w_gate, w_up, w_down, next_gain, next_mod, rows_per_batch, tm=512, tf=512):
    t, d = x.shape
    f_dim = w_gate.shape[1]
    tm = _tile(rows_per_batch, tm)
    tf = min(tf, f_dim)
    tpb = rows_per_batch // tm
    return pl.pallas_call(
        _ffn_kernel,
        grid=(t // tm, f_dim // tf),
        in_specs=[pl.BlockSpec((tm, d), lambda i, f: (i, 0)),
                  pl.BlockSpec((1, d), lambda i, f: (0, 0)),
                  pl.BlockSpec((1, N_MOD, d), lambda i, f: (i // tpb, 0, 0)),
                  pl.BlockSpec((d, tf), lambda i, f: (0, f)),
                  pl.BlockSpec((d, tf), lambda i, f: (0, f)),
                  pl.BlockSpec((tf, d), lambda i, f: (f, 0)),
                  pl.BlockSpec((1, d), lambda i, f: (0, 0)),
                  pl.BlockSpec((1, N_MOD, d), lambda i, f: (i // tpb, 0, 0))],
        out_specs=[pl.BlockSpec((tm, d), lambda i, f: (i, 0)),
                   pl.BlockSpec((d // LANES, tm, LANES), lambda i, f: (0, i, 0))],
        out_shape=[jax.ShapeDtypeStruct((t, d), F32), jax.ShapeDtypeStruct((d // LANES, t, LANES), F32)],
        scratch_shapes=[pltpu.VMEM((tm, d), BF16), pltpu.VMEM((tm, d), F32)],
        compiler_params=_params("parallel", "arbitrary"),
        name="ffn_swiglu",
    )(x, gain.reshape(1, d), mod, w_gate, w_up, w_down, next_gain.reshape(1, d), next_mod)


def s5_operators(lam_re, lam_im, log_step, b_re, b_im, c_re, c_im, n_chunks):
    g_all, p = lam_re.shape
    n = b_re.shape[-1]
    sub, tg = S5_SUB, S5_TILE_GROUPS
    nt = g_all // tg
    lam = lax.complex(lam_re.astype(F32), lam_im.astype(F32))
    lam_dt = lam * jnp.exp(log_step.astype(F32))[:, None]
    lam_bar = jnp.exp(lam_dt)
    b_bar = ((lam_bar - 1.0) / lam)[:, :, None] * lax.complex(b_re.astype(F32), b_im.astype(F32))
    c_mat = lax.complex(c_re.astype(F32), c_im.astype(F32))
    par = ((jnp.arange(tg) % 2)[:, None] == jnp.arange(2)[None, :]).astype(F32)
    ones_n = jnp.ones((n,), F32)

    def lay_in(a):
        z = jnp.einsum('qgpn,gr->qgnrp', a.reshape(nt, tg, p, n), par)
        return z.reshape(nt, tg * n, 2 * p)

    def lay_out(a):
        z = jnp.einsum('qgmp,gr->qrpgm', a.reshape(nt, tg, n, p), par)
        return z.reshape(nt, 2 * p, tg * n)

    lam_g = lam_bar.reshape(nt, tg, p)
    lam_in = jnp.einsum('qgp,n,r->qgnrp', lam_g, ones_n.astype(lam_g.dtype),
                        jnp.ones((2,), lam_g.dtype)).reshape(nt, tg * n, 2 * p)
    lam_out = jnp.einsum('qgp,m,r->qrpgm', lam_g, ones_n.astype(lam_g.dtype),
                         jnp.ones((2,), lam_g.dtype)).reshape(nt, 2 * p, tg * n)
    base = jnp.stack([lay_in(b_bar.real), lay_in(b_bar.imag), lam_in.real, lam_in.imag,
                      lay_out(c_mat.real), lay_out(c_mat.imag), lam_out.real, lam_out.imag], axis=1)

    n_lvl = max(1, (n_chunks - 1).bit_length())
    lv = jnp.exp(lam_dt[None] * (sub * 2.0 ** jnp.arange(n_lvl, dtype=F32))[:, None, None])
    lv = lv.reshape(n_lvl, nt, tg * p)
    lam_lv = jnp.stack([lv.real, lv.imag], axis=2).transpose(1, 0, 2, 3)
    return base, lam_lv


def _gelu_tanh(x):
    return 0.5 * x * (1.0 + jnp.tanh(0.7978845608028654 * (x + 0.044715 * (x * x * x))))


def _s5_kernel(u_ref, base_ref, lam_ref, d_ref, o_ref, ucat, bcat, ccat, krev, tpair):
    sub = 2 * ccat.shape[0]
    cw = u_ref.shape[2]
    r = u_ref.shape[1] // sub
    sw = bcat.shape[1] // 2
    pair = base_ref.shape[3]

    @pl.when(pl.program_id(1) == 0)
    def _():
        rb = lax.broadcasted_iota(jnp.int32, (cw, sw), 0) // (2 * S5_GROUP)
        cb = lax.broadcasted_iota(jnp.int32, (cw, sw), 1) // pair
        in_mask = rb == cb
        rc = lax.broadcasted_iota(jnp.int32, (sw, cw), 0) // pair
        cc = lax.broadcasted_iota(jnp.int32, (sw, cw), 1) // (2 * S5_GROUP)
        out_mask = rc == cc
        reps = sw // pair

        def expand(x, axis, mask):
            return jnp.where(mask, jnp.concatenate([x] * reps, axis=axis), 0.0).astype(BF16)

        def cmul(ar, ai, br, bi):
            return ar * br - ai * bi, ar * bi + ai * br

        b_r, b_i, lb_r, lb_i = (base_ref[0, k] for k in range(4))
        for l in reversed(range(sub)):
            bcat[l * cw:(l + 1) * cw, 0:sw] = expand(b_r, 1, in_mask)
            bcat[l * cw:(l + 1) * cw, sw:2 * sw] = expand(b_i, 1, in_mask)
            b_r, b_i = cmul(b_r, b_i, lb_r, lb_i)
        b_now = bcat[(sub - 1) * cw:sub * cw, :]

        c_r, c_i, lc_r, lc_i = (base_ref[0, k] for k in range(4, 8))
        c_now = jnp.concatenate([expand(c_r, 0, out_mask), expand(-c_i, 0, out_mask)], axis=0)
        krev[(sub - 1) * cw:sub * cw, :] = _dot(b_now, c_now).astype(BF16)
        for l in range(sub):
            c_r, c_i = cmul(c_r, c_i, lc_r, lc_i)
            half = slice((l % 2) * cw, (l % 2 + 1) * cw)
            ccat[l // 2, 0:sw, half] = expand(c_r, 0, out_mask)
            ccat[l // 2, sw:2 * sw, half] = expand(-c_i, 0, out_mask)
            if l < sub - 1:
                krev[(sub - 2 - l) * cw:(sub - 1 - l) * cw, :] = _dot(b_now, ccat[l // 2, :, half]).astype(BF16)
        for p in range(sub // 2):
            off = p * (p + 1) * cw
            n0 = (2 * p + 1) * cw
            tpair[off:off + n0, 0:cw] = krev[(sub - 1 - 2 * p) * cw:sub * cw, :]
            tpair[off + n0:off + n0 + cw, 0:cw] = jnp.zeros((cw, cw), BF16)
            tpair[off:off + n0 + cw, cw:2 * cw] = krev[(sub - 2 - 2 * p) * cw:sub * cw, :]

    for l in range(sub):
        ucat[:, l * cw:(l + 1) * cw] = u_ref[0, pl.ds(l, r, stride=sub), :].astype(BF16)

    v = _dot(ucat[...], bcat[...])
    s_re, s_im = v[:, 0:sw], v[:, sw:2 * sw]
    rowi = lax.broadcasted_iota(jnp.int32, (r, sw), 0)
    shift, lvl = 1, 0
    while shift < r:
        lr = lam_ref[0, lvl, 0:1, :]
        li = lam_ref[0, lvl, 1:2, :]
        keep = rowi >= shift
        p_re = jnp.where(keep, pltpu.roll(s_re, shift, axis=0), 0.0)
        p_im = jnp.where(keep, pltpu.roll(s_im, shift, axis=0), 0.0)
        s_re, s_im = s_re + lr * p_re - li * p_im, s_im + lr * p_im + li * p_re
        shift, lvl = shift * 2, lvl + 1
    first = rowi >= 1
    x_re = jnp.where(first, pltpu.roll(s_re, 1, axis=0), 0.0)
    x_im = jnp.where(first, pltpu.roll(s_im, 1, axis=0), 0.0)
    xb = jnp.concatenate([x_re, x_im], axis=1).astype(BF16)

    for p in range(sub // 2):
        off = p * (p + 1) * cw
        n_in = (2 * p + 2) * cw
        y2 = _dot(ucat[:, 0:n_in], tpair[off:off + n_in, :]) + _dot(xb, ccat[p])
        for l in (2 * p, 2 * p + 1):
            y = y2[:, (l % 2) * cw:(l % 2 + 1) * cw]
            ul = u_ref[0, pl.ds(l, r, stride=sub), :]
            o_ref[0, pl.ds(l, r, stride=sub), :] = _gelu_tanh(y + d_ref[...] * ul)


def s5_scan_gelu(u, ops, d_skip, batch, seq):
    base, lam_lv = ops
    nt, n_base, cw, pair = base.shape
    t = u.shape[1]
    d = nt * cw
    sub = S5_SUB
    sw = S5_TILE_GROUPS * S5_STATE
    n_lvl = lam_lv.shape[1]
    return pl.pallas_call(
        _s5_kernel,
        grid=(nt, batch),
        in_specs=[pl.BlockSpec((1, seq, cw), lambda c, b: (c, b, 0)),
                  pl.BlockSpec((1, n_base, cw, pair), lambda c, b: (c, 0, 0, 0)),
                  pl.BlockSpec((1, n_lvl, 2, sw), lambda c, b: (c, 0, 0, 0)),
                  pl.BlockSpec((1, cw), lambda c, b: (0, c))],
        out_specs=pl.BlockSpec((1, seq, cw), lambda c, b: (c, b, 0)),
        out_shape=jax.ShapeDtypeStruct((nt, t, cw), F32),
        scratch_shapes=[pltpu.VMEM((seq // sub, sub * cw), BF16),
                        pltpu.VMEM((sub * cw, 2 * sw), BF16),
                        pltpu.VMEM((sub // 2, 2 * sw, 2 * cw), BF16),
                        pltpu.VMEM((sub * cw, cw), BF16),
                        pltpu.VMEM(((sub // 2) * (sub // 2 + 1) * cw, 2 * cw), BF16)],
        compiler_params=_params("parallel", "arbitrary"),
        name="s5_scan",
    )(u, base, lam_lv, d_skip.reshape(1, d))


def _glu_kernel(g_ref, wa_ref, wb_ref, x_ref, mod_ref, o_ref):
    g = jnp.concatenate([g_ref[c] for c in range(g_ref.shape[0])], axis=1).astype(BF16)
    mix = _dot(g, wa_ref[...]) * jax.nn.sigmoid(_dot(g, wb_ref[...]))
    o_ref[...] = x_ref[...] + mod_ref[0][2:3] * mix


def glu_out(g, w_a, w_b, x, mod, rows_per_batch, tm=512, tn=1024):
    t, d = x.shape
    tm = _tile(rows_per_batch, tm)
    tn = min(tn, d)
    tpb = rows_per_batch // tm
    return pl.pallas_call(
        _glu_kernel,
        grid=(d // tn, t // tm),
        in_specs=[pl.BlockSpec((g.shape[0], tm, g.shape[2]), lambda j, i: (0, i, 0)),
                  pl.BlockSpec((d, tn), lambda j, i: (0, j)),
                  pl.BlockSpec((d, tn), lambda j, i: (0, j)),
                  pl.BlockSpec((tm, tn), lambda j, i: (i, j)),
                  pl.BlockSpec((1, N_MOD, tn), lambda j, i: (i // tpb, 0, j))],
        out_specs=pl.BlockSpec((tm, tn), lambda j, i: (i, j)),
        out_shape=jax.ShapeDtypeStruct((t, d), F32),
        compiler_params=_params("parallel", "parallel"),
        name="glu_out",
    )(g, w_a, w_b, x, mod)


def _router_kernel(x_ref, g_ref, mod_ref, rw_ref, rb_ref, h_ref, r_ref):
    m = mod_ref[0]
    h = _norm_mod(x_ref[...], g_ref[...], m[3:4], m[4:5])
    h_ref[...] = h.astype(BF16)
    logits = _dot_split(h, rw_ref[...]) + rb_ref[...]
    lane = lax.broadcasted_iota(jnp.int32, logits.shape, 1)
    logits = jnp.where(lane < N_EXPERTS, logits, NEG_INF)
    m1 = jnp.max(logits, axis=1, keepdims=True)
    i1 = jnp.min(jnp.where(logits == m1, lane, LANES), axis=1, keepdims=True)
    rest = jnp.where(lane == i1, NEG_INF, logits)
    m2 = jnp.max(rest, axis=1, keepdims=True)
    i2 = jnp.min(jnp.where(rest == m2, lane, LANES), axis=1, keepdims=True)
    e2 = jnp.exp(m2 - m1)
    g1 = 1.0 / (1.0 + e2)
    g2 = e2 / (1.0 + e2)
    r_ref[...] = jnp.where(lane == 0, i1.astype(F32),
                           jnp.where(lane == 1, i2.astype(F32),
                                     jnp.where(lane == 2, g1, jnp.where(lane == 3, g2, 0.0))))


def moe_router(x, gain, mod, router_w, router_b, rows_per_batch, tm=512):
    t, d = x.shape
    e = router_w.shape[1]
    tm = _tile(rows_per_batch, tm)
    tpb = rows_per_batch // tm
    rw = _split_cols(jnp.pad(router_w.astype(F32), ((0, 0), (0, LANES - e))))
    rb = jnp.pad(router_b.astype(F32), (0, LANES - e)).reshape(1, LANES)
    return pl.pallas_call(
        _router_kernel,
        grid=(t // tm,),
        in_specs=[pl.BlockSpec((tm, d), lambda i: (i, 0)),
                  pl.BlockSpec((1, d), lambda i: (0, 0)),
                  pl.BlockSpec((1, N_MOD, d), lambda i: (i // tpb, 0, 0)),
                  pl.BlockSpec((d, 2 * LANES), lambda i: (0, 0)),
                  pl.BlockSpec((1, LANES), lambda i: (0, 0))],
        out_specs=[pl.BlockSpec((tm, d), lambda i: (i, 0)),
                   pl.BlockSpec((tm, LANES), lambda i: (i, 0))],
        out_shape=[jax.ShapeDtypeStruct((t, d), BF16), jax.ShapeDtypeStruct((t, LANES), F32)],
        compiler_params=_params("parallel"),
        name="moe_router",
    )(x, gain.reshape(1, d), mod, rw, rb)


def moe_dispatch(top_e, rows, sub_rows):
    t = top_e.shape[0]
    n_assign = t * TOP_K
    n_blocks = -(-n_assign // rows) + N_EXPERTS
    e_flat = top_e.reshape(-1)
    onehot = (e_flat[:, None] == jnp.arange(N_EXPERTS, dtype=jnp.int32)[None, :]).astype(jnp.int32)
    csum = jnp.cumsum(onehot, axis=0)
    rank = jnp.sum((csum - onehot) * onehot, axis=1)
    counts = csum[-1]
    padded = (counts + rows - 1) // rows * rows
    pad_end = jnp.cumsum(padded)
    pad_start = pad_end - padded
    dest = jnp.sum(onehot * pad_start[None, :], axis=1) + rank
    tok = jnp.arange(n_assign, dtype=jnp.int32) // TOP_K
    spread = jnp.arange(n_blocks * rows, dtype=jnp.int32) % t
    row_tok = spread.at[dest].set(tok)
    n_active = pad_end[-1] // rows
    blk = jnp.arange(n_blocks, dtype=jnp.int32)
    blk_start = jnp.minimum(blk, n_active - 1) * rows
    block_expert = jnp.minimum(jnp.sum((blk_start[:, None] >= pad_end[None, :]).astype(jnp.int32), axis=1),
                               N_EXPERTS - 1)
    real_rows = jnp.clip(pad_start[block_expert] + counts[block_expert] - blk_start, 0, rows)
    n_sub = jnp.where(blk < n_active, (real_rows + sub_rows - 1) // sub_rows, 0).astype(jnp.int32)
    return row_tok, dest.reshape(t, TOP_K), block_expert, n_sub


def _expert_kernel(be_ref, ns_ref, x_ref, wg_ref, wu_ref, wd_ref, o_ref, acc_ref, *, sub_rows):
    i = pl.program_id(0)
    f = pl.program_id(1)
    rows = x_ref.shape[0]

    @pl.when(f == 0)
    def _():
        acc_ref[...] = jnp.zeros_like(acc_ref)

    def run(n_rows):
        x = x_ref[0:n_rows, :]
        act = (_silu(_dot(x, wg_ref[0].astype(BF16))) * _dot(x, wu_ref[0].astype(BF16))).astype(BF16)
        acc_ref[0:n_rows, :] += _dot(act, wd_ref[0].astype(BF16))

    for s in range(1, rows // sub_rows + 1):
        pl.when(ns_ref[i] == s)(functools.partial(run, s * sub_rows))

    @pl.when(f == pl.num_programs(1) - 1)
    def _():
        o_ref[...] = acc_ref[...].astype(BF16)


def moe_experts(xg, block_expert, n_sub, w_gate, w_up, w_down, layer, rows, sub_rows, tf=512):
    r_tot, d = xg.shape
    f_dim = w_gate.shape[2]
    tf = min(tf, f_dim)
    nf = f_dim // tf
    n_blocks = r_tot // rows
    e0 = layer * N_EXPERTS

    def f_idx(i, f, ns):
        return jnp.where(ns[i] > 0, f, nf - 1)

    grid_spec = pltpu.PrefetchScalarGridSpec(
        num_scalar_prefetch=2,
        grid=(n_blocks, nf),
        in_specs=[pl.BlockSpec((rows, d), lambda i, f, be, ns: (i, 0), pipeline_mode=pl.Buffered(1)),
                  pl.BlockSpec((1, d, tf), lambda i, f, be, ns: (e0 + be[i], 0, f_idx(i, f, ns))),
                  pl.BlockSpec((1, d, tf), lambda i, f, be, ns: (e0 + be[i], 0, f_idx(i, f, ns))),
                  pl.BlockSpec((1, tf, d), lambda i, f, be, ns: (e0 + be[i], f_idx(i, f, ns), 0))],
        out_specs=pl.BlockSpec((rows, d), lambda i, f, be, ns: (i, 0)),
        scratch_shapes=[pltpu.VMEM((rows, d), F32)],
    )
    return pl.pallas_call(
        functools.partial(_expert_kernel, sub_rows=sub_rows),
        grid_spec=grid_spec,
        out_shape=jax.ShapeDtypeStruct((r_tot, d), BF16),
        compiler_params=_params("arbitrary", "arbitrary"),
        name="moe_experts",
    )(block_expert, n_sub, xg, w_gate, w_up, w_down)


def _combine_kernel(x_ref, y1_ref, y2_ref, r_ref, mod_ref, gf_ref, o_ref, *, final_norm):
    r = r_ref[...]
    ff = r[:, 2:3] * y1_ref[...].astype(F32) + r[:, 3:4] * y2_ref[...].astype(F32)
    x_new = x_ref[...] + mod_ref[0][5:6] * ff
    if final_norm:
        ms = jnp.mean(x_new * x_new, axis=-1, keepdims=True)
        x_new = x_new * lax.rsqrt(ms + EPS) * gf_ref[...]
    o_ref[...] = x_new


def moe_combine(x, y1, y2, route, mod, g_final, rows_per_batch, final_norm, tm=512):
    t, d = x.shape
    tm = _tile(rows_per_batch, tm)
    tpb = rows_per_batch // tm
    row = lambda w: pl.BlockSpec((tm, w), lambda i: (i, 0))
    return pl.pallas_call(
        functools.partial(_combine_kernel, final_norm=final_norm),
        grid=(t // tm,),
        in_specs=[row(d), row(d), row(d), row(LANES),
                  pl.BlockSpec((1, N_MOD, d), lambda i: (i // tpb, 0, 0)),
                  pl.BlockSpec((1, d), lambda i: (0, 0))],
        out_specs=row(d),
        out_shape=jax.ShapeDtypeStruct((t, d), F32),
        compiler_params=_params("parallel"),
        name="moe_combine",
    )(x, y1, y2, route, mod, g_final.reshape(1, d))


def kernel(x, c, w_ada, b_ada, g_mix, g_ffn, g_final, w_in, conv_w, b_igate, b_fgate, mh_gain, w_out,
           ffn_w_gate, ffn_w_up, ffn_w_down, s5_lam_re, s5_lam_im, s5_log_step, s5_b_re, s5_b_im,
           s5_c_re, s5_c_im, s5_d, glu_w_a, glu_w_b, router_w, router_b, exp_w_gate, exp_w_up, exp_w_down):
    batch, seq, d = x.shape
    depth = w_ada.shape[0]
    t = batch * seq
    m_width = mh_gain.shape[1]
    heads_b = (d - m_width) // MOBA_HEAD_DIM
    n_gate = 2 * MLSTM_HEADS

    mods = ada_mod(c, w_ada, b_ada)
    xs = x.reshape(t, d)
    for layer in range(depth):
        i = layer // 2
        mod = mods[layer]
        if layer % 2 == 0:
            w = w_in[i]
            g0 = 4 * m_width
            w_big = jnp.concatenate([w[:, :g0], w[:, g0 + n_gate:]], axis=1).astype(BF16)
            w_gates = _split_cols(jnp.pad(w[:, g0:g0 + n_gate], ((0, 0), (0, LANES - n_gate))))
            proj, gates = in_proj(xs, g_mix[layer], mod, w_big, w_gates, seq)
            hm = mlstm_mix(proj, gates, conv_w[i], b_igate[i], b_fgate[i], mh_gain[i], batch, seq)
            hb = moba_mix(proj, g0 // MOBA_HEAD_DIM, batch, seq, heads_b)
            xs = out_proj(hm, hb, w_out[i].astype(BF16), xs, mod, seq)
            f_pad = -ffn_w_gate.shape[2] % 512
            wg = jnp.pad(ffn_w_gate[i], ((0, 0), (0, f_pad))).astype(BF16)
            wu = jnp.pad(ffn_w_up[i], ((0, 0), (0, f_pad))).astype(BF16)
            wd = jnp.pad(ffn_w_down[i], ((0, f_pad), (0, 0))).astype(BF16)
            xs, u = ffn_swiglu(xs, g_ffn[layer], mod, wg, wu, wd, g_mix[layer + 1], mods[layer + 1], seq)
        else:
            ops = s5_operators(s5_lam_re[i], s5_lam_im[i], s5_log_step[i], s5_b_re[i], s5_b_im[i],
                               s5_c_re[i], s5_c_im[i], seq // S5_SUB)
            g = s5_scan_gelu(u, ops, s5_d[i], batch, seq)
            xs = glu_out(g, glu_w_a[i].astype(BF16), glu_w_b[i].astype(BF16), xs, mod, seq)
            h, route = moe_router(xs, g_ffn[layer], mod, router_w[i], router_b[i], seq)
            top_e = route[:, 0:TOP_K].astype(jnp.int32)
            row_tok, pos, block_expert, n_sub = moe_dispatch(top_e, MOE_ROWS, MOE_SUB_ROWS)
            stack = lambda w_: w_.reshape((-1,) + w_.shape[2:])
            y_rows = moe_experts(h[row_tok], block_expert, n_sub, stack(exp_w_gate), stack(exp_w_up),
                                 stack(exp_w_down), i, MOE_ROWS, MOE_SUB_ROWS)
            xs = moe_combine(xs, y_rows[pos[:, 0]], y_rows[pos[:, 1]], route, mod, g_final, seq,
                             final_norm=(layer == depth - 1))
    if depth % 2 == 1:
        raise NotImplementedError("layers come in (even, odd) pairs: the S5 input and the final norm are fused")
    return xs.reshape(batch, seq, d)
```
